```python
import jax
import jax.numpy as jnp
from jax import lax
import numpy as np

D_MODEL = 4096
BATCH = 8
SEQ = 2048
DEPTH = 4
DEC_BATCH = 8
DEC_SEQ = 16
PAST_LEN = 2048

CHUNK = 64
N_MIXERS = 3
N_ATTN = (DEPTH + 2) // 3
N_POOL = (DEPTH + 1) // 3
N_RWKV = DEPTH // 3
HEAD_DIM = 128
N_HEADS = D_MODEL // HEAD_DIM
N_KV_HEADS = 8
N_GROUPS = N_HEADS // N_KV_HEADS
Q_DIM = N_HEADS * HEAD_DIM
KV_DIM = N_KV_HEADS * HEAD_DIM
WINDOW = 128
N_BAND = WINDOW // CHUNK
ROPE_THETA = 10000.0
POOL_WINDOWS = (2, 4, 8, 16)
N_POOL_GROUPS = 4
POOL_GROUP = D_MODEL // N_POOL_GROUPS
POOL_HIST = 15
RW_HEAD = 64
RW_HEADS = D_MODEL // RW_HEAD
D_DECAY_LORA = 128
D_AAA_LORA = 128
D_GATE_LORA = 480
LNX_EPS = 64e-5
D_FF = 11008
CONV_W = 3
NORM_EPS = 1e-6

kernel_name = "hybrid_streaming_swa_pool_rwkv7_convffn_step"


def rms_norm(x, g):
    xf = x.astype(jnp.float32)
    y = xf * lax.rsqrt(jnp.mean(xf * xf, axis=-1, keepdims=True) + NORM_EPS)
    return (y * g.astype(jnp.float32)).astype(x.dtype)


def rope(x, pos):
    half = HEAD_DIM // 2
    inv = ROPE_THETA ** (-jnp.arange(half, dtype=jnp.float32) / half)
    ang = pos.astype(jnp.float32)[:, None] * inv[None, :]
    cos = jnp.cos(ang)[None, :, None, :]
    sin = jnp.sin(ang)[None, :, None, :]
    xf = x.astype(jnp.float32)
    x1, x2 = xf[..., :half], xf[..., half:]
    return jnp.concatenate([x1 * cos - x2 * sin, x2 * cos + x1 * sin], axis=-1).astype(x.dtype)


def sink_attention(q, k, v, sinks, mask):
    s = jnp.einsum('bnqhgd,bnshd->bnhgqs', q.astype(jnp.float32), k.astype(jnp.float32)) * HEAD_DIM ** -0.5
    s = jnp.where(mask[None, :, None, None], s, -1e30)
    sink = sinks.astype(jnp.float32).reshape(1, 1, N_KV_HEADS, N_GROUPS, 1, 1)
    m = jnp.maximum(jnp.max(s, axis=-1, keepdims=True), sink)
    p = jnp.exp(s - m)
    denom = jnp.sum(p, axis=-1, keepdims=True) + jnp.exp(sink - m)
    o = jnp.einsum('bnhgqs,bnshd->bnqhgd', p / denom, v.astype(jnp.float32))
    return o.astype(q.dtype)


def attn_qkv(xn, w_qkv, pos):
    b, s, _ = xn.shape
    qkv = xn @ w_qkv
    q = qkv[..., :Q_DIM].reshape(b, s, N_HEADS, HEAD_DIM)
    k = qkv[..., Q_DIM:Q_DIM + KV_DIM].reshape(b, s, N_KV_HEADS, HEAD_DIM)
    v = qkv[..., Q_DIM + KV_DIM:].reshape(b, s, N_KV_HEADS, HEAD_DIM)
    return rope(q, pos), rope(k, pos), v


def attn_prompt(xn, w_qkv, w_o, sinks):
    b, s, _ = xn.shape
    nc = s // CHUNK
    q, k, v = attn_qkv(xn, w_qkv, jnp.arange(s))

    def band(t):
        tp = jnp.pad(t, ((0, 0), (N_BAND * CHUNK, 0), (0, 0), (0, 0)))
        tp = tp.reshape(b, nc + N_BAND, CHUNK, N_KV_HEADS, HEAD_DIM)
        return jnp.concatenate([tp[:, j:j + nc] for j in range(N_BAND + 1)], axis=2)

    kb, vb = band(k), band(v)
    blk = jnp.arange(nc)[:, None] + jnp.arange(N_BAND + 1)[None, :] - N_BAND
    mask = jnp.repeat(blk >= 0, CHUNK, axis=1)[:, None, :]
    qb = q.reshape(b, nc, CHUNK, N_KV_HEADS, N_GROUPS, HEAD_DIM)
    o = sink_attention(qb, kb, vb, sinks, mask).reshape(b, s, Q_DIM)
    return o @ w_o, k[:, s - WINDOW:], v[:, s - WINDOW:]


def attn_sample(xn, cache_k, cache_v, w_qkv, w_o, sinks):
    b, t, _ = xn.shape
    q, k, v = attn_qkv(xn, w_qkv, PAST_LEN + jnp.arange(t))
    kf = jnp.concatenate([cache_k.astype(k.dtype), k], axis=1)[:, None]
    vf = jnp.concatenate([cache_v.astype(v.dtype), v], axis=1)[:, None]
    mask = jnp.ones((1, 1, WINDOW + t), dtype=bool)
    qb = q.reshape(b, 1, t, N_KV_HEADS, N_GROUPS, HEAD_DIM)
    o = sink_attention(qb, kf, vf, sinks, mask).reshape(b, t, Q_DIM)
    return o @ w_o, k, v


def pool_mixer(u, hist, pos0, w_pool, pool_scale):
    b, s, _ = u.shape
    uf = u.astype(jnp.float32)
    padded = jnp.concatenate([hist.astype(u.dtype), u], axis=1)
    cs = jnp.concatenate([jnp.zeros((b, 1, D_MODEL), jnp.float32),
                          jnp.cumsum(padded.astype(jnp.float32), axis=1)], axis=1)
    end = cs[:, POOL_HIST + 1:]
    pos = pos0 + jnp.arange(s)
    outs = []
    for g, w in enumerate(POOL_WINDOWS):
        lo, hi = g * POOL_GROUP, (g + 1) * POOL_GROUP
        start = cs[:, POOL_HIST + 1 - w:POOL_HIST + 1 - w + s, lo:hi]
        cnt = jnp.minimum(w, pos + 1).astype(jnp.float32)[None, :, None]
        outs.append((end[..., lo:hi] - start) / cnt - uf[..., lo:hi])
    d = jnp.stack(outs, axis=2).astype(u.dtype)
    y = jnp.einsum('bsgc,gcd->bsgd', d, w_pool).reshape(b, s, D_MODEL)
    return y * pool_scale, padded[:, s:]


def rwkv_mixer(u, shift, wkv, mix, w_r, w_k, w_v, w_o, w0, w1, w2, a0, a1, a2, g1, g2, k_k, k_a, r_k, lnx):
    b, s, _ = u.shape
    prev = jnp.concatenate([shift.astype(u.dtype), u[:, :-1]], axis=1)
    xx = prev - u
    xr, xw, xk, xv, xa, xg = [u + xx * mix[j] for j in range(6)]
    r = xr @ w_r
    k = xk @ w_k
    v = xv @ w_v
    logw = -jax.nn.softplus(-(w0 + jnp.tanh(xw @ w1) @ w2)) - 0.5
    a = jax.nn.sigmoid(a0 + (xa @ a1) @ a2)
    g = jax.nn.sigmoid(xg @ g1) @ g2

    def heads(t):
        return t.reshape(b, s, RW_HEADS, RW_HEAD).astype(jnp.float32)

    kk = heads(k * k_k)
    kk = kk / jnp.maximum(jnp.sqrt(jnp.sum(kk * kk, axis=-1, keepdims=True)), 1e-12)
    k = k * (1 + (a - 1) * k_a)
    r_h, k_h, v_h, a_h = heads(r), heads(k), heads(v), heads(a)
    decay = jnp.exp(-jnp.exp(heads(logw)))

    def step(st, inp):
        r_t, w_t, k_t, v_t, kk_t, a_t = inp
        sa = jnp.einsum('bhij,bhj->bhi', st, -kk_t)
        st = st * w_t[:, :, None, :] + sa[..., None] * (kk_t * a_t)[:, :, None, :] + v_t[..., None] * k_t[:, :, None, :]
        return st, jnp.einsum('bhij,bhj->bhi', st, r_t)

    seqs = tuple(jnp.moveaxis(t, 1, 0) for t in (r_h, decay, k_h, v_h, kk, a_h))
    wkv_new, o = lax.scan(step, wkv.astype(jnp.float32), seqs)
    o = jnp.moveaxis(o, 0, 1)
    mu = jnp.mean(o, axis=-1, keepdims=True)
    var = jnp.mean(jnp.square(o - mu), axis=-1, keepdims=True)
    o = ((o - mu) * lax.rsqrt(var + LNX_EPS)).reshape(b, s, D_MODEL) * lnx[0].astype(jnp.float32) + lnx[1].astype(jnp.float32)
    bonus = jnp.sum(r_h * k_h * r_k.astype(jnp.float32), axis=-1, keepdims=True) * v_h
    o = (o + bonus.reshape(b, s, D_MODEL)).astype(u.dtype)
    return (o * g) @ w_o, u[:, -1:], wkv_new.astype(wkv.dtype)


def conv_ffn(h, hist, w_up, conv_w, conv_b, w_down):
    s = h.shape[1]
    up = h @ w_up
    padded = jnp.concatenate([hist.astype(up.dtype), up], axis=1)
    c = conv_b
    for j in range(CONV_W):
        c = c + padded[:, j:j + s] * conv_w[j]
    gate, val = c[..., :D_FF], c[..., D_FF:]
    y = (jax.nn.gelu(gate, approximate=True) * val) @ w_down
    return y, padded[:, s:]


def setup_inputs(seed: int = 0) -> dict:
    key = jax.random.key(seed)
    ks = iter(jax.random.split(key, 48))
    D = D_MODEL

    def nrm(shape, scale=1.0):
        return jax.random.normal(next(ks), shape, jnp.float32) * scale

    return {
        "x_prompt": nrm((BATCH, SEQ, D)),
        "x_sample": nrm((DEC_BATCH, DEC_SEQ, D)),
        "cache_k": nrm((N_ATTN, DEC_BATCH, WINDOW, N_KV_HEADS, HEAD_DIM)),
        "cache_v": nrm((N_ATTN, DEC_BATCH, WINDOW, N_KV_HEADS, HEAD_DIM)),
        "state_pool": nrm((N_POOL, DEC_BATCH, POOL_HIST, D)),
        "state_shift": nrm((N_RWKV, DEC_BATCH, 1, D)),
        "state_wkv": nrm((N_RWKV, DEC_BATCH, RW_HEADS, RW_HEAD, RW_HEAD), 0.1),
        "state_conv": nrm((DEPTH, DEC_BATCH, CONV_W - 1, 2 * D_FF)),
        "norm_g": 1.0 + nrm((DEPTH, 4, D), 0.1),
        "attn_w_qkv": nrm((N_ATTN, D, Q_DIM + 2 * KV_DIM), D ** -0.5),
        "attn_w_o": nrm((N_ATTN, Q_DIM, D), Q_DIM ** -0.5),
        "attn_sinks": nrm((N_ATTN, N_HEADS)),
        "pool_w": nrm((N_POOL, N_POOL_GROUPS, POOL_GROUP, POOL_GROUP), POOL_GROUP ** -0.5),
        "pool_scale": 1.0 + nrm((N_POOL, D), 0.1),
        "rw_mix": jax.random.uniform(next(ks), (N_RWKV, 6, D), jnp.float32),
        "rw_w_r": nrm((N_RWKV, D, D), D ** -0.5),
        "rw_w_k": nrm((N_RWKV, D, D), D ** -0.5),
        "rw_w_v": nrm((N_RWKV, D, D), D ** -0.5),
        "rw_w_o": nrm((N_RWKV, D, D), D ** -0.5),
        "rw_w0": nrm((N_RWKV, D), 0.5),
        "rw_w1": nrm((N_RWKV, D, D_DECAY_LORA), D ** -0.5),
        "rw_w2": nrm((N_RWKV, D_DECAY_LORA, D), 0.5 * D_DECAY_LORA ** -0.5),
        "rw_a0": nrm((N_RWKV, D), 0.1),
        "rw_a1": nrm((N_RWKV, D, D_AAA_LORA), D ** -0.5),
        "rw_a2": nrm((N_RWKV, D_AAA_LORA, D), D_AAA_LORA ** -0.5),
        "rw_g1": nrm((N_RWKV, D, D_GATE_LORA), D ** -0.5),
        "rw_g2": nrm((N_RWKV, D_GATE_LORA, D), D_GATE_LORA ** -0.5),
        "rw_k_k": 0.85 + nrm((N_RWKV, D), 0.05),
        "rw_k_a": 1.0 + nrm((N_RWKV, D), 0.05),
        "rw_r_k": nrm((N_RWKV, RW_HEADS, RW_HEAD), 0.1),
        "rw_lnx": jnp.stack([1.0 + nrm((N_RWKV, D), 0.1), nrm((N_RWKV, D), 0.01)], axis=1),
        "ffn_w_up": nrm((DEPTH, D, 2 * D_FF), D ** -0.5),
        "ffn_conv_w": nrm((DEPTH, CONV_W, 2 * D_FF), CONV_W ** -0.5),
        "ffn_conv_b": nrm((DEPTH, 2 * D_FF), 0.01),
        "ffn_w_down": nrm((DEPTH, D_FF, D), D_FF ** -0.5),
    }


def reference(x_prompt, x_sample, cache_k, cache_v, state_pool, state_shift, state_wkv, state_conv,
              norm_g, attn_w_qkv, attn_w_o, attn_sinks, pool_w, pool_scale,
              rw_mix, rw_w_r, rw_w_k, rw_w_v, rw_w_o, rw_w0, rw_w1, rw_w2, rw_a0, rw_a1, rw_a2,
              rw_g1, rw_g2, rw_k_k, rw_k_a, rw_r_k, rw_lnx,
              ffn_w_up, ffn_conv_w, ffn_conv_b, ffn_w_down):
    xp, xs = x_prompt, x_sample
    bp = xp.shape[0]
    dt = xp.dtype
    kp_l, vp_l, ks_l, vs_l = [], [], [], []
    poolp_l, pools_l, shp_l, shs_l, wkvp_l, wkvs_l, convp_l, convs_l = [], [], [], [], [], [], [], []
    for i in range(DEPTH):
        kind, j = i % N_MIXERS, i // N_MIXERS
        un_p = rms_norm(xp, norm_g[i, 0])
        un_s = rms_norm(xs, norm_g[i, 0])
        if kind == 0:
            mp, kp, vp = attn_prompt(un_p, attn_w_qkv[j], attn_w_o[j], attn_sinks[j])
            ms, kn, vn = attn_sample(un_s, cache_k[j], cache_v[j], attn_w_qkv[j], attn_w_o[j], attn_sinks[j])
            kp_l.append(kp)
            vp_l.append(vp)
            ks_l.append(kn)
            vs_l.append(vn)
        elif kind == 1:
            mp, hp = pool_mixer(un_p, jnp.zeros((bp, POOL_HIST, D_MODEL), dt), 0, pool_w[j], pool_scale[j])
            ms, hs = pool_mixer(un_s, state_pool[j], PAST_LEN, pool_w[j], pool_scale[j])
            poolp_l.append(hp)
            pools_l.append(hs)
        else:
            rw = (rw_mix[j], rw_w_r[j], rw_w_k[j], rw_w_v[j], rw_w_o[j], rw_w0[j], rw_w1[j], rw_w2[j],
                  rw_a0[j], rw_a1[j], rw_a2[j], rw_g1[j], rw_g2[j], rw_k_k[j], rw_k_a[j], rw_r_k[j], rw_lnx[j])
            mp, shp, wkvp = rwkv_mixer(un_p, jnp.zeros((bp, 1, D_MODEL), dt),
                                       jnp.zeros((bp, RW_HEADS, RW_HEAD, RW_HEAD), dt), *rw)
            ms, shs, wkvs = rwkv_mixer(un_s, state_shift[j], state_wkv[j], *rw)
            shp_l.append(shp)
            shs_l.append(shs)
            wkvp_l.append(wkvp)
            wkvs_l.append(wkvs)
        xp = xp + rms_norm(mp, norm_g[i, 1])
        xs = xs + rms_norm(ms, norm_g[i, 1])
        fp, cp = conv_ffn(rms_norm(xp, norm_g[i, 2]), jnp.zeros((bp, CONV_W - 1, 2 * D_FF), dt),
                          ffn_w_up[i], ffn_conv_w[i], ffn_conv_b[i], ffn_w_down[i])
        fs, cs = conv_ffn(rms_norm(xs, norm_g[i, 2]), state_conv[i],
                          ffn_w_up[i], ffn_conv_w[i], ffn_conv_b[i], ffn_w_down[i])
        convp_l.append(cp)
        convs_l.append(cs)
        xp = xp + rms_norm(fp, norm_g[i, 3])
        xs = xs + rms_norm(fs, norm_g[i, 3])
    return (xp, xs,
            jnp.stack(kp_l), jnp.stack(vp_l), jnp.stack(poolp_l), jnp.stack(shp_l), jnp.stack(wkvp_l), jnp.stack(convp_l),
            jnp.stack(ks_l), jnp.stack(vs_l), jnp.stack(pools_l), jnp.stack(shs_l), jnp.stack(wkvs_l), jnp.stack(convs_l))
```

```python
import functools
import math

import jax
import jax.numpy as jnp
from jax import lax
from jax.experimental import pallas as pl
from jax.experimental.pallas import tpu as pltpu

BF16 = jnp.bfloat16
F32 = jnp.float32

CHUNK = 64
HEAD_DIM = 128
N_GROUPS = 4
ROPE_THETA = 10000.0
POOL_WINDOWS = (2, 4, 8, 16)
POOL_HIST = 15
RW_HEAD = 64
LNX_EPS = 64e-5
NORM_EPS = 1e-6
PAST_LEN = 2048
CONV_W = 3

LANES = 128
VMEM_LIMIT_BYTES = 56 * 1024 * 1024
HALO_ROWS = 16


def _params(*sem):
    return pltpu.CompilerParams(dimension_semantics=sem, vmem_limit_bytes=VMEM_LIMIT_BYTES)


def _pick(n, prefs):
    for p in prefs:
        if n % p == 0:
            return p
    return n


def _rms(xf, g):
    return xf * lax.rsqrt(jnp.mean(xf * xf, axis=-1, keepdims=True) + NORM_EPS) * g


def _seg_ones(rows):
    r = lax.broadcasted_iota(jnp.int32, (rows, LANES), 0)
    c = lax.broadcasted_iota(jnp.int32, (rows, LANES), 1)
    return ((r % LANES) // RW_HEAD == c // RW_HEAD).astype(BF16)


def _seg_sum(p, jj):
    hi = p.astype(BF16)
    lo = (p - hi.astype(F32)).astype(BF16)
    return jnp.dot(jnp.concatenate([hi, lo], axis=1), jj, preferred_element_type=F32)


def _mm_kernel(*refs, norm, rope_blocks, act):
    it = iter(refs)
    x_ref = next(it)
    g_ref = next(it) if norm else None
    w_ref = next(it)
    cos_ref = next(it) if rope_blocks else None
    sin_ref = next(it) if rope_blocks else None
    o_ref = next(it)
    xn_ref = next(it) if norm else None
    j = pl.program_id(1)

    if norm:
        @pl.when(j == 0)
        def _():
            xn_ref[...] = _rms(x_ref[...].astype(F32), g_ref[...]).astype(BF16)
        lhs = xn_ref[...]
    else:
        lhs = x_ref[...]
    acc = jnp.dot(lhs, w_ref[...], preferred_element_type=F32)
    if act == "tanh":
        acc = jnp.tanh(acc)
    elif act == "sigmoid":
        acc = jax.nn.sigmoid(acc)

    if rope_blocks:
        @pl.when(j < rope_blocks)
        def _():
            cos = cos_ref[...]
            sin = sin_ref[...]
            for h in range(acc.shape[1] // HEAD_DIM):
                xh = acc[:, h * HEAD_DIM:(h + 1) * HEAD_DIM]
                o_ref[:, h * HEAD_DIM:(h + 1) * HEAD_DIM] = (
                    xh * cos + pltpu.roll(xh, HEAD_DIM // 2, 1) * sin).astype(o_ref.dtype)

        @pl.when(j >= rope_blocks)
        def _():
            o_ref[...] = acc.astype(o_ref.dtype)
    else:
        o_ref[...] = acc.astype(o_ref.dtype)


def _mm(x, w, out_dtype, *, g=None, rope=None, act=None):
    m, k = x.shape
    n = w.shape[1]
    norm = g is not None
    if norm or k > 8192:
        tm = min(512, m)
    else:
        tm = min(1024, m)
    tn = _pick(math.gcd(n, rope[2]) if rope else n, (256,) if k > 8192 else (512, 256, 128))
    rope_blocks = 0
    in_specs = [pl.BlockSpec((tm, k), lambda i, j: (i, 0))]
    args = [x]
    if norm:
        in_specs.append(pl.BlockSpec((1, k), lambda i, j: (0, 0)))
        args.append(g.reshape(1, k).astype(F32))
    in_specs.append(pl.BlockSpec((k, tn), lambda i, j: (0, j)))
    args.append(w)
    if rope is not None:
        cos, sin, n_cols = rope
        assert n_cols % tn == 0 and tn % HEAD_DIM == 0
        rope_blocks = n_cols // tn
        in_specs += [pl.BlockSpec((tm, HEAD_DIM), lambda i, j: (i, 0))] * 2
        args += [cos, sin]
    scratch = [pltpu.VMEM((tm, k), BF16)] if norm else []
    return pl.pallas_call(
        functools.partial(_mm_kernel, norm=norm, rope_blocks=rope_blocks, act=act),
        out_shape=jax.ShapeDtypeStruct((m, n), out_dtype),
        grid=(m // tm, n // tn),
        in_specs=in_specs,
        out_specs=pl.BlockSpec((tm, tn), lambda i, j: (i, j)),
        scratch_shapes=scratch,
        compiler_params=_params("arbitrary", "arbitrary"),
        name="mm",
    )(*args)


def _add_norm_kernel(x_ref, m_ref, g_ref, o_ref):
    o_ref[...] = x_ref[...] + _rms(m_ref[...].astype(F32), g_ref[...])


def _add_norm(x, mix, g):
    m, d = x.shape
    tr = min(256, m)
    return pl.pallas_call(
        _add_norm_kernel,
        out_shape=jax.ShapeDtypeStruct((m, d), F32),
        grid=(m // tr,),
        in_specs=[pl.BlockSpec((tr, d), lambda i: (i, 0)),
                  pl.BlockSpec((tr, d), lambda i: (i, 0)),
                  pl.BlockSpec((1, d), lambda i: (0, 0))],
        out_specs=pl.BlockSpec((tr, d), lambda i: (i, 0)),
        compiler_params=_params("arbitrary"),
        name="add_norm",
    )(x, mix, g.reshape(1, d))


def _attn_kernel(sink_ref, q_ref, *refs, n_pieces, n_kv, band_mask):
    k_refs = refs[:n_pieces]
    v_refs = refs[n_pieces:2 * n_pieces]
    o_ref = refs[2 * n_pieces]
    tq = q_ref.shape[0]
    c = pl.program_id(1)
    scale = HEAD_DIM ** -0.5
    for kv in range(n_kv):
        lo, hi = kv * HEAD_DIM, (kv + 1) * HEAD_DIM
        q4 = jnp.concatenate(
            [q_ref[:, (kv * N_GROUPS + r) * HEAD_DIM:(kv * N_GROUPS + r + 1) * HEAD_DIM] for r in range(N_GROUPS)],
            axis=0)
        kb = jnp.concatenate([kr[:, lo:hi].astype(BF16) for kr in k_refs], axis=0)
        vb = jnp.concatenate([vr[:, lo:hi].astype(BF16) for vr in v_refs], axis=0)
        s = lax.dot_general(q4, kb, (((1,), (1,)), ((), ())), preferred_element_type=F32) * scale
        if band_mask:
            col = lax.broadcasted_iota(jnp.int32, s.shape, 1)
            s = jnp.where(col >= CHUNK * (n_pieces - 1 - c), s, -1e30)
        sink = jnp.concatenate(
            [jnp.full((tq, 1), sink_ref[kv * N_GROUPS + r], F32) for r in range(N_GROUPS)], axis=0)
        mx = jnp.maximum(jnp.max(s, axis=-1, keepdims=True), sink)
        p = jnp.exp(s - mx)
        denom = jnp.sum(p, axis=-1, keepdims=True) + jnp.exp(sink - mx)
        o = jnp.dot((p / denom).astype(BF16), vb, preferred_element_type=F32)
        for r in range(N_GROUPS):
            h = kv * N_GROUPS + r
            o_ref[:, h * HEAD_DIM:(h + 1) * HEAD_DIM] = o[r * tq:(r + 1) * tq].astype(o_ref.dtype)


def _attn_prompt(qkv, sinks, b, s, d):
    kv_dim = d // N_GROUPS
    n_kv = kv_dim // HEAD_DIM
    nc = s // CHUNK
    n_band = 3
    kcol, vcol = d // kv_dim, d // kv_dim + 1

    def piece(jj, col):
        return pl.BlockSpec((CHUNK, kv_dim),
                            lambda bi, c, sk: (bi * nc + jnp.maximum(c - (n_band - 1) + jj, 0), col))

    in_specs = [pl.BlockSpec((CHUNK, d), lambda bi, c, sk: (bi * nc + c, 0))]
    in_specs += [piece(jj, kcol) for jj in range(n_band)] + [piece(jj, vcol) for jj in range(n_band)]
    return pl.pallas_call(
        functools.partial(_attn_kernel, n_pieces=n_band, n_kv=n_kv, band_mask=True),
        out_shape=jax.ShapeDtypeStruct((b * s, d), BF16),
        grid_spec=pltpu.PrefetchScalarGridSpec(
            num_scalar_prefetch=1, grid=(b, nc), in_specs=in_specs,
            out_specs=pl.BlockSpec((CHUNK, d), lambda bi, c, sk: (bi * nc + c, 0))),
        compiler_params=_params("arbitrary", "arbitrary"),
        name="attn_prompt",
    )(sinks, *([qkv] * (1 + 2 * n_band)))


def _attn_sample(qkv, cache_k, cache_v, sinks, b, t, d):
    kv_dim = d // N_GROUPS
    n_kv = kv_dim // HEAD_DIM
    win = cache_k.shape[1]
    kcol, vcol = d // kv_dim, d // kv_dim + 1
    in_specs = [
        pl.BlockSpec((t, d), lambda bi, c, sk: (bi, 0)),
        pl.BlockSpec((None, win, kv_dim), lambda bi, c, sk: (bi, 0, 0)),
        pl.BlockSpec((t, kv_dim), lambda bi, c, sk: (bi, kcol)),
        pl.BlockSpec((None, win, kv_dim), lambda bi, c, sk: (bi, 0, 0)),
        pl.BlockSpec((t, kv_dim), lambda bi, c, sk: (bi, vcol)),
    ]
    return pl.pallas_call(
        functools.partial(_attn_kernel, n_pieces=2, n_kv=n_kv, band_mask=False),
        out_shape=jax.ShapeDtypeStruct((b * t, d), BF16),
        grid_spec=pltpu.PrefetchScalarGridSpec(
            num_scalar_prefetch=1, grid=(b, 1), in_specs=in_specs,
            out_specs=pl.BlockSpec((t, d), lambda bi, c, sk: (bi, 0))),
        compiler_params=_params("arbitrary", "arbitrary"),
        name="attn_sample",
    )(sinks, qkv, cache_k, qkv, cache_v, qkv)


def _pool_kernel(x_ref, halo_ref, g_ref, w_ref, sc_ref, o_ref, st_ref, *, halo_is_x, pos0):
    si = pl.program_id(1)
    ts, d = x_ref.shape
    gw = d // len(POOL_WINDOWS)
    g = g_ref[...]
    un = _rms(x_ref[...], g)
    if halo_is_x:
        halo = jnp.where(si > 0, _rms(halo_ref[...], g), 0.0)
    else:
        halo = halo_ref[...]
    full = jnp.concatenate([halo, un], axis=0)
    pos = pos0 + si * ts + lax.broadcasted_iota(jnp.int32, (ts, 1), 0)
    for gi, win in enumerate(POOL_WINDOWS):
        lo, hi = gi * gw, (gi + 1) * gw
        acc = full[:, lo:hi]
        span = 1
        while span < win:
            acc = acc + pltpu.roll(acc, span, 0)
            span *= 2
        cnt = jnp.minimum(win, pos + 1).astype(F32)
        dlt = (acc[HALO_ROWS:] / cnt - un[:, lo:hi]).astype(BF16)
        y = jnp.dot(dlt, w_ref[gi], preferred_element_type=F32)
        o_ref[:, lo:hi] = (y * sc_ref[:, lo:hi]).astype(o_ref.dtype)
    st_ref[...] = full[ts:]


def _pool(x, halo, g, w, scale, b, s, *, halo_is_x, pos0):
    d = x.shape[1]
    ts = min(256, s)
    ns = s // ts
    hb = ts // HALO_ROWS
    if halo_is_x:
        halo_spec = pl.BlockSpec((HALO_ROWS, d), lambda bi, si: (jnp.maximum((bi * ns + si) * hb - 1, 0), 0))
    else:
        halo_spec = pl.BlockSpec((None, HALO_ROWS, d), lambda bi, si: (bi, 0, 0))
    ng, gw = w.shape[0], w.shape[1]
    return pl.pallas_call(
        functools.partial(_pool_kernel, halo_is_x=halo_is_x, pos0=pos0),
        out_shape=(jax.ShapeDtypeStruct((b * s, d), BF16), jax.ShapeDtypeStruct((b, HALO_ROWS, d), F32)),
        grid=(b, ns),
        in_specs=[pl.BlockSpec((ts, d), lambda bi, si: (bi * ns + si, 0)),
                  halo_spec,
                  pl.BlockSpec((1, d), lambda bi, si: (0, 0)),
                  pl.BlockSpec((ng, gw, gw), lambda bi, si: (0, 0, 0)),
                  pl.BlockSpec((1, d), lambda bi, si: (0, 0))],
        out_specs=(pl.BlockSpec((ts, d), lambda bi, si: (bi * ns + si, 0)),
                   pl.BlockSpec((None, HALO_ROWS, d), lambda bi, si: (bi, 0, 0))),
        compiler_params=_params("arbitrary", "arbitrary"),
        name="pool",
    )(x, halo, g.reshape(1, d), w, scale.reshape(1, d))


def _rw_mix_kernel(x_ref, halo_ref, g_ref, mix_ref, *o_refs, halo_is_x):
    si = pl.program_id(1)
    g = g_ref[...]
    un = _rms(x_ref[...], g)
    n_h = halo_ref.shape[0]
    if halo_is_x:
        prev_row = jnp.where(si > 0, _rms(halo_ref[n_h - 1:n_h, :], g), 0.0)
    else:
        prev_row = halo_ref[n_h - 1:n_h, :]
    row = lax.broadcasted_iota(jnp.int32, (un.shape[0], 1), 0)
    prev = jnp.where(row == 0, prev_row, pltpu.roll(un, 1, 0))
    xx = prev - un
    for jm in range(6):
        o_refs[jm][...] = (un + xx * mix_ref[jm:jm + 1, :]).astype(BF16)
    st_ref = o_refs[6]
    st_ref[...] = un[un.shape[0] - st_ref.shape[0]:]


def _rw_mix(x, halo, g, mix, b, s, *, halo_is_x):
    d = x.shape[1]
    ts = min(256, s)
    ns = s // ts
    hr = 8
    if halo_is_x:
        halo_spec = pl.BlockSpec((hr, d), lambda bi, si: (jnp.maximum((bi * ns + si) * (ts // hr) - 1, 0), 0))
    else:
        halo_spec = pl.BlockSpec((None, hr, d), lambda bi, si: (bi, 0, 0))
    row_spec = pl.BlockSpec((ts, d), lambda bi, si: (bi * ns + si, 0))
    return pl.pallas_call(
        functools.partial(_rw_mix_kernel, halo_is_x=halo_is_x),
        out_shape=tuple([jax.ShapeDtypeStruct((b * s, d), BF16)] * 6 + [jax.ShapeDtypeStruct((b, hr, d), F32)]),
        grid=(b, ns),
        in_specs=[row_spec, halo_spec,
                  pl.BlockSpec((1, d), lambda bi, si: (0, 0)),
                  pl.BlockSpec((8, d), lambda bi, si: (0, 0))],
        out_specs=tuple([row_spec] * 6 + [pl.BlockSpec((None, hr, d), lambda bi, si: (bi, 0, 0))]),
        compiler_params=_params("arbitrary", "arbitrary"),
        name="rw_mix",
    )(x, halo, g.reshape(1, d), jnp.pad(mix, ((0, 2), (0, 0))))


def _rw_prep_kernel(k_ref, lw_ref, la_ref, w0_ref, a0_ref, kk_ref, ka_ref,
                    kk_o, k2_o, dec_o, kka_o):
    k = k_ref[...].astype(F32)
    a = jax.nn.sigmoid(a0_ref[...] + la_ref[...].astype(F32))
    dec_o[...] = jnp.exp(-math.exp(-0.5) * jax.nn.sigmoid(w0_ref[...] + lw_ref[...].astype(F32)))
    k2_o[...] = k * (1.0 + (a - 1.0) * ka_ref[...])
    kkr = k * kk_ref[...]
    sq = kkr * kkr
    jj = _seg_ones(2 * LANES)
    for c in range(k.shape[1] // LANES):
        lo, hi = c * LANES, (c + 1) * LANES
        ss = _seg_sum(sq[:, lo:hi], jj)
        kkn = kkr[:, lo:hi] / jnp.maximum(jnp.sqrt(ss), 1e-12)
        kk_o[:, lo:hi] = kkn
        kka_o[:, lo:hi] = kkn * a[:, lo:hi]


def _rw_prep(k, lw, la, w0, a0, k_k, k_a):
    m, d = k.shape
    tr = min(256, m)
    row = pl.BlockSpec((tr, d), lambda i: (i, 0))
    vec = pl.BlockSpec((1, d), lambda i: (0, 0))
    return pl.pallas_call(
        _rw_prep_kernel,
        out_shape=tuple([jax.ShapeDtypeStruct((m, d), F32)] * 4),
        grid=(m // tr,),
        in_specs=[row, row, row, vec, vec, vec, vec],
        out_specs=tuple([row] * 4),
        compiler_params=_params("arbitrary"),
        name="rw_prep",
    )(k, lw, la, w0.reshape(1, d), a0.reshape(1, d), k_k.reshape(1, d), k_a.reshape(1, d))


def _rw_scan_kernel(r_ref, w_ref, k_ref, v_ref, kk_ref, kka_ref, s0_ref, y_ref, sT_ref, st_ref):
    ci = pl.program_id(1)
    tsteps, ntile, _ = r_ref.shape
    n = RW_HEAD

    @pl.when(ci == 0)
    def _():
        st_ref[...] = s0_ref[...]

    jj = _seg_ones(2 * LANES)
    j1 = _seg_ones(LANES)
    ri = lax.broadcasted_iota(jnp.int32, (n, LANES), 0)
    li = lax.broadcasted_iota(jnp.int32, (n, LANES), 1)
    eye = ri == li % n

    def bcast(ref, t, c):
        return jnp.broadcast_to(ref[t, c:c + 1, :].astype(F32), (n, LANES))

    def step(t, carry):
        for c in range(ntile):
            rows = pl.ds(c * n, n)
            st = st_ref[rows, :]
            kk = bcast(kk_ref, t, c)
            sa = -_seg_sum(st * kk, jj)
            vb = jnp.dot(jnp.where(eye, bcast(v_ref, t, c), 0.0).astype(BF16), j1,
                         preferred_element_type=F32)
            st = st * bcast(w_ref, t, c) + sa * bcast(kka_ref, t, c) + vb * bcast(k_ref, t, c)
            st_ref[rows, :] = st
            yb = _seg_sum(st * bcast(r_ref, t, c), jj)
            y_ref[t, c:c + 1, :] = jnp.sum(jnp.where(eye, yb, 0.0), axis=0, keepdims=True)
        return carry

    lax.fori_loop(0, tsteps, step, 0)
    sT_ref[...] = st_ref[...]


def _rw_scan(r, w, k, v, kk, kka, s0, b, s):
    d = r.shape[1]
    ntile = d // LANES
    tc = min(64, s)
    nch = s // tc
    seq = pl.BlockSpec((tc, ntile, LANES), lambda bi, ci: (bi * nch + ci, 0, 0))
    st = pl.BlockSpec((None, ntile * RW_HEAD, LANES), lambda bi, ci: (bi, 0, 0))
    y, s_t = pl.pallas_call(
        _rw_scan_kernel,
        out_shape=(jax.ShapeDtypeStruct((b * s, ntile, LANES), F32),
                   jax.ShapeDtypeStruct(s0.shape, F32)),
        grid=(b, nch),
        in_specs=[seq] * 6 + [st],
        out_specs=(seq, st),
        scratch_shapes=[pltpu.VMEM((ntile * RW_HEAD, LANES), F32)],
        compiler_params=_params("arbitrary", "arbitrary"),
        name="rw_scan",
    )(*[t.reshape(b * s, ntile, LANES) for t in (r, w, k, v, kk, kka)], s0)
    return y.reshape(b * s, d), s_t


def _rw_post_kernel(y_ref, r_ref, k_ref, v_ref, g_ref, ln_ref, rk_ref, o_ref):
    jj = _seg_ones(2 * LANES)
    for c in range(y_ref.shape[1] // LANES):
        sl = slice(c * LANES, (c + 1) * LANES)
        y = y_ref[:, sl]
        mu = _seg_sum(y, jj) * (1.0 / RW_HEAD)
        yc = y - mu
        var = _seg_sum(yc * yc, jj) * (1.0 / RW_HEAD)
        o = yc * lax.rsqrt(var + LNX_EPS) * ln_ref[0:1, sl] + ln_ref[1:2, sl]
        rk = r_ref[:, sl].astype(F32) * k_ref[:, sl] * rk_ref[:, sl]
        bonus = _seg_sum(rk, jj) * v_ref[:, sl].astype(F32)
        o_ref[:, sl] = ((o + bonus) * g_ref[:, sl].astype(F32)).astype(o_ref.dtype)


def _rw_post(y, r, k2, v, gate, lnx, r_k):
    m, d = y.shape
    tr = min(256, m)
    row = pl.BlockSpec((tr, d), lambda i: (i, 0))
    return pl.pallas_call(
        _rw_post_kernel,
        out_shape=jax.ShapeDtypeStruct((m, d), BF16),
        grid=(m // tr,),
        in_specs=[row] * 5 + [pl.BlockSpec((8, d), lambda i: (0, 0)), pl.BlockSpec((1, d), lambda i: (0, 0))],
        out_specs=row,
        compiler_params=_params("arbitrary"),
        name="rw_post",
    )(y, r, k2, v, gate, jnp.pad(lnx, ((0, 6), (0, 0))), r_k.reshape(1, d))


def _conv_act_kernel(ug_ref, uv_ref, hg_ref, hv_ref, cwg_ref, cwv_ref, cbg_ref, cbv_ref, o_ref, *, halo_is_up):
    si = pl.program_id(1)

    def conv(u_ref, h_ref, cw_ref, cb_ref):
        halo = h_ref[...].astype(F32)
        if halo_is_up:
            halo = jnp.where(si > 0, halo, 0.0)
        full = jnp.concatenate([halo, u_ref[...].astype(F32)], axis=0)
        c = cb_ref[...] + full * cw_ref[2:3, :]
        c = c + pltpu.roll(full, 1, 0) * cw_ref[1:2, :]
        c = c + pltpu.roll(full, 2, 0) * cw_ref[0:1, :]
        return c[HALO_ROWS:]

    gate = conv(ug_ref, hg_ref, cwg_ref, cbg_ref)
    val = conv(uv_ref, hv_ref, cwv_ref, cbv_ref)
    gelu = 0.5 * gate * (1.0 + jnp.tanh(math.sqrt(2.0 / math.pi) * (gate + 0.044715 * (gate * gate * gate))))
    o_ref[...] = (gelu * val).astype(o_ref.dtype)


def _conv_act(up, halo, conv_w, conv_b, b, s, *, halo_is_up):
    f = up.shape[1] // 2
    ts = min(512, s)
    ns = s // ts
    tf = _pick(f, (256, 128))
    nf = f // tf
    hb = ts // HALO_ROWS

    def blk(off):
        return pl.BlockSpec((ts, tf), lambda bi, si, fi: (bi * ns + si, off + fi))

    def halo_blk(off):
        if halo_is_up:
            return pl.BlockSpec((HALO_ROWS, tf),
                                lambda bi, si, fi: (jnp.maximum((bi * ns + si) * hb - 1, 0), off + fi))
        return pl.BlockSpec((None, HALO_ROWS, tf), lambda bi, si, fi: (bi, 0, off + fi))

    def vec(rows, off):
        return pl.BlockSpec((rows, tf), lambda bi, si, fi: (0, off + fi))

    cw = jnp.pad(conv_w, ((0, 8 - CONV_W), (0, 0)))
    return pl.pallas_call(
        functools.partial(_conv_act_kernel, halo_is_up=halo_is_up),
        out_shape=jax.ShapeDtypeStruct((b * s, f), BF16),
        grid=(b, ns, nf),
        in_specs=[blk(0), blk(nf), halo_blk(0), halo_blk(nf), vec(8, 0), vec(8, nf), vec(1, 0), vec(1, nf)],
        out_specs=pl.BlockSpec((ts, tf), lambda bi, si, fi: (bi * ns + si, fi)),
        compiler_params=_params("arbitrary", "arbitrary", "arbitrary"),
        name="conv_act",
    )(up, up, halo, halo, cw, cw, conv_b.reshape(1, 2 * f), conv_b.reshape(1, 2 * f))


def _rope_tables(pos):
    half = HEAD_DIM // 2
    inv = ROPE_THETA ** (-jnp.arange(half, dtype=F32) / half)
    ang = pos.astype(F32)[:, None] * inv[None, :]
    cos, sin = jnp.cos(ang), jnp.sin(ang)
    return jnp.concatenate([cos, cos], axis=1), jnp.concatenate([-sin, sin], axis=1)


def _state_to_tiles(wkv):
    b, h, n, _ = wkv.shape
    return wkv.reshape(b, h // 2, 2, n, n).transpose(0, 1, 3, 2, 4).reshape(b, h // 2 * n, 2 * n)


def _tiles_to_state(st, h):
    b = st.shape[0]
    n = RW_HEAD
    return st.reshape(b, h // 2, n, 2, n).transpose(0, 1, 3, 2, 4).reshape(b, h, n, n)


def _attn_layer(x, b, s, g, w_qkv, w_o, sinks, pos, cache=None):
    d = x.shape[1]
    kv_dim = d // N_GROUPS
    cos, sin = _rope_tables(pos)
    cos, sin = jnp.tile(cos, (b, 1)), jnp.tile(sin, (b, 1))
    qkv = _mm(x, w_qkv, BF16, g=g, rope=(cos, sin, d + kv_dim))
    if cache is None:
        o = _attn_prompt(qkv, sinks, b, s, d)
    else:
        o = _attn_sample(qkv, cache[0].reshape(b, -1, kv_dim), cache[1].reshape(b, -1, kv_dim), sinks, b, s, d)
    k3 = qkv[:, d:d + kv_dim].reshape(b, s, kv_dim // HEAD_DIM, HEAD_DIM).astype(F32)
    v3 = qkv[:, d + kv_dim:].reshape(b, s, kv_dim // HEAD_DIM, HEAD_DIM).astype(F32)
    return _mm(o, w_o, BF16), k3, v3


def _rwkv_layer(x, b, s, g, shift, wkv, p):
    d = x.shape[1]
    heads = d // RW_HEAD
    if shift is None:
        mixed = _rw_mix(x, x, g, p["mix"], b, s, halo_is_x=True)
        s0 = jnp.zeros((b, heads // 2 * RW_HEAD, 2 * RW_HEAD), F32)
    else:
        halo = jnp.pad(shift.reshape(b, 1, d), ((0, 0), (7, 0), (0, 0)))
        mixed = _rw_mix(x, halo, g, p["mix"], b, s, halo_is_x=False)
        s0 = _state_to_tiles(wkv.astype(F32))
    xr, xw, xk, xv, xa, xg, tail = mixed
    r = _mm(xr, p["w_r"], BF16)
    k = _mm(xk, p["w_k"], BF16)
    v = _mm(xv, p["w_v"], BF16)
    lw = _mm(_mm(xw, p["w1"], BF16, act="tanh"), p["w2"], F32)
    la = _mm(_mm(xa, p["a1"], BF16), p["a2"], F32)
    gate = _mm(_mm(xg, p["g1"], BF16, act="sigmoid"), p["g2"], BF16)
    kk, k2, dec, kka = _rw_prep(k, lw, la, p["w0"], p["a0"], p["k_k"], p["k_a"])
    y, s_t = _rw_scan(r, dec, k2, v, kk, kka, s0, b, s)
    o = _rw_post(y, r, k2, v, gate, p["lnx"], p["r_k"].reshape(-1))
    return _mm(o, p["w_o"], BF16), tail[:, 7:8, :], _tiles_to_state(s_t, heads)


def _ffn_layer(x, b, s, g, hist, w_up, conv_w, conv_b, w_down):
    f2 = w_up.shape[1]
    up = _mm(x, w_up, BF16, g=g)
    if hist is None:
        h = _conv_act(up, up, conv_w, conv_b, b, s, halo_is_up=True)
    else:
        halo = jnp.pad(hist, ((0, 0), (HALO_ROWS - (CONV_W - 1), 0), (0, 0)))
        h = _conv_act(up, halo, conv_w, conv_b, b, s, halo_is_up=False)
    tail = up.reshape(b, s, f2)[:, s - (CONV_W - 1):, :].astype(F32)
    return _mm(h, w_down, BF16), tail


def kernel(x_prompt, x_sample, cache_k, cache_v, state_pool, state_shift, state_wkv, state_conv, norm_g, attn_w_qkv, attn_w_o, attn_sinks, pool_w, pool_scale, rw_mix, rw_w_r, rw_w_k, rw_w_v, rw_w_o, rw_w0, rw_w1, rw_w2, rw_a0, rw_a1, rw_a2, rw_g1, rw_g2, rw_k_k, rw_k_a, rw_r_k, rw_lnx, ffn_w_up, ffn_conv_w, ffn_conv_b, ffn_w_down):
    bp, sp, d = x_prompt.shape
    bs, ss, _ = x_sample.shape
    depth = norm_g.shape[0]
    window = cache_k.shape[2]
    xp = x_prompt.reshape(bp * sp, d)
    xs = x_sample.reshape(bs * ss, d)
    outs = {n: [] for n in ("kp", "vp", "kn", "vn", "poolp", "pools", "shp", "shs", "wkvp", "wkvs", "convp", "convs")}
    for i in range(depth):
        kind, j = i % 3, i // 3
        g = norm_g[i]
        if kind == 0:
            wq, wo = attn_w_qkv[j].astype(BF16), attn_w_o[j].astype(BF16)
            mp, kp, vp = _attn_layer(xp, bp, sp, g[0], wq, wo, attn_sinks[j], jnp.arange(sp))
            ms, kn, vn = _attn_layer(xs, bs, ss, g[0], wq, wo, attn_sinks[j], PAST_LEN + jnp.arange(ss),
                                     cache=(cache_k[j], cache_v[j]))
            outs["kp"].append(kp[:, sp - window:])
            outs["vp"].append(vp[:, sp - window:])
            outs["kn"].append(kn)
            outs["vn"].append(vn)
        elif kind == 1:
            wp = pool_w[j].astype(BF16)
            mp, hp = _pool(xp, xp, g[0], wp, pool_scale[j], bp, sp, halo_is_x=True, pos0=0)
            halo = jnp.pad(state_pool[j], ((0, 0), (HALO_ROWS - POOL_HIST, 0), (0, 0)))
            ms, hs = _pool(xs, halo, g[0], wp, pool_scale[j], bs, ss, halo_is_x=False, pos0=PAST_LEN)
            outs["poolp"].append(hp[:, HALO_ROWS - POOL_HIST:])
            outs["pools"].append(hs[:, HALO_ROWS - POOL_HIST:])
        else:
            p = dict(mix=rw_mix[j], w_r=rw_w_r[j].astype(BF16), w_k=rw_w_k[j].astype(BF16),
                     w_v=rw_w_v[j].astype(BF16), w_o=rw_w_o[j].astype(BF16), w0=rw_w0[j],
                     w1=rw_w1[j].astype(BF16), w2=rw_w2[j].astype(BF16), a0=rw_a0[j],
                     a1=rw_a1[j].astype(BF16), a2=rw_a2[j].astype(BF16), g1=rw_g1[j].astype(BF16),
                     g2=rw_g2[j].astype(BF16), k_k=rw_k_k[j], k_a=rw_k_a[j], r_k=rw_r_k[j], lnx=rw_lnx[j])
            mp, shp, wkvp = _rwkv_layer(xp, bp, sp, g[0], None, None, p)
            ms, shs, wkvs = _rwkv_layer(xs, bs, ss, g[0], state_shift[j], state_wkv[j], p)
            outs["shp"].append(shp)
            outs["shs"].append(shs)
            outs["wkvp"].append(wkvp)
            outs["wkvs"].append(wkvs)
        xp = _add_norm(xp, mp, g[1])
        xs = _add_norm(xs, ms, g[1])
        w_up, w_down = ffn_w_up[i].astype(BF16), ffn_w_down[i].astype(BF16)
        fp, cp = _ffn_layer(xp, bp, sp, g[2], None, w_up, ffn_conv_w[i], ffn_conv_b[i], w_down)
        fs, cs = _ffn_layer(xs, bs, ss, g[2], state_conv[i], w_up, ffn_conv_w[i], ffn_conv_b[i], w_down)
        outs["convp"].append(cp)
        outs["convs"].append(cs)
        xp = _add_norm(xp, fp, g[3])
        xs = _add_norm(xs, fs, g[3])
    st = lambda n: jnp.stack(outs[n])
    return (xp.reshape(bp, sp, d), xs.reshape(bs, ss, d),
            st("kp"), st("vp"), st("poolp"), st("shp"), st("wkvp"), st("convp"),
            st("kn"), st("vn"), st("pools"), st("shs"), st("wkvs"), st("convs"))
```

```python
import functools
import math

import jax
import jax.numpy as jnp
from jax import lax
from jax.experimental import pallas as pl
from jax.experimental.pallas import tpu as pltpu

BF16 = jnp.bfloat16
F32 = jnp.float32

CHUNK = 64
HEAD_DIM = 128
N_GROUPS = 4
ROPE_THETA = 10000.0
POOL_WINDOWS = (2, 4, 8, 16)
POOL_HIST = 15
RW_HEAD = 64
LNX_EPS = 64e-5
NORM_EPS = 1e-6
PAST_LEN = 2048
CONV_W = 3
LOG_DECAY_MIN = -math.exp(-0.5)
RW_CHUNK = 64
RW_TILES_PER_STEP = 4
SOLVE_BLOCK = 16

LANES = 128
VMEM_LIMIT_BYTES = 56 * 1024 * 1024
HALO_ROWS = 16


def _params(*sem):
    return pltpu.CompilerParams(dimension_semantics=sem, vmem_limit_bytes=VMEM_LIMIT_BYTES)


def _pick(n, prefs):
    for p in prefs:
        if n % p == 0:
            return p
    return n


def _rms(xf, g):
    return xf * lax.rsqrt(jnp.mean(xf * xf, axis=-1, keepdims=True) + NORM_EPS) * g


def _seg_ones(rows):
    r = lax.broadcasted_iota(jnp.int32, (rows, LANES), 0)
    c = lax.broadcasted_iota(jnp.int32, (rows, LANES), 1)
    return ((r % LANES) // RW_HEAD == c // RW_HEAD).astype(BF16)


def _seg_sum(p, jj):
    hi = p.astype(BF16)
    lo = (p - hi.astype(F32)).astype(BF16)
    return jnp.dot(jnp.concatenate([hi, lo], axis=1), jj, preferred_element_type=F32)


def _mm_kernel(*refs, norm, rope_blocks, act):
    it = iter(refs)
    x_ref = next(it)
    g_ref = next(it) if norm else None
    w_ref = next(it)
    cos_ref = next(it) if rope_blocks else None
    sin_ref = next(it) if rope_blocks else None
    o_ref = next(it)
    xn_ref = next(it) if norm else None
    j = pl.program_id(1)

    if norm:
        @pl.when(j == 0)
        def _():
            xn_ref[...] = _rms(x_ref[...].astype(F32), g_ref[...]).astype(BF16)
        lhs = xn_ref[...]
    else:
        lhs = x_ref[...]
    acc = jnp.dot(lhs, w_ref[...], preferred_element_type=F32)
    if act == "tanh":
        acc = jnp.tanh(acc)
    elif act == "sigmoid":
        acc = jax.nn.sigmoid(acc)

    if rope_blocks:
        @pl.when(j < rope_blocks)
        def _():
            cos = cos_ref[...]
            sin = sin_ref[...]
            for h in range(acc.shape[1] // HEAD_DIM):
                xh = acc[:, h * HEAD_DIM:(h + 1) * HEAD_DIM]
                o_ref[:, h * HEAD_DIM:(h + 1) * HEAD_DIM] = (
                    xh * cos + pltpu.roll(xh, HEAD_DIM // 2, 1) * sin).astype(o_ref.dtype)

        @pl.when(j >= rope_blocks)
        def _():
            o_ref[...] = acc.astype(o_ref.dtype)
    else:
        o_ref[...] = acc.astype(o_ref.dtype)


def _mm(x, w, out_dtype, *, g=None, rope=None, act=None):
    m, k = x.shape
    n = w.shape[1]
    norm = g is not None
    if norm or k > 8192:
        tm = min(512, m)
    else:
        tm = min(1024, m)
    tn = _pick(math.gcd(n, rope[2]) if rope else n, (256,) if k > 8192 else (512, 256, 128))
    rope_blocks = 0
    in_specs = [pl.BlockSpec((tm, k), lambda i, j: (i, 0))]
    args = [x]
    if norm:
        in_specs.append(pl.BlockSpec((1, k), lambda i, j: (0, 0)))
        args.append(g.reshape(1, k).astype(F32))
    in_specs.append(pl.BlockSpec((k, tn), lambda i, j: (0, j)))
    args.append(w)
    if rope is not None:
        cos, sin, n_cols = rope
        assert n_cols % tn == 0 and tn % HEAD_DIM == 0
        rope_blocks = n_cols // tn
        in_specs += [pl.BlockSpec((tm, HEAD_DIM), lambda i, j: (i, 0))] * 2
        args += [cos, sin]
    scratch = [pltpu.VMEM((tm, k), BF16)] if norm else []
    return pl.pallas_call(
        functools.partial(_mm_kernel, norm=norm, rope_blocks=rope_blocks, act=act),
        out_shape=jax.ShapeDtypeStruct((m, n), out_dtype),
        grid=(m // tm, n // tn),
        in_specs=in_specs,
        out_specs=pl.BlockSpec((tm, tn), lambda i, j: (i, j)),
        scratch_shapes=scratch,
        compiler_params=_params("arbitrary", "arbitrary"),
        name="mm",
    )(*args)


def _add_norm_kernel(x_ref, m_ref, g_ref, *refs):
    x = x_ref[...] + _rms(m_ref[...].astype(F32), g_ref[...])
    if len(refs) == 1:
        refs[0][...] = x
    else:
        gn_ref, o_ref, on_ref = refs
        o_ref[...] = x
        on_ref[...] = _rms(x, gn_ref[...]).astype(on_ref.dtype)


def _add_norm(x, mix, g, g_next=None):
    m, d = x.shape
    tr = min(256, m)
    row = pl.BlockSpec((tr, d), lambda i: (i, 0))
    vec = pl.BlockSpec((1, d), lambda i: (0, 0))
    with_next = g_next is not None
    return pl.pallas_call(
        _add_norm_kernel,
        out_shape=((jax.ShapeDtypeStruct((m, d), F32), jax.ShapeDtypeStruct((m, d), BF16)) if with_next
                   else jax.ShapeDtypeStruct((m, d), F32)),
        grid=(m // tr,),
        in_specs=[row, row, vec] + ([vec] if with_next else []),
        out_specs=(row, row) if with_next else row,
        compiler_params=_params("arbitrary"),
        name="add_norm",
    )(x, mix, g.reshape(1, d), *([g_next.reshape(1, d)] if with_next else []))


def _attn_kernel(sink_ref, q_ref, *refs, n_pieces, n_kv, band_mask):
    k_refs = refs[:n_pieces]
    v_refs = refs[n_pieces:2 * n_pieces]
    o_ref = refs[2 * n_pieces]
    tq = q_ref.shape[0]
    c = pl.program_id(1)
    scale = HEAD_DIM ** -0.5
    for kv in range(n_kv):
        lo, hi = kv * HEAD_DIM, (kv + 1) * HEAD_DIM
        q4 = jnp.concatenate(
            [q_ref[:, (kv * N_GROUPS + r) * HEAD_DIM:(kv * N_GROUPS + r + 1) * HEAD_DIM] for r in range(N_GROUPS)],
            axis=0)
        kb = jnp.concatenate([kr[:, lo:hi].astype(BF16) for kr in k_refs], axis=0)
        vb = jnp.concatenate([vr[:, lo:hi].astype(BF16) for vr in v_refs], axis=0)
        s = lax.dot_general(q4, kb, (((1,), (1,)), ((), ())), preferred_element_type=F32) * scale
        if band_mask:
            col = lax.broadcasted_iota(jnp.int32, s.shape, 1)
            s = jnp.where(col >= CHUNK * (n_pieces - 1 - c), s, -1e30)
        sink = jnp.concatenate(
            [jnp.full((tq, 1), sink_ref[kv * N_GROUPS + r], F32) for r in range(N_GROUPS)], axis=0)
        mx = jnp.maximum(jnp.max(s, axis=-1, keepdims=True), sink)
        p = jnp.exp(s - mx)
        denom = jnp.sum(p, axis=-1, keepdims=True) + jnp.exp(sink - mx)
        o = jnp.dot((p / denom).astype(BF16), vb, preferred_element_type=F32)
        for r in range(N_GROUPS):
            h = kv * N_GROUPS + r
            o_ref[:, h * HEAD_DIM:(h + 1) * HEAD_DIM] = o[r * tq:(r + 1) * tq].astype(o_ref.dtype)


def _attn_prompt(qkv, sinks, b, s, d):
    kv_dim = d // N_GROUPS
    n_kv = kv_dim // HEAD_DIM
    nc = s // CHUNK
    n_band = 3
    kcol, vcol = d // kv_dim, d // kv_dim + 1

    def piece(jj, col):
        return pl.BlockSpec((CHUNK, kv_dim),
                            lambda bi, c, sk: (bi * nc + jnp.maximum(c - (n_band - 1) + jj, 0), col))

    in_specs = [pl.BlockSpec((CHUNK, d), lambda bi, c, sk: (bi * nc + c, 0))]
    in_specs += [piece(jj, kcol) for jj in range(n_band)] + [piece(jj, vcol) for jj in range(n_band)]
    return pl.pallas_call(
        functools.partial(_attn_kernel, n_pieces=n_band, n_kv=n_kv, band_mask=True),
        out_shape=jax.ShapeDtypeStruct((b * s, d), BF16),
        grid_spec=pltpu.PrefetchScalarGridSpec(
            num_scalar_prefetch=1, grid=(b, nc), in_specs=in_specs,
            out_specs=pl.BlockSpec((CHUNK, d), lambda bi, c, sk: (bi * nc + c, 0))),
        compiler_params=_params("arbitrary", "arbitrary"),
        name="attn_prompt",
    )(sinks, *([qkv] * (1 + 2 * n_band)))


def _attn_sample(qkv, cache_k, cache_v, sinks, b, t, d):
    kv_dim = d // N_GROUPS
    n_kv = kv_dim // HEAD_DIM
    win = cache_k.shape[1]
    kcol, vcol = d // kv_dim, d // kv_dim + 1
    in_specs = [
        pl.BlockSpec((t, d), lambda bi, c, sk: (bi, 0)),
        pl.BlockSpec((None, win, kv_dim), lambda bi, c, sk: (bi, 0, 0)),
        pl.BlockSpec((t, kv_dim), lambda bi, c, sk: (bi, kcol)),
        pl.BlockSpec((None, win, kv_dim), lambda bi, c, sk: (bi, 0, 0)),
        pl.BlockSpec((t, kv_dim), lambda bi, c, sk: (bi, vcol)),
    ]
    return pl.pallas_call(
        functools.partial(_attn_kernel, n_pieces=2, n_kv=n_kv, band_mask=False),
        out_shape=jax.ShapeDtypeStruct((b * t, d), BF16),
        grid_spec=pltpu.PrefetchScalarGridSpec(
            num_scalar_prefetch=1, grid=(b, 1), in_specs=in_specs,
            out_specs=pl.BlockSpec((t, d), lambda bi, c, sk: (bi, 0))),
        compiler_params=_params("arbitrary", "arbitrary"),
        name="attn_sample",
    )(sinks, qkv, cache_k, qkv, cache_v, qkv)


def _pool_kernel(x_ref, halo_ref, g_ref, w_ref, sc_ref, o_ref, st_ref, *, halo_is_x, pos0):
    si = pl.program_id(1)
    ts, d = x_ref.shape
    gw = d // len(POOL_WINDOWS)
    g = g_ref[...]
    un = _rms(x_ref[...], g)
    if halo_is_x:
        halo = jnp.where(si > 0, _rms(halo_ref[...], g), 0.0)
    else:
        halo = halo_ref[...]
    full = jnp.concatenate([halo, un], axis=0)
    pos = pos0 + si * ts + lax.broadcasted_iota(jnp.int32, (ts, 1), 0)
    for gi, win in enumerate(POOL_WINDOWS):
        lo, hi = gi * gw, (gi + 1) * gw
        acc = full[:, lo:hi]
        span = 1
        while span < win:
            acc = acc + pltpu.roll(acc, span, 0)
            span *= 2
        cnt = jnp.minimum(win, pos + 1).astype(F32)
        dlt = (acc[HALO_ROWS:] / cnt - un[:, lo:hi]).astype(BF16)
        y = jnp.dot(dlt, w_ref[gi], preferred_element_type=F32)
        o_ref[:, lo:hi] = (y * sc_ref[:, lo:hi]).astype(o_ref.dtype)
    st_ref[...] = full[ts:]


def _pool(x, halo, g, w, scale, b, s, *, halo_is_x, pos0):
    d = x.shape[1]
    ts = min(256, s)
    ns = s // ts
    hb = ts // HALO_ROWS
    if halo_is_x:
        halo_spec = pl.BlockSpec((HALO_ROWS, d), lambda bi, si: (jnp.maximum((bi * ns + si) * hb - 1, 0), 0))
    else:
        halo_spec = pl.BlockSpec((None, HALO_ROWS, d), lambda bi, si: (bi, 0, 0))
    ng, gw = w.shape[0], w.shape[1]
    return pl.pallas_call(
        functools.partial(_pool_kernel, halo_is_x=halo_is_x, pos0=pos0),
        out_shape=(jax.ShapeDtypeStruct((b * s, d), BF16), jax.ShapeDtypeStruct((b, HALO_ROWS, d), F32)),
        grid=(b, ns),
        in_specs=[pl.BlockSpec((ts, d), lambda bi, si: (bi * ns + si, 0)),
                  halo_spec,
                  pl.BlockSpec((1, d), lambda bi, si: (0, 0)),
                  pl.BlockSpec((ng, gw, gw), lambda bi, si: (0, 0, 0)),
                  pl.BlockSpec((1, d), lambda bi, si: (0, 0))],
        out_specs=(pl.BlockSpec((ts, d), lambda bi, si: (bi * ns + si, 0)),
                   pl.BlockSpec((None, HALO_ROWS, d), lambda bi, si: (bi, 0, 0))),
        compiler_params=_params("arbitrary", "arbitrary"),
        name="pool",
    )(x, halo, g.reshape(1, d), w, scale.reshape(1, d))


def _rw_mix_kernel(x_ref, halo_ref, g_ref, mix_ref, *o_refs, halo_is_x):
    si = pl.program_id(1)
    g = g_ref[...]
    un = _rms(x_ref[...], g)
    n_h = halo_ref.shape[0]
    if halo_is_x:
        prev_row = jnp.where(si > 0, _rms(halo_ref[n_h - 1:n_h, :], g), 0.0)
    else:
        prev_row = halo_ref[n_h - 1:n_h, :]
    row = lax.broadcasted_iota(jnp.int32, (un.shape[0], 1), 0)
    prev = jnp.where(row == 0, prev_row, pltpu.roll(un, 1, 0))
    xx = prev - un
    for jm in range(6):
        o_refs[jm][...] = (un + xx * mix_ref[jm:jm + 1, :]).astype(BF16)
    st_ref = o_refs[6]
    st_ref[...] = un[un.shape[0] - st_ref.shape[0]:]


def _rw_mix(x, halo, g, mix, b, s, *, halo_is_x):
    d = x.shape[1]
    ts = min(256, s)
    ns = s // ts
    hr = 8
    if halo_is_x:
        halo_spec = pl.BlockSpec((hr, d), lambda bi, si: (jnp.maximum((bi * ns + si) * (ts // hr) - 1, 0), 0))
    else:
        halo_spec = pl.BlockSpec((None, hr, d), lambda bi, si: (bi, 0, 0))
    row_spec = pl.BlockSpec((ts, d), lambda bi, si: (bi * ns + si, 0))
    return pl.pallas_call(
        functools.partial(_rw_mix_kernel, halo_is_x=halo_is_x),
        out_shape=tuple([jax.ShapeDtypeStruct((b * s, d), BF16)] * 6 + [jax.ShapeDtypeStruct((b, hr, d), F32)]),
        grid=(b, ns),
        in_specs=[row_spec, halo_spec,
                  pl.BlockSpec((1, d), lambda bi, si: (0, 0)),
                  pl.BlockSpec((8, d), lambda bi, si: (0, 0))],
        out_specs=tuple([row_spec] * 6 + [pl.BlockSpec((None, hr, d), lambda bi, si: (bi, 0, 0))]),
        compiler_params=_params("arbitrary", "arbitrary"),
        name="rw_mix",
    )(x, halo, g.reshape(1, d), jnp.pad(mix, ((0, 2), (0, 0))))


def _rw_prep_kernel(k_ref, lw_ref, la_ref, w0_ref, a0_ref, kk_ref, ka_ref,
                    kk_o, k2_o, ld_o, kka_o):
    k = k_ref[...].astype(F32)
    a = jax.nn.sigmoid(a0_ref[...] + la_ref[...].astype(F32))
    ld_o[...] = LOG_DECAY_MIN * jax.nn.sigmoid(w0_ref[...] + lw_ref[...].astype(F32))
    k2_o[...] = k * (1.0 + (a - 1.0) * ka_ref[...])
    kkr = k * kk_ref[...]
    sq = kkr * kkr
    jj = _seg_ones(2 * LANES)
    for c in range(k.shape[1] // LANES):
        lo, hi = c * LANES, (c + 1) * LANES
        ss = _seg_sum(sq[:, lo:hi], jj)
        kkn = kkr[:, lo:hi] / jnp.maximum(jnp.sqrt(ss), 1e-12)
        kk_o[:, lo:hi] = kkn
        kka_o[:, lo:hi] = kkn * a[:, lo:hi]


def _rw_prep(k, lw, la, w0, a0, k_k, k_a):
    m, d = k.shape
    tr = min(128, m)
    row = pl.BlockSpec((tr, d), lambda i: (i, 0))
    vec = pl.BlockSpec((1, d), lambda i: (0, 0))
    return pl.pallas_call(
        _rw_prep_kernel,
        out_shape=tuple([jax.ShapeDtypeStruct((m, d), F32)] * 4),
        grid=(m // tr,),
        in_specs=[row, row, row, vec, vec, vec, vec],
        out_specs=tuple([row] * 4),
        compiler_params=_params("arbitrary"),
        name="rw_prep",
    )(k, lw, la, w0.reshape(1, d), a0.reshape(1, d), k_k.reshape(1, d), k_a.reshape(1, d))


def _dot_nt(a, b):
    return lax.dot_general(a, b, (((1,), (1,)), ((), ())), preferred_element_type=F32)


def _rw_chunk(r, ld, k, v, kk, kka, zs):
    c, width = r.shape
    n = RW_HEAD
    tiles = [slice(g * LANES, (g + 1) * LANES) for g in range(width // LANES)]
    lanes = lambda parts: jnp.concatenate(parts, axis=1) if len(parts) > 1 else parts[0]
    row = lax.broadcasted_iota(jnp.int32, (c, 2 * c), 0)
    col = lax.broadcasted_iota(jnp.int32, (c, 2 * c), 1)
    s_idx = col % c
    first = col < c
    strict = s_idx < row
    incl = s_idx <= row
    head0 = lax.broadcasted_iota(jnp.int32, (c, width), 1) % LANES < n

    tri = (lax.broadcasted_iota(jnp.int32, (c, 3 * c), 1) % c <= lax.broadcasted_iota(jnp.int32, (c, 3 * c), 0))
    hi = ld.astype(BF16)
    mid = (ld - hi.astype(F32)).astype(BF16)
    lo = (ld - hi.astype(F32) - mid.astype(F32)).astype(BF16)
    cum = jnp.dot(tri.astype(BF16), jnp.concatenate([hi, mid, lo], axis=0), preferred_element_type=F32)

    e_cum = jnp.exp(cum)
    e_inv = jnp.exp(-cum)
    at = -kk * jnp.exp(cum - ld)
    rt = r * e_cum
    bt = kka * e_inv
    kt = k * e_inv
    w_last = e_cum[c - 1:c, :]

    zero = jnp.zeros_like(at)
    a0r0 = jnp.concatenate([jnp.where(head0, at, zero), jnp.where(head0, rt, zero)], axis=0).astype(BF16)
    a1 = jnp.where(head0, zero, at).astype(BF16)
    r1 = jnp.where(head0, zero, rt).astype(BF16)
    ar = jnp.concatenate([at, rt], axis=0).astype(BF16)
    bk = jnp.concatenate([bt, kt], axis=0).astype(BF16)
    kb = jnp.concatenate([kt, bt], axis=0).astype(BF16)
    vb = v.astype(BF16)
    vv = jnp.concatenate([vb, vb], axis=0)

    l2, g0r, g1r, mv, p = [], [], [], [], []
    for t, z in zip(tiles, zs):
        g0 = _dot_nt(a0r0[:, t], bk[:, t])
        g1a = _dot_nt(a1[:, t], kb[:, t])
        g1r.append(_dot_nt(r1[:, t], bk[:, t]))
        g0a = g0[:c]
        g0r.append(g0[c:])
        l2.append(jnp.where(strict, jnp.where(first, g0a, g1a), 0.0))
        mak = jnp.concatenate([jnp.where(strict & ~first, g0a, 0.0), jnp.where(strict & first, g1a, 0.0)], axis=0)
        mv.append(jnp.dot(mak.astype(BF16), vv[:, t], preferred_element_type=F32))
        p.append(jnp.dot(ar[:, t], z.astype(BF16), preferred_element_type=F32))
    mv, p = lanes(mv), lanes(p)
    u0 = p[:c] + jnp.where(head0, mv[:c], mv[c:])

    sb = SOLVE_BLOCK
    nb = c // sb
    spread = (lax.broadcasted_iota(jnp.int32, (2 * c, LANES), 0) // c
              == lax.broadcasted_iota(jnp.int32, (2 * c, LANES), 1) // n).astype(BF16)
    lane_b = lax.broadcasted_iota(jnp.int32, (sb, 2 * c), 1) % c
    first_b = lax.broadcasted_iota(jnp.int32, (sb, 2 * c), 1) < c
    head0_b = lax.broadcasted_iota(jnp.int32, (sb, width), 1) % LANES < n
    lcol = lanes([
        jnp.dot(jnp.concatenate([jnp.where(lane_b == blk * sb + s, l2g[blk * sb:(blk + 1) * sb], 0.0).astype(BF16)
                                 for blk in range(nb) for s in range(sb - 1)], axis=0),
                spread, preferred_element_type=F32) for l2g in l2])
    solved = []
    for blk in range(nb):
        ub = u0[blk * sb:(blk + 1) * sb]
        if blk > 0:
            done = jnp.concatenate(solved + [jnp.zeros((c - blk * sb, width), F32)], axis=0).astype(BF16)
            done = jnp.concatenate([done, done], axis=0)
            off = []
            for t, l2g in zip(tiles, l2):
                band = l2g[blk * sb:(blk + 1) * sb]
                lhs = jnp.concatenate([jnp.where(first_b, band, 0.0), jnp.where(first_b, 0.0, band)], axis=0)
                off.append(jnp.dot(lhs.astype(BF16), done[:, t], preferred_element_type=F32))
            off = lanes(off)
            ub = ub + jnp.where(head0_b, off[:sb], off[sb:])
        for s in range(sb - 1):
            i = blk * (sb - 1) + s
            ub = ub + lcol[i * sb:(i + 1) * sb] * ub[s:s + 1, :]
        solved.append(ub)
    u = jnp.concatenate(solved, axis=0)

    uv = jnp.concatenate([u, v], axis=0).astype(BF16)
    bkw = jnp.concatenate([bt * w_last, kt * w_last], axis=0)
    same_head = (lax.broadcasted_iota(jnp.int32, (LANES, LANES), 0) // n
                 == lax.broadcasted_iota(jnp.int32, (LANES, LANES), 1) // n)
    ny, z_new = [], []
    for t, z, g0rg, g1rg in zip(tiles, zs, g0r, g1r):
        nmat = jnp.concatenate([jnp.where(incl, g0rg, 0.0), jnp.where(incl, g1rg, 0.0)], axis=0).astype(BF16)
        ny.append(jnp.dot(nmat, uv[:, t], preferred_element_type=F32))
        upd_z = jnp.dot(bkw[:, t].T.astype(BF16), uv[:, t], preferred_element_type=F32)
        w_col = jnp.broadcast_to(w_last[:, t], (LANES, LANES)).T
        z_new.append(z * w_col + jnp.where(same_head, upd_z, 0.0))
    ny = lanes(ny)
    return p[c:] + jnp.where(head0, ny[:c], ny[c:]), z_new


def _rw_scan_kernel(r_ref, ld_ref, k_ref, v_ref, kk_ref, kka_ref, s0_ref, y_ref, sT_ref, z_ref):
    @pl.when(pl.program_id(2) == 0)
    def _():
        z_ref[...] = s0_ref[...]

    tg = z_ref.shape[0]
    y, zs = _rw_chunk(*[ref[...].astype(F32) for ref in (r_ref, ld_ref, k_ref, v_ref, kk_ref, kka_ref)],
                      [z_ref[g] for g in range(tg)])
    y_ref[...] = y
    for g in range(tg):
        z_ref[g] = zs[g]
    sT_ref[...] = z_ref[...]


def _rw_scan(r, ld, k, v, kk, kka, s0, b, s):
    d = r.shape[1]
    ntile = d // LANES
    tg = min(RW_TILES_PER_STEP, ntile)
    tc = min(RW_CHUNK, s)
    nch = s // tc
    seq = pl.BlockSpec((tc, tg * LANES), lambda bi, gi, ci: (bi * nch + ci, gi))
    st = pl.BlockSpec((None, tg, LANES, LANES), lambda bi, gi, ci: (bi, gi, 0, 0))
    return pl.pallas_call(
        _rw_scan_kernel,
        out_shape=(jax.ShapeDtypeStruct((b * s, d), F32), jax.ShapeDtypeStruct(s0.shape, F32)),
        grid=(b, ntile // tg, nch),
        in_specs=[seq] * 6 + [st],
        out_specs=(seq, st),
        scratch_shapes=[pltpu.VMEM((tg, LANES, LANES), F32)],
        compiler_params=_params("arbitrary", "arbitrary", "arbitrary"),
        name="rw_scan",
    )(r, ld, k, v, kk, kka, s0)


def _rw_post_kernel(y_ref, r_ref, k_ref, v_ref, g_ref, ln_ref, rk_ref, o_ref):
    jj = _seg_ones(2 * LANES)
    for c in range(y_ref.shape[1] // LANES):
        sl = slice(c * LANES, (c + 1) * LANES)
        y = y_ref[:, sl]
        mu = _seg_sum(y, jj) * (1.0 / RW_HEAD)
        yc = y - mu
        var = _seg_sum(yc * yc, jj) * (1.0 / RW_HEAD)
        o = yc * lax.rsqrt(var + LNX_EPS) * ln_ref[0:1, sl] + ln_ref[1:2, sl]
        rk = r_ref[:, sl].astype(F32) * k_ref[:, sl] * rk_ref[:, sl]
        bonus = _seg_sum(rk, jj) * v_ref[:, sl].astype(F32)
        o_ref[:, sl] = ((o + bonus) * g_ref[:, sl].astype(F32)).astype(o_ref.dtype)


def _rw_post(y, r, k2, v, gate, lnx, r_k):
    m, d = y.shape
    tr = min(256, m)
    row = pl.BlockSpec((tr, d), lambda i: (i, 0))
    return pl.pallas_call(
        _rw_post_kernel,
        out_shape=jax.ShapeDtypeStruct((m, d), BF16),
        grid=(m // tr,),
        in_specs=[row] * 5 + [pl.BlockSpec((8, d), lambda i: (0, 0)), pl.BlockSpec((1, d), lambda i: (0, 0))],
        out_specs=row,
        compiler_params=_params("arbitrary"),
        name="rw_post",
    )(y, r, k2, v, gate, jnp.pad(lnx, ((0, 6), (0, 0))), r_k.reshape(1, d))


def _conv_act_kernel(ug_ref, uv_ref, hg_ref, hv_ref, cwg_ref, cwv_ref, cbg_ref, cbv_ref, o_ref, *, halo_is_up):
    si = pl.program_id(1)

    def conv(u_ref, h_ref, cw_ref, cb_ref):
        halo = h_ref[...].astype(F32)
        if halo_is_up:
            halo = jnp.where(si > 0, halo, 0.0)
        full = jnp.concatenate([halo, u_ref[...].astype(F32)], axis=0)
        c = cb_ref[...] + full * cw_ref[2:3, :]
        c = c + pltpu.roll(full, 1, 0) * cw_ref[1:2, :]
        c = c + pltpu.roll(full, 2, 0) * cw_ref[0:1, :]
        return c[HALO_ROWS:]

    gate = conv(ug_ref, hg_ref, cwg_ref, cbg_ref)
    val = conv(uv_ref, hv_ref, cwv_ref, cbv_ref)
    gelu = 0.5 * gate * (1.0 + jnp.tanh(math.sqrt(2.0 / math.pi) * (gate + 0.044715 * (gate * gate * gate))))
    o_ref[...] = (gelu * val).astype(o_ref.dtype)


def _conv_act(up, halo, conv_w, conv_b, b, s, *, halo_is_up):
    f = up.shape[1] // 2
    ts = min(512, s)
    ns = s // ts
    tf = _pick(f, (256, 128))
    nf = f // tf
    hb = ts // HALO_ROWS

    def blk(off):
        return pl.BlockSpec((ts, tf), lambda bi, si, fi: (bi * ns + si, off + fi))

    def halo_blk(off):
        if halo_is_up:
            return pl.BlockSpec((HALO_ROWS, tf),
                                lambda bi, si, fi: (jnp.maximum((bi * ns + si) * hb - 1, 0), off + fi))
        return pl.BlockSpec((None, HALO_ROWS, tf), lambda bi, si, fi: (bi, 0, off + fi))

    def vec(rows, off):
        return pl.BlockSpec((rows, tf), lambda bi, si, fi: (0, off + fi))

    cw = jnp.pad(conv_w, ((0, 8 - CONV_W), (0, 0)))
    return pl.pallas_call(
        functools.partial(_conv_act_kernel, halo_is_up=halo_is_up),
        out_shape=jax.ShapeDtypeStruct((b * s, f), BF16),
        grid=(b, ns, nf),
        in_specs=[blk(0), blk(nf), halo_blk(0), halo_blk(nf), vec(8, 0), vec(8, nf), vec(1, 0), vec(1, nf)],
        out_specs=pl.BlockSpec((ts, tf), lambda bi, si, fi: (bi * ns + si, fi)),
        compiler_params=_params("arbitrary", "arbitrary", "arbitrary"),
        name="conv_act",
    )(up, up, halo, halo, cw, cw, conv_b.reshape(1, 2 * f), conv_b.reshape(1, 2 * f))


def _rope_tables(pos):
    half = HEAD_DIM // 2
    inv = ROPE_THETA ** (-jnp.arange(half, dtype=F32) / half)
    ang = pos.astype(F32)[:, None] * inv[None, :]
    cos, sin = jnp.cos(ang), jnp.sin(ang)
    return jnp.concatenate([cos, cos], axis=1), jnp.concatenate([-sin, sin], axis=1)


def _state_to_tiles(wkv):
    b, h, n, _ = wkv.shape
    t = wkv.reshape(b, h // 2, 2, n, n).transpose(0, 1, 2, 4, 3)
    z = t[:, :, :, :, None, :] * jnp.eye(2, dtype=wkv.dtype)[None, None, :, None, :, None]
    return z.reshape(b, h // 2, 2 * n, 2 * n)


def _tiles_to_state(z, h):
    b = z.shape[0]
    n = RW_HEAD
    z = z.reshape(b, h // 2, 2, n, 2, n)
    t = jnp.stack([z[:, :, 0, :, 0, :], z[:, :, 1, :, 1, :]], axis=2)
    return t.transpose(0, 1, 2, 4, 3).reshape(b, h, n, n)


def _attn_layer(x, xn, b, s, g, w_qkv, w_o, sinks, pos, cache=None):
    d = x.shape[1]
    kv_dim = d // N_GROUPS
    cos, sin = _rope_tables(pos)
    cos, sin = jnp.tile(cos, (b, 1)), jnp.tile(sin, (b, 1))
    if xn is None:
        qkv = _mm(x, w_qkv, BF16, g=g, rope=(cos, sin, d + kv_dim))
    else:
        qkv = _mm(xn, w_qkv, BF16, rope=(cos, sin, d + kv_dim))
    if cache is None:
        o = _attn_prompt(qkv, sinks, b, s, d)
    else:
        o = _attn_sample(qkv, cache[0].reshape(b, -1, kv_dim), cache[1].reshape(b, -1, kv_dim), sinks, b, s, d)
    k3 = qkv[:, d:d + kv_dim].reshape(b, s, kv_dim // HEAD_DIM, HEAD_DIM).astype(F32)
    v3 = qkv[:, d + kv_dim:].reshape(b, s, kv_dim // HEAD_DIM, HEAD_DIM).astype(F32)
    return _mm(o, w_o, BF16), k3, v3


def _rwkv_layer(x, b, s, g, shift, wkv, p):
    d = x.shape[1]
    heads = d // RW_HEAD
    if shift is None:
        mixed = _rw_mix(x, x, g, p["mix"], b, s, halo_is_x=True)
        s0 = jnp.zeros((b, heads // 2, 2 * RW_HEAD, 2 * RW_HEAD), F32)
    else:
        halo = jnp.pad(shift.reshape(b, 1, d), ((0, 0), (7, 0), (0, 0)))
        mixed = _rw_mix(x, halo, g, p["mix"], b, s, halo_is_x=False)
        s0 = _state_to_tiles(wkv.astype(F32))
    xr, xw, xk, xv, xa, xg, tail = mixed
    r = _mm(xr, p["w_r"], BF16)
    k = _mm(xk, p["w_k"], BF16)
    v = _mm(xv, p["w_v"], BF16)
    lw = _mm(_mm(xw, p["w1"], BF16, act="tanh"), p["w2"], F32)
    la = _mm(_mm(xa, p["a1"], BF16), p["a2"], F32)
    gate = _mm(_mm(xg, p["g1"], BF16, act="sigmoid"), p["g2"], BF16)
    kk, k2, ld, kka = _rw_prep(k, lw, la, p["w0"], p["a0"], p["k_k"], p["k_a"])
    y, s_t = _rw_scan(r, ld, k2, v, kk, kka, s0, b, s)
    o = _rw_post(y, r, k2, v, gate, p["lnx"], p["r_k"].reshape(-1))
    return _mm(o, p["w_o"], BF16), tail[:, 7:8, :], _tiles_to_state(s_t, heads)


def _ffn_layer(xn, b, s, hist, w_up, conv_w, conv_b, w_down):
    f2 = w_up.shape[1]
    up = _mm(xn, w_up, BF16)
    if hist is None:
        h = _conv_act(up, up, conv_w, conv_b, b, s, halo_is_up=True)
    else:
        halo = jnp.pad(hist, ((0, 0), (HALO_ROWS - (CONV_W - 1), 0), (0, 0)))
        h = _conv_act(up, halo, conv_w, conv_b, b, s, halo_is_up=False)
    tail = up.reshape(b, s, f2)[:, s - (CONV_W - 1):, :].astype(F32)
    return _mm(h, w_down, BF16), tail


def kernel(x_prompt, x_sample, cache_k, cache_v, state_pool, state_shift, state_wkv, state_conv, norm_g, attn_w_qkv, attn_w_o, attn_sinks, pool_w, pool_scale, rw_mix, rw_w_r, rw_w_k, rw_w_v, rw_w_o, rw_w0, rw_w1, rw_w2, rw_a0, rw_a1, rw_a2, rw_g1, rw_g2, rw_k_k, rw_k_a, rw_r_k, rw_lnx, ffn_w_up, ffn_conv_w, ffn_conv_b, ffn_w_down):
    bp, sp, d = x_prompt.shape
    bs, ss, _ = x_sample.shape
    depth = norm_g.shape[0]
    window = cache_k.shape[2]
    xp = x_prompt.reshape(bp * sp, d)
    xs = x_sample.reshape(bs * ss, d)
    outs = {n: [] for n in ("kp", "vp", "kn", "vn", "poolp", "pools", "shp", "shs", "wkvp", "wkvs", "convp", "convs")}
    xnp = xns = None
    for i in range(depth):
        kind, j = i % 3, i // 3
        g = norm_g[i]
        g_after = norm_g[i + 1, 0] if i + 1 < depth else None
        if kind == 0:
            wq, wo = attn_w_qkv[j].astype(BF16), attn_w_o[j].astype(BF16)
            mp, kp, vp = _attn_layer(xp, xnp, bp, sp, g[0], wq, wo, attn_sinks[j], jnp.arange(sp))
            ms, kn, vn = _attn_layer(xs, xns, bs, ss, g[0], wq, wo, attn_sinks[j], PAST_LEN + jnp.arange(ss),
                                     cache=(cache_k[j], cache_v[j]))
            outs["kp"].append(kp[:, sp - window:])
            outs["vp"].append(vp[:, sp - window:])
            outs["kn"].append(kn)
            outs["vn"].append(vn)
        elif kind == 1:
            wp = pool_w[j].astype(BF16)
            mp, hp = _pool(xp, xp, g[0], wp, pool_scale[j], bp, sp, halo_is_x=True, pos0=0)
            halo = jnp.pad(state_pool[j], ((0, 0), (HALO_ROWS - POOL_HIST, 0), (0, 0)))
            ms, hs = _pool(xs, halo, g[0], wp, pool_scale[j], bs, ss, halo_is_x=False, pos0=PAST_LEN)
            outs["poolp"].append(hp[:, HALO_ROWS - POOL_HIST:])
            outs["pools"].append(hs[:, HALO_ROWS - POOL_HIST:])
        else:
            p = dict(mix=rw_mix[j], w_r=rw_w_r[j].astype(BF16), w_k=rw_w_k[j].astype(BF16),
                     w_v=rw_w_v[j].astype(BF16), w_o=rw_w_o[j].astype(BF16), w0=rw_w0[j],
                     w1=rw_w1[j].astype(BF16), w2=rw_w2[j].astype(BF16), a0=rw_a0[j],
                     a1=rw_a1[j].astype(BF16), a2=rw_a2[j].astype(BF16), g1=rw_g1[j].astype(BF16),
                     g2=rw_g2[j].astype(BF16), k_k=rw_k_k[j], k_a=rw_k_a[j], r_k=rw_r_k[j], lnx=rw_lnx[j])
            mp, shp, wkvp = _rwkv_layer(xp, bp, sp, g[0], None, None, p)
            ms, shs, wkvs = _rwkv_layer(xs, bs, ss, g[0], state_shift[j], state_wkv[j], p)
            outs["shp"].append(shp)
            outs["shs"].append(shs)
            outs["wkvp"].append(wkvp)
            outs["wkvs"].append(wkvs)
        xp, xnp = _add_norm(xp, mp, g[1], g[2])
        xs, xns = _add_norm(xs, ms, g[1], g[2])
        w_up, w_down = ffn_w_up[i].astype(BF16), ffn_w_down[i].astype(BF16)
        fp, cp = _ffn_layer(xnp, bp, sp, None, w_up, ffn_conv_w[i], ffn_conv_b[i], w_down)
        fs, cs = _ffn_layer(xns, bs, ss, state_conv[i], w_up, ffn_conv_w[i], ffn_conv_b[i], w_down)
        outs["convp"].append(cp)
        outs["convs"].append(cs)
        if g_after is not None and (i + 1) % 3 == 0:
            xp, xnp = _add_norm(xp, fp, g[3], g_after)
            xs, xns = _add_norm(xs, fs, g[3], g_after)
        else:
            xp, xnp = _add_norm(xp, fp, g[3]), None
            xs, xns = _add_norm(xs, fs, g[3]), None
    st = lambda n: jnp.stack(outs[n])
    return (xp.reshape(bp, sp, d), xs.reshape(bs, ss, d),
            st("kp"), st("vp"), st("poolp"), st("shp"), st("wkvp"), st("convp"),
            st("kn"), st("vn"), st("pools"), st("shs"), st("wkvs"), st("convs"))
```

```python
import functools
import math

import jax
import jax.numpy as jnp
from jax import lax
from jax.experimental import pallas as pl
from jax.experimental.pallas import tpu as pltpu

BF16 = jnp.bfloat16
F32 = jnp.float32

CHUNK = 64
HEAD_DIM = 128
N_GROUPS = 4
ROPE_THETA = 10000.0
POOL_WINDOWS = (2, 4, 8, 16)
POOL_HIST = 15
RW_HEAD = 64
LNX_EPS = 64e-5
NORM_EPS = 1e-6
PAST_LEN = 2048
CONV_W = 3
LOG_DECAY_MIN = -math.exp(-0.5)
RW_CHUNK = 64
RW_TILES_PER_STEP = 16
SOLVE_BLOCK = 16
CONV_ACT_BLOCK_BYTES = 2 * 1024 * 1024
FFN_COLS = 256
FFN_ROWS = 512

LANES = 128
VMEM_LIMIT_BYTES = 56 * 1024 * 1024
HALO_ROWS = 16


def _params(*sem):
    return pltpu.CompilerParams(dimension_semantics=sem, vmem_limit_bytes=VMEM_LIMIT_BYTES)


def _pick(n, prefs):
    for p in prefs:
        if n % p == 0:
            return p
    return n


def _rms(xf, g):
    return xf * lax.rsqrt(jnp.mean(xf * xf, axis=-1, keepdims=True) + NORM_EPS) * g


def _seg_ones(rows):
    r = lax.broadcasted_iota(jnp.int32, (rows, LANES), 0)
    c = lax.broadcasted_iota(jnp.int32, (rows, LANES), 1)
    return ((r % LANES) // RW_HEAD == c // RW_HEAD).astype(BF16)


def _seg_sum(p, jj):
    hi = p.astype(BF16)
    lo = (p - hi.astype(F32)).astype(BF16)
    return jnp.dot(jnp.concatenate([hi, lo], axis=1), jj, preferred_element_type=F32)


def _mm_kernel(*refs, norm, rope_blocks, act):
    it = iter(refs)
    x_ref = next(it)
    g_ref = next(it) if norm else None
    w_ref = next(it)
    cos_ref = next(it) if rope_blocks else None
    sin_ref = next(it) if rope_blocks else None
    o_ref = next(it)
    xn_ref = next(it) if norm else None
    j = pl.program_id(1)

    if norm:
        @pl.when(j == 0)
        def _():
            xn_ref[...] = _rms(x_ref[...].astype(F32), g_ref[...]).astype(BF16)
        lhs = xn_ref[...]
    else:
        lhs = x_ref[...]
    acc = jnp.dot(lhs, w_ref[...], preferred_element_type=F32)
    if act == "tanh":
        acc = jnp.tanh(acc)
    elif act == "sigmoid":
        acc = jax.nn.sigmoid(acc)

    if rope_blocks:
        @pl.when(j < rope_blocks)
        def _():
            cos = cos_ref[...]
            sin = sin_ref[...]
            for h in range(acc.shape[1] // HEAD_DIM):
                xh = acc[:, h * HEAD_DIM:(h + 1) * HEAD_DIM]
                o_ref[:, h * HEAD_DIM:(h + 1) * HEAD_DIM] = (
                    xh * cos + pltpu.roll(xh, HEAD_DIM // 2, 1) * sin).astype(o_ref.dtype)

        @pl.when(j >= rope_blocks)
        def _():
            o_ref[...] = acc.astype(o_ref.dtype)
    else:
        o_ref[...] = acc.astype(o_ref.dtype)


def _mm(x, w, out_dtype, *, g=None, rope=None, act=None):
    m, k = x.shape
    n = w.shape[1]
    norm = g is not None
    if norm or k > 8192:
        tm = min(512, m)
    else:
        tm = min(1024, m)
    tn = _pick(math.gcd(n, rope[2]) if rope else n, (256,) if k > 8192 else (512, 256, 128))
    rope_blocks = 0
    in_specs = [pl.BlockSpec((tm, k), lambda i, j: (i, 0))]
    args = [x]
    if norm:
        in_specs.append(pl.BlockSpec((1, k), lambda i, j: (0, 0)))
        args.append(g.reshape(1, k).astype(F32))
    in_specs.append(pl.BlockSpec((k, tn), lambda i, j: (0, j)))
    args.append(w)
    if rope is not None:
        cos, sin, n_cols = rope
        assert n_cols % tn == 0 and tn % HEAD_DIM == 0
        rope_blocks = n_cols // tn
        in_specs += [pl.BlockSpec((tm, HEAD_DIM), lambda i, j: (i, 0))] * 2
        args += [cos, sin]
    scratch = [pltpu.VMEM((tm, k), BF16)] if norm else []
    return pl.pallas_call(
        functools.partial(_mm_kernel, norm=norm, rope_blocks=rope_blocks, act=act),
        out_shape=jax.ShapeDtypeStruct((m, n), out_dtype),
        grid=(m // tm, n // tn),
        in_specs=in_specs,
        out_specs=pl.BlockSpec((tm, tn), lambda i, j: (i, j)),
        scratch_shapes=scratch,
        compiler_params=_params("arbitrary", "arbitrary"),
        name="mm",
    )(*args)


def _add_norm_kernel(x_ref, m_ref, g_ref, *refs):
    x = x_ref[...] + _rms(m_ref[...].astype(F32), g_ref[...])
    if len(refs) == 1:
        refs[0][...] = x
    else:
        gn_ref, o_ref, on_ref = refs
        o_ref[...] = x
        on_ref[...] = _rms(x, gn_ref[...]).astype(on_ref.dtype)


def _add_norm(x, mix, g, g_next=None):
    m, d = x.shape
    tr = min(256, m)
    row = pl.BlockSpec((tr, d), lambda i: (i, 0))
    vec = pl.BlockSpec((1, d), lambda i: (0, 0))
    with_next = g_next is not None
    return pl.pallas_call(
        _add_norm_kernel,
        out_shape=((jax.ShapeDtypeStruct((m, d), F32), jax.ShapeDtypeStruct((m, d), BF16)) if with_next
                   else jax.ShapeDtypeStruct((m, d), F32)),
        grid=(m // tr,),
        in_specs=[row, row, vec] + ([vec] if with_next else []),
        out_specs=(row, row) if with_next else row,
        compiler_params=_params("arbitrary"),
        name="add_norm",
    )(x, mix, g.reshape(1, d), *([g_next.reshape(1, d)] if with_next else []))


def _attn_kernel(sink_ref, q_ref, *refs, n_pieces, n_kv, band_mask):
    k_refs = refs[:n_pieces]
    v_refs = refs[n_pieces:2 * n_pieces]
    o_ref = refs[2 * n_pieces]
    tq = q_ref.shape[0]
    c = pl.program_id(1)
    scale = HEAD_DIM ** -0.5
    scores = []
    for kv in range(n_kv):
        lo, hi = kv * HEAD_DIM, (kv + 1) * HEAD_DIM
        q4 = jnp.concatenate(
            [q_ref[:, (kv * N_GROUPS + r) * HEAD_DIM:(kv * N_GROUPS + r + 1) * HEAD_DIM] for r in range(N_GROUPS)],
            axis=0)
        kb = jnp.concatenate([kr[:, lo:hi].astype(BF16) for kr in k_refs], axis=0)
        scores.append(lax.dot_general(q4, kb, (((1,), (1,)), ((), ())), preferred_element_type=F32))
    s = jnp.concatenate(scores, axis=0) * scale
    if band_mask:
        col = lax.broadcasted_iota(jnp.int32, s.shape, 1)
        s = jnp.where(col >= CHUNK * (n_pieces - 1 - c), s, -1e30)
    sink = jnp.concatenate([jnp.full((tq, 1), sink_ref[h], F32) for h in range(n_kv * N_GROUPS)], axis=0)
    mx = jnp.maximum(jnp.max(s, axis=-1, keepdims=True), sink)
    p = jnp.exp(s - mx)
    denom = jnp.sum(p, axis=-1, keepdims=True) + jnp.exp(sink - mx)
    pb = p.astype(BF16)
    inv = 1.0 / denom
    rows = N_GROUPS * tq
    for kv in range(n_kv):
        vb = jnp.concatenate([vr[:, kv * HEAD_DIM:(kv + 1) * HEAD_DIM].astype(BF16) for vr in v_refs], axis=0)
        o = jnp.dot(pb[kv * rows:(kv + 1) * rows], vb, preferred_element_type=F32) * inv[kv * rows:(kv + 1) * rows]
        for r in range(N_GROUPS):
            h = kv * N_GROUPS + r
            o_ref[:, h * HEAD_DIM:(h + 1) * HEAD_DIM] = o[r * tq:(r + 1) * tq].astype(o_ref.dtype)


def _attn_prompt(qkv, sinks, b, s, d):
    kv_dim = d // N_GROUPS
    n_kv = kv_dim // HEAD_DIM
    nc = s // CHUNK
    n_band = 3
    kcol, vcol = d // kv_dim, d // kv_dim + 1

    def piece(jj, col):
        return pl.BlockSpec((CHUNK, kv_dim),
                            lambda bi, c, sk: (bi * nc + jnp.maximum(c - (n_band - 1) + jj, 0), col))

    in_specs = [pl.BlockSpec((CHUNK, d), lambda bi, c, sk: (bi * nc + c, 0))]
    in_specs += [piece(jj, kcol) for jj in range(n_band)] + [piece(jj, vcol) for jj in range(n_band)]
    return pl.pallas_call(
        functools.partial(_attn_kernel, n_pieces=n_band, n_kv=n_kv, band_mask=True),
        out_shape=jax.ShapeDtypeStruct((b * s, d), BF16),
        grid_spec=pltpu.PrefetchScalarGridSpec(
            num_scalar_prefetch=1, grid=(b, nc), in_specs=in_specs,
            out_specs=pl.BlockSpec((CHUNK, d), lambda bi, c, sk: (bi * nc + c, 0))),
        compiler_params=_params("arbitrary", "arbitrary"),
        name="attn_prompt",
    )(sinks, *([qkv] * (1 + 2 * n_band)))


def _attn_sample(qkv, cache_k, cache_v, sinks, b, t, d):
    kv_dim = d // N_GROUPS
    n_kv = kv_dim // HEAD_DIM
    win = cache_k.shape[1]
    kcol, vcol = d // kv_dim, d // kv_dim + 1
    in_specs = [
        pl.BlockSpec((t, d), lambda bi, c, sk: (bi, 0)),
        pl.BlockSpec((None, win, kv_dim), lambda bi, c, sk: (bi, 0, 0)),
        pl.BlockSpec((t, kv_dim), lambda bi, c, sk: (bi, kcol)),
        pl.BlockSpec((None, win, kv_dim), lambda bi, c, sk: (bi, 0, 0)),
        pl.BlockSpec((t, kv_dim), lambda bi, c, sk: (bi, vcol)),
    ]
    return pl.pallas_call(
        functools.partial(_attn_kernel, n_pieces=2, n_kv=n_kv, band_mask=False),
        out_shape=jax.ShapeDtypeStruct((b * t, d), BF16),
        grid_spec=pltpu.PrefetchScalarGridSpec(
            num_scalar_prefetch=1, grid=(b, 1), in_specs=in_specs,
            out_specs=pl.BlockSpec((t, d), lambda bi, c, sk: (bi, 0))),
        compiler_params=_params("arbitrary", "arbitrary"),
        name="attn_sample",
    )(sinks, qkv, cache_k, qkv, cache_v, qkv)


def _pool_kernel(x_ref, halo_ref, g_ref, w_ref, sc_ref, o_ref, st_ref, *, halo_is_x, pos0):
    si = pl.program_id(1)
    ts, d = x_ref.shape
    gw = d // len(POOL_WINDOWS)
    g = g_ref[...]
    un = _rms(x_ref[...], g)
    if halo_is_x:
        halo = jnp.where(si > 0, _rms(halo_ref[...], g), 0.0)
    else:
        halo = halo_ref[...]
    full = jnp.concatenate([halo, un], axis=0)
    pos = pos0 + si * ts + lax.broadcasted_iota(jnp.int32, (ts, 1), 0)
    for gi, win in enumerate(POOL_WINDOWS):
        lo, hi = gi * gw, (gi + 1) * gw
        acc = full[:, lo:hi]
        span = 1
        while span < win:
            acc = acc + pltpu.roll(acc, span, 0)
            span *= 2
        cnt = jnp.minimum(win, pos + 1).astype(F32)
        dlt = (acc[HALO_ROWS:] / cnt - un[:, lo:hi]).astype(BF16)
        y = jnp.dot(dlt, w_ref[gi], preferred_element_type=F32)
        o_ref[:, lo:hi] = (y * sc_ref[:, lo:hi]).astype(o_ref.dtype)
    st_ref[...] = full[ts:]


def _pool(x, halo, g, w, scale, b, s, *, halo_is_x, pos0):
    d = x.shape[1]
    ts = min(256, s)
    ns = s // ts
    hb = ts // HALO_ROWS
    if halo_is_x:
        halo_spec = pl.BlockSpec((HALO_ROWS, d), lambda bi, si: (jnp.maximum((bi * ns + si) * hb - 1, 0), 0))
    else:
        halo_spec = pl.BlockSpec((None, HALO_ROWS, d), lambda bi, si: (bi, 0, 0))
    ng, gw = w.shape[0], w.shape[1]
    return pl.pallas_call(
        functools.partial(_pool_kernel, halo_is_x=halo_is_x, pos0=pos0),
        out_shape=(jax.ShapeDtypeStruct((b * s, d), BF16), jax.ShapeDtypeStruct((b, HALO_ROWS, d), F32)),
        grid=(b, ns),
        in_specs=[pl.BlockSpec((ts, d), lambda bi, si: (bi * ns + si, 0)),
                  halo_spec,
                  pl.BlockSpec((1, d), lambda bi, si: (0, 0)),
                  pl.BlockSpec((ng, gw, gw), lambda bi, si: (0, 0, 0)),
                  pl.BlockSpec((1, d), lambda bi, si: (0, 0))],
        out_specs=(pl.BlockSpec((ts, d), lambda bi, si: (bi * ns + si, 0)),
                   pl.BlockSpec((None, HALO_ROWS, d), lambda bi, si: (bi, 0, 0))),
        compiler_params=_params("arbitrary", "arbitrary"),
        name="pool",
    )(x, halo, g.reshape(1, d), w, scale.reshape(1, d))


def _rw_mix_kernel(x_ref, halo_ref, g_ref, mix_ref, *o_refs, halo_is_x):
    si = pl.program_id(1)
    g = g_ref[...]
    un = _rms(x_ref[...], g)
    n_h = halo_ref.shape[0]
    if halo_is_x:
        prev_row = jnp.where(si > 0, _rms(halo_ref[n_h - 1:n_h, :], g), 0.0)
    else:
        prev_row = halo_ref[n_h - 1:n_h, :]
    row = lax.broadcasted_iota(jnp.int32, (un.shape[0], 1), 0)
    prev = jnp.where(row == 0, prev_row, pltpu.roll(un, 1, 0))
    xx = prev - un
    for jm in range(6):
        o_refs[jm][...] = (un + xx * mix_ref[jm:jm + 1, :]).astype(BF16)
    st_ref = o_refs[6]
    st_ref[...] = un[un.shape[0] - st_ref.shape[0]:]


def _rw_mix(x, halo, g, mix, b, s, *, halo_is_x):
    d = x.shape[1]
    ts = min(256, s)
    ns = s // ts
    hr = 8
    if halo_is_x:
        halo_spec = pl.BlockSpec((hr, d), lambda bi, si: (jnp.maximum((bi * ns + si) * (ts // hr) - 1, 0), 0))
    else:
        halo_spec = pl.BlockSpec((None, hr, d), lambda bi, si: (bi, 0, 0))
    row_spec = pl.BlockSpec((ts, d), lambda bi, si: (bi * ns + si, 0))
    return pl.pallas_call(
        functools.partial(_rw_mix_kernel, halo_is_x=halo_is_x),
        out_shape=tuple([jax.ShapeDtypeStruct((b * s, d), BF16)] * 6 + [jax.ShapeDtypeStruct((b, hr, d), F32)]),
        grid=(b, ns),
        in_specs=[row_spec, halo_spec,
                  pl.BlockSpec((1, d), lambda bi, si: (0, 0)),
                  pl.BlockSpec((8, d), lambda bi, si: (0, 0))],
        out_specs=tuple([row_spec] * 6 + [pl.BlockSpec((None, hr, d), lambda bi, si: (bi, 0, 0))]),
        compiler_params=_params("arbitrary", "arbitrary"),
        name="rw_mix",
    )(x, halo, g.reshape(1, d), jnp.pad(mix, ((0, 2), (0, 0))))


def _dot_nt(a, b):
    return lax.dot_general(a, b, (((1,), (1,)), ((), ())), preferred_element_type=F32)


def _rw_chunk(r, ld, k, v, kk, kka, zs):
    c, width = r.shape
    n = RW_HEAD
    tiles = [slice(g * LANES, (g + 1) * LANES) for g in range(width // LANES)]
    lanes = lambda parts: jnp.concatenate(parts, axis=1) if len(parts) > 1 else parts[0]
    row = lax.broadcasted_iota(jnp.int32, (c, 2 * c), 0)
    col = lax.broadcasted_iota(jnp.int32, (c, 2 * c), 1)
    s_idx = col % c
    first = col < c
    strict = s_idx < row
    incl = s_idx <= row
    head0 = lax.broadcasted_iota(jnp.int32, (c, width), 1) % LANES < n

    tri = (lax.broadcasted_iota(jnp.int32, (c, 3 * c), 1) % c <= lax.broadcasted_iota(jnp.int32, (c, 3 * c), 0))
    hi = ld.astype(BF16)
    mid = (ld - hi.astype(F32)).astype(BF16)
    lo = (ld - hi.astype(F32) - mid.astype(F32)).astype(BF16)
    cum = jnp.dot(tri.astype(BF16), jnp.concatenate([hi, mid, lo], axis=0), preferred_element_type=F32)

    e_cum = jnp.exp(cum)
    e_inv = jnp.exp(-cum)
    at = -kk * jnp.exp(cum - ld)
    rt = r * e_cum
    bt = kka * e_inv
    kt = k * e_inv
    w_last = e_cum[c - 1:c, :]

    zero = jnp.zeros_like(at)
    a0r0 = jnp.concatenate([jnp.where(head0, at, zero), jnp.where(head0, rt, zero)], axis=0).astype(BF16)
    a1 = jnp.where(head0, zero, at).astype(BF16)
    r1 = jnp.where(head0, zero, rt).astype(BF16)
    ar = jnp.concatenate([at, rt], axis=0).astype(BF16)
    bk = jnp.concatenate([bt, kt], axis=0).astype(BF16)
    kb = jnp.concatenate([kt, bt], axis=0).astype(BF16)
    vb = v.astype(BF16)
    vv = jnp.concatenate([vb, vb], axis=0)

    l2, g0r, g1r, mv, p = [], [], [], [], []
    for t, z in zip(tiles, zs):
        g0 = _dot_nt(a0r0[:, t], bk[:, t])
        g1a = _dot_nt(a1[:, t], kb[:, t])
        g1r.append(_dot_nt(r1[:, t], bk[:, t]))
        g0a = g0[:c]
        g0r.append(g0[c:])
        l2.append(jnp.where(strict, jnp.where(first, g0a, g1a), 0.0))
        mak = jnp.concatenate([jnp.where(strict & ~first, g0a, 0.0), jnp.where(strict & first, g1a, 0.0)], axis=0)
        mv.append(jnp.dot(mak.astype(BF16), vv[:, t], preferred_element_type=F32))
        p.append(jnp.dot(ar[:, t], z.astype(BF16), preferred_element_type=F32))
    mv, p = lanes(mv), lanes(p)
    u0 = p[:c] + jnp.where(head0, mv[:c], mv[c:])

    sb = SOLVE_BLOCK
    nb = c // sb
    spread = (lax.broadcasted_iota(jnp.int32, (2 * c, LANES), 0) // c
              == lax.broadcasted_iota(jnp.int32, (2 * c, LANES), 1) // n).astype(BF16)
    lane_b = lax.broadcasted_iota(jnp.int32, (sb, 2 * c), 1) % c
    first_b = lax.broadcasted_iota(jnp.int32, (sb, 2 * c), 1) < c
    head0_b = lax.broadcasted_iota(jnp.int32, (sb, width), 1) % LANES < n
    lcol = lanes([
        jnp.dot(jnp.concatenate([jnp.where(lane_b == blk * sb + s, l2g[blk * sb:(blk + 1) * sb], 0.0).astype(BF16)
                                 for blk in range(nb) for s in range(sb - 1)], axis=0),
                spread, preferred_element_type=F32) for l2g in l2])
    solved = []
    for blk in range(nb):
        ub = u0[blk * sb:(blk + 1) * sb]
        if blk > 0:
            done = jnp.concatenate(solved + [jnp.zeros((c - blk * sb, width), F32)], axis=0).astype(BF16)
            done = jnp.concatenate([done, done], axis=0)
            off = []
            for t, l2g in zip(tiles, l2):
                band = l2g[blk * sb:(blk + 1) * sb]
                lhs = jnp.concatenate([jnp.where(first_b, band, 0.0), jnp.where(first_b, 0.0, band)], axis=0)
                off.append(jnp.dot(lhs.astype(BF16), done[:, t], preferred_element_type=F32))
            off = lanes(off)
            ub = ub + jnp.where(head0_b, off[:sb], off[sb:])
        for s in range(sb - 1):
            i = blk * (sb - 1) + s
            ub = ub + lcol[i * sb:(i + 1) * sb] * ub[s:s + 1, :]
        solved.append(ub)
    u = jnp.concatenate(solved, axis=0)

    uv = jnp.concatenate([u, v], axis=0).astype(BF16)
    bkw = jnp.concatenate([bt * w_last, kt * w_last], axis=0)
    same_head = (lax.broadcasted_iota(jnp.int32, (LANES, LANES), 0) // n
                 == lax.broadcasted_iota(jnp.int32, (LANES, LANES), 1) // n)
    ny, z_new = [], []
    for t, z, g0rg, g1rg in zip(tiles, zs, g0r, g1r):
        nmat = jnp.concatenate([jnp.where(incl, g0rg, 0.0), jnp.where(incl, g1rg, 0.0)], axis=0).astype(BF16)
        ny.append(jnp.dot(nmat, uv[:, t], preferred_element_type=F32))
        upd_z = jnp.dot(bkw[:, t].T.astype(BF16), uv[:, t], preferred_element_type=F32)
        w_col = jnp.broadcast_to(w_last[:, t], (LANES, LANES)).T
        z_new.append(z * w_col + jnp.where(same_head, upd_z, 0.0))
    ny = lanes(ny)
    return p[c:] + jnp.where(head0, ny[:c], ny[c:]), z_new


def _seg_sum_wide(p, jj):
    parts = [_seg_sum(p[:, g * LANES:(g + 1) * LANES], jj) for g in range(p.shape[1] // LANES)]
    return jnp.concatenate(parts, axis=1) if len(parts) > 1 else parts[0]


_PV_W0, _PV_A0, _PV_KK, _PV_KA, _PV_RK, _PV_LN_G, _PV_LN_B = range(7)


def _rw_scan_kernel(r_ref, k_ref, v_ref, lw_ref, la_ref, g_ref, pv_ref, s0_ref, o_ref, sT_ref, z_ref):
    @pl.when(pl.program_id(2) == 0)
    def _():
        z_ref[...] = s0_ref[...]

    pv = lambda i: pv_ref[i:i + 1, :]
    jj = _seg_ones(2 * LANES)
    r = r_ref[...].astype(F32)
    k = k_ref[...].astype(F32)
    v = v_ref[...].astype(F32)
    a = jax.nn.sigmoid(pv(_PV_A0) + la_ref[...])
    ld = LOG_DECAY_MIN * jax.nn.sigmoid(pv(_PV_W0) + lw_ref[...])
    k2 = k * (1.0 + (a - 1.0) * pv(_PV_KA))
    kkr = k * pv(_PV_KK)
    kk = kkr / jnp.maximum(jnp.sqrt(_seg_sum_wide(kkr * kkr, jj)), 1e-12)

    tg = z_ref.shape[0]
    y, zs = _rw_chunk(r, ld, k2, v, kk, kk * a, [z_ref[g] for g in range(tg)])
    for g in range(tg):
        z_ref[g] = zs[g]
    sT_ref[...] = z_ref[...]

    yc = y - _seg_sum_wide(y, jj) * (1.0 / RW_HEAD)
    var = _seg_sum_wide(yc * yc, jj) * (1.0 / RW_HEAD)
    o = yc * lax.rsqrt(var + LNX_EPS) * pv(_PV_LN_G) + pv(_PV_LN_B)
    bonus = _seg_sum_wide(r * k2 * pv(_PV_RK), jj) * v
    o_ref[...] = ((o + bonus) * g_ref[...].astype(F32)).astype(o_ref.dtype)


def _rw_scan(r, k, v, lw, la, gate, pvec, s0, b, s):
    d = r.shape[1]
    ntile = d // LANES
    tg = min(RW_TILES_PER_STEP, ntile)
    tc = min(RW_CHUNK, s)
    nch = s // tc
    seq = pl.BlockSpec((tc, tg * LANES), lambda bi, gi, ci: (bi * nch + ci, gi))
    st = pl.BlockSpec((None, tg, LANES, LANES), lambda bi, gi, ci: (bi, gi, 0, 0))
    return pl.pallas_call(
        _rw_scan_kernel,
        out_shape=(jax.ShapeDtypeStruct((b * s, d), BF16), jax.ShapeDtypeStruct(s0.shape, F32)),
        grid=(b, ntile // tg, nch),
        in_specs=[seq] * 6 + [pl.BlockSpec((8, tg * LANES), lambda bi, gi, ci: (0, gi)), st],
        out_specs=(seq, st),
        scratch_shapes=[pltpu.VMEM((tg, LANES, LANES), F32)],
        compiler_params=_params("arbitrary", "arbitrary", "arbitrary"),
        name="rw_scan",
    )(r, k, v, lw, la, gate, pvec, s0)


def _gelu_tanh(x):
    return 0.5 * x * (1.0 + jnp.tanh(math.sqrt(2.0 / math.pi) * (x + 0.044715 * (x * x * x))))


def _conv_taps(full, cw_ref, cb_ref, cols):
    return (cb_ref[:, cols] + full * cw_ref[2:3, cols] + pltpu.roll(full, 1, 0) * cw_ref[1:2, cols]
            + pltpu.roll(full, 2, 0) * cw_ref[0:1, cols])


def _conv_act_kernel(u_ref, h_ref, cw_ref, cb_ref, o_ref, *, tf):
    for j in range(o_ref.shape[1] // tf):
        cols = slice(2 * j * tf, 2 * (j + 1) * tf)
        full = jnp.concatenate([h_ref[:, cols], u_ref[:, cols].astype(F32)], axis=0)
        c = _conv_taps(full, cw_ref, cb_ref, cols)[HALO_ROWS:]
        o_ref[:, j * tf:(j + 1) * tf] = (_gelu_tanh(c[:, :tf]) * c[:, tf:]).astype(o_ref.dtype)


def _conv_act(up, halo, cw_pair, cb_pair, b, s, tf):
    f = up.shape[1] // 2
    return pl.pallas_call(
        functools.partial(_conv_act_kernel, tf=tf),
        out_shape=jax.ShapeDtypeStruct((b * s, f), BF16),
        grid=(b,),
        in_specs=[pl.BlockSpec((s, 2 * f), lambda bi: (bi, 0)),
                  pl.BlockSpec((None, HALO_ROWS, 2 * f), lambda bi: (bi, 0, 0)),
                  pl.BlockSpec((8, 2 * f), lambda bi: (0, 0)),
                  pl.BlockSpec((1, 2 * f), lambda bi: (0, 0))],
        out_specs=pl.BlockSpec((s, f), lambda bi: (bi, 0)),
        compiler_params=_params("arbitrary"),
        name="conv_act",
    )(up, halo, cw_pair, cb_pair)


def _ffn_up_kernel(x_ref, w_ref, cw_ref, cb_ref, h_ref, tail_ref, up_ref, *, rows):
    s = x_ref.shape[0]
    tf = h_ref.shape[1]
    n_pass = s // rows

    def project(k):
        up_ref[k % 2] = jnp.dot(x_ref[k * rows:(k + 1) * rows, :], w_ref[...], preferred_element_type=F32)

    project(0)
    prev = jnp.zeros((8, 2 * tf), F32)
    for k in range(n_pass):
        if k + 1 < n_pass:
            project(k + 1)
        up = up_ref[k % 2]
        c = _conv_taps(jnp.concatenate([prev, up], axis=0), cw_ref, cb_ref, slice(None))[8:]
        h_ref[k * rows:(k + 1) * rows, :] = (_gelu_tanh(c[:, :tf]) * c[:, tf:]).astype(h_ref.dtype)
        prev = up[rows - 8:]
    tail_ref[...] = prev


def _pair_cols(a, f, tf):
    lead = a.shape[:-1]
    return a.reshape(*lead, 2, f // tf, tf).swapaxes(-3, -2).reshape(*lead, 2 * f)


def _unpair_cols(a, f, tf):
    lead = a.shape[:-1]
    return a.reshape(*lead, f // tf, 2, tf).swapaxes(-3, -2).reshape(*lead, 2 * f)


def _ffn_up(xn, w_pair, cw_pair, cb_pair, b, s):
    d = xn.shape[1]
    f = w_pair.shape[1] // 2
    tf = _pick(f, (FFN_COLS, LANES))
    nf = f // tf
    rows = min(FFN_ROWS, s)
    return pl.pallas_call(
        functools.partial(_ffn_up_kernel, rows=rows),
        out_shape=(jax.ShapeDtypeStruct((b * s, f), BF16), jax.ShapeDtypeStruct((b, 8, 2 * f), F32)),
        grid=(b, nf),
        in_specs=[pl.BlockSpec((s, d), lambda bi, j: (bi, 0)),
                  pl.BlockSpec((d, 2 * tf), lambda bi, j: (0, j)),
                  pl.BlockSpec((8, 2 * tf), lambda bi, j: (0, j)),
                  pl.BlockSpec((1, 2 * tf), lambda bi, j: (0, j))],
        out_specs=(pl.BlockSpec((s, tf), lambda bi, j: (bi, j)),
                   pl.BlockSpec((None, 8, 2 * tf), lambda bi, j: (bi, 0, j))),
        scratch_shapes=[pltpu.VMEM((2, rows, 2 * tf), F32)],
        compiler_params=_params("arbitrary", "arbitrary"),
        name="ffn_up",
    )(xn, w_pair, cw_pair, cb_pair)


def _rope_tables(pos):
    half = HEAD_DIM // 2
    inv = ROPE_THETA ** (-jnp.arange(half, dtype=F32) / half)
    ang = pos.astype(F32)[:, None] * inv[None, :]
    cos, sin = jnp.cos(ang), jnp.sin(ang)
    return jnp.concatenate([cos, cos], axis=1), jnp.concatenate([-sin, sin], axis=1)


def _state_to_tiles(wkv):
    b, h, n, _ = wkv.shape
    t = wkv.reshape(b, h // 2, 2, n, n).transpose(0, 1, 2, 4, 3)
    z = t[:, :, :, :, None, :] * jnp.eye(2, dtype=wkv.dtype)[None, None, :, None, :, None]
    return z.reshape(b, h // 2, 2 * n, 2 * n)


def _tiles_to_state(z, h):
    b = z.shape[0]
    n = RW_HEAD
    z = z.reshape(b, h // 2, 2, n, 2, n)
    t = jnp.stack([z[:, :, 0, :, 0, :], z[:, :, 1, :, 1, :]], axis=2)
    return t.transpose(0, 1, 2, 4, 3).reshape(b, h, n, n)


def _attn_layer(x, xn, b, s, g, w_qkv, w_o, sinks, pos, cache=None):
    d = x.shape[1]
    kv_dim = d // N_GROUPS
    cos, sin = _rope_tables(pos)
    cos, sin = jnp.tile(cos, (b, 1)), jnp.tile(sin, (b, 1))
    if xn is None:
        qkv = _mm(x, w_qkv, BF16, g=g, rope=(cos, sin, d + kv_dim))
    else:
        qkv = _mm(xn, w_qkv, BF16, rope=(cos, sin, d + kv_dim))
    if cache is None:
        o = _attn_prompt(qkv, sinks, b, s, d)
    else:
        o = _attn_sample(qkv, cache[0].reshape(b, -1, kv_dim), cache[1].reshape(b, -1, kv_dim), sinks, b, s, d)
    k3 = qkv[:, d:d + kv_dim].reshape(b, s, kv_dim // HEAD_DIM, HEAD_DIM).astype(F32)
    v3 = qkv[:, d + kv_dim:].reshape(b, s, kv_dim // HEAD_DIM, HEAD_DIM).astype(F32)
    return _mm(o, w_o, BF16), k3, v3


def _rwkv_layer(x, b, s, g, shift, wkv, p):
    d = x.shape[1]
    heads = d // RW_HEAD
    if shift is None:
        mixed = _rw_mix(x, x, g, p["mix"], b, s, halo_is_x=True)
        s0 = jnp.zeros((b, heads // 2, 2 * RW_HEAD, 2 * RW_HEAD), F32)
    else:
        halo = jnp.pad(shift.reshape(b, 1, d), ((0, 0), (7, 0), (0, 0)))
        mixed = _rw_mix(x, halo, g, p["mix"], b, s, halo_is_x=False)
        s0 = _state_to_tiles(wkv.astype(F32))
    xr, xw, xk, xv, xa, xg, tail = mixed
    r = _mm(xr, p["w_r"], BF16)
    k = _mm(xk, p["w_k"], BF16)
    v = _mm(xv, p["w_v"], BF16)
    lw = _mm(_mm(xw, p["w1"], BF16, act="tanh"), p["w2"], F32)
    la = _mm(_mm(xa, p["a1"], BF16), p["a2"], F32)
    gate = _mm(_mm(xg, p["g1"], BF16, act="sigmoid"), p["g2"], BF16)
    pvec = jnp.stack([p["w0"], p["a0"], p["k_k"], p["k_a"], p["r_k"].reshape(-1), p["lnx"][0], p["lnx"][1],
                      jnp.zeros_like(p["w0"])])
    o, s_t = _rw_scan(r, k, v, lw, la, gate, pvec, s0, b, s)
    return _mm(o, p["w_o"], BF16), tail[:, 7:8, :], _tiles_to_state(s_t, heads)


def _ffn_layer(xn, b, s, hist, w_pair, cw_pair, cb_pair, w_down):
    f = w_pair.shape[1] // 2
    tf = _pick(f, (FFN_COLS, LANES))
    if hist is None:
        h, tail = _ffn_up(xn, w_pair, cw_pair, cb_pair, b, s)
        tail = tail[:, 8 - (CONV_W - 1):, :]
    else:
        up = _mm(xn, w_pair, BF16)
        halo = jnp.pad(_pair_cols(hist, f, tf), ((0, 0), (HALO_ROWS - (CONV_W - 1), 0), (0, 0)))
        h = _conv_act(up, halo, cw_pair, cb_pair, b, s, tf)
        tail = up.reshape(b, s, 2 * f)[:, s - (CONV_W - 1):, :].astype(F32)
    return _mm(h, w_down, BF16), _unpair_cols(tail, f, tf)


def kernel(x_prompt, x_sample, cache_k, cache_v, state_pool, state_shift, state_wkv, state_conv, norm_g, attn_w_qkv, attn_w_o, attn_sinks, pool_w, pool_scale, rw_mix, rw_w_r, rw_w_k, rw_w_v, rw_w_o, rw_w0, rw_w1, rw_w2, rw_a0, rw_a1, rw_a2, rw_g1, rw_g2, rw_k_k, rw_k_a, rw_r_k, rw_lnx, ffn_w_up, ffn_conv_w, ffn_conv_b, ffn_w_down):
    bp, sp, d = x_prompt.shape
    bs, ss, _ = x_sample.shape
    depth = norm_g.shape[0]
    window = cache_k.shape[2]
    xp = x_prompt.reshape(bp * sp, d)
    xs = x_sample.reshape(bs * ss, d)
    outs = {n: [] for n in ("kp", "vp", "kn", "vn", "poolp", "pools", "shp", "shs", "wkvp", "wkvs", "convp", "convs")}
    xnp = xns = None
    for i in range(depth):
        kind, j = i % 3, i // 3
        g = norm_g[i]
        g_after = norm_g[i + 1, 0] if i + 1 < depth else None
        if kind == 0:
            wq, wo = attn_w_qkv[j].astype(BF16), attn_w_o[j].astype(BF16)
            mp, kp, vp = _attn_layer(xp, xnp, bp, sp, g[0], wq, wo, attn_sinks[j], jnp.arange(sp))
            ms, kn, vn = _attn_layer(xs, xns, bs, ss, g[0], wq, wo, attn_sinks[j], PAST_LEN + jnp.arange(ss),
                                     cache=(cache_k[j], cache_v[j]))
            outs["kp"].append(kp[:, sp - window:])
            outs["vp"].append(vp[:, sp - window:])
            outs["kn"].append(kn)
            outs["vn"].append(vn)
        elif kind == 1:
            wp = pool_w[j].astype(BF16)
            mp, hp = _pool(xp, xp, g[0], wp, pool_scale[j], bp, sp, halo_is_x=True, pos0=0)
            halo = jnp.pad(state_pool[j], ((0, 0), (HALO_ROWS - POOL_HIST, 0), (0, 0)))
            ms, hs = _pool(xs, halo, g[0], wp, pool_scale[j], bs, ss, halo_is_x=False, pos0=PAST_LEN)
            outs["poolp"].append(hp[:, HALO_ROWS - POOL_HIST:])
            outs["pools"].append(hs[:, HALO_ROWS - POOL_HIST:])
        else:
            p = dict(mix=rw_mix[j], w_r=rw_w_r[j].astype(BF16), w_k=rw_w_k[j].astype(BF16),
                     w_v=rw_w_v[j].astype(BF16), w_o=rw_w_o[j].astype(BF16), w0=rw_w0[j],
                     w1=rw_w1[j].astype(BF16), w2=rw_w2[j].astype(BF16), a0=rw_a0[j],
                     a1=rw_a1[j].astype(BF16), a2=rw_a2[j].astype(BF16), g1=rw_g1[j].astype(BF16),
                     g2=rw_g2[j].astype(BF16), k_k=rw_k_k[j], k_a=rw_k_a[j], r_k=rw_r_k[j], lnx=rw_lnx[j])
            mp, shp, wkvp = _rwkv_layer(xp, bp, sp, g[0], None, None, p)
            ms, shs, wkvs = _rwkv_layer(xs, bs, ss, g[0], state_shift[j], state_wkv[j], p)
            outs["shp"].append(shp)
            outs["shs"].append(shs)
            outs["wkvp"].append(wkvp)
            outs["wkvs"].append(wkvs)
        xp, xnp = _add_norm(xp, mp, g[1], g[2])
        xs, xns = _add_norm(xs, ms, g[1], g[2])
        f = ffn_w_down.shape[1]
        tf = _pick(f, (FFN_COLS, LANES))
        w_pair, w_down = _pair_cols(ffn_w_up[i].astype(BF16), f, tf), ffn_w_down[i].astype(BF16)
        cw_pair = jnp.pad(_pair_cols(ffn_conv_w[i], f, tf), ((0, 8 - CONV_W), (0, 0)))
        cb_pair = _pair_cols(ffn_conv_b[i], f, tf).reshape(1, 2 * f)
        fp, cp = _ffn_layer(xnp, bp, sp, None, w_pair, cw_pair, cb_pair, w_down)
        fs, cs = _ffn_layer(xns, bs, ss, state_conv[i], w_pair, cw_pair, cb_pair, w_down)
        outs["convp"].append(cp)
        outs["convs"].append(cs)
        if g_after is not None and (i + 1) % 3 == 0:
            xp, xnp = _add_norm(xp, fp, g[3], g_after)
            xs, xns = _add_norm(xs, fs, g[3], g_after)
        else:
            xp, xnp = _add_norm(xp, fp, g[3]), None
            xs, xns = _add_norm(xs, fs, g[3]), None
    st = lambda n: jnp.stack(outs[n])
    return (xp.reshape(bp, sp, d), xs.reshape(bs, ss, d),
            st("kp"), st("vp"), st("poolp"), st("shp"), st("wkvp"), st("convp"),
            st("kn"), st("vn"), st("pools"), st("shs"), st("wkvs"), st("convs"))
```

```python
import functools
import math

import jax
import jax.numpy as jnp
from jax import lax
from jax.experimental import pallas as pl
from jax.experimental.pallas import tpu as pltpu

BF16 = jnp.bfloat16
F32 = jnp.float32

CHUNK = 64
HEAD_DIM = 128
N_GROUPS = 4
ROPE_THETA = 10000.0
POOL_WINDOWS = (2, 4, 8, 16)
POOL_HIST = 15
RW_HEAD = 64
LNX_EPS = 64e-5
NORM_EPS = 1e-6
PAST_LEN = 2048
CONV_W = 3
LOG_DECAY_MIN = -math.exp(-0.5)
RW_CHUNK = 64
RW_TILES_PER_STEP = 16
SOLVE_BLOCK = 16
CONV_ACT_BLOCK_BYTES = 2 * 1024 * 1024
FFN_COLS = 256
FFN_ROWS = 512

LANES = 128
VMEM_LIMIT_BYTES = 56 * 1024 * 1024
HALO_ROWS = 16


def _params(*sem):
    return pltpu.CompilerParams(dimension_semantics=sem, vmem_limit_bytes=VMEM_LIMIT_BYTES)


def _pick(n, prefs):
    for p in prefs:
        if n % p == 0:
            return p
    return n


def _rms(xf, g):
    return xf * lax.rsqrt(jnp.mean(xf * xf, axis=-1, keepdims=True) + NORM_EPS) * g


def _seg_ones(rows):
    r = lax.broadcasted_iota(jnp.int32, (rows, LANES), 0)
    c = lax.broadcasted_iota(jnp.int32, (rows, LANES), 1)
    return ((r % LANES) // RW_HEAD == c // RW_HEAD).astype(BF16)


def _seg_sum(p, jj):
    hi = p.astype(BF16)
    lo = (p - hi.astype(F32)).astype(BF16)
    return jnp.dot(jnp.concatenate([hi, lo], axis=1), jj, preferred_element_type=F32)


def _mm_kernel(*refs, norm, rope_blocks, act):
    it = iter(refs)
    x_ref = next(it)
    g_ref = next(it) if norm else None
    w_ref = next(it)
    cos_ref = next(it) if rope_blocks else None
    sin_ref = next(it) if rope_blocks else None
    o_ref = next(it)
    xn_ref = next(it) if norm else None
    j = pl.program_id(1)

    if norm:
        @pl.when(j == 0)
        def _():
            xn_ref[...] = _rms(x_ref[...].astype(F32), g_ref[...]).astype(BF16)
        lhs = xn_ref[...]
    else:
        lhs = x_ref[...]
    acc = jnp.dot(lhs, w_ref[...], preferred_element_type=F32)
    if act == "tanh":
        acc = jnp.tanh(acc)
    elif act == "sigmoid":
        acc = jax.nn.sigmoid(acc)

    if rope_blocks:
        @pl.when(j < rope_blocks)
        def _():
            cos = cos_ref[...]
            sin = sin_ref[...]
            for h in range(acc.shape[1] // HEAD_DIM):
                xh = acc[:, h * HEAD_DIM:(h + 1) * HEAD_DIM]
                o_ref[:, h * HEAD_DIM:(h + 1) * HEAD_DIM] = (
                    xh * cos + pltpu.roll(xh, HEAD_DIM // 2, 1) * sin).astype(o_ref.dtype)

        @pl.when(j >= rope_blocks)
        def _():
            o_ref[...] = acc.astype(o_ref.dtype)
    else:
        o_ref[...] = acc.astype(o_ref.dtype)


def _mm(x, w, out_dtype, *, g=None, rope=None, act=None):
    w, layer = w
    m, k = x.shape
    n = w.shape[2]
    norm = g is not None
    if norm or k > 8192:
        tm = min(512, m)
    else:
        tm = min(1024, m)
    tn = _pick(math.gcd(n, rope[2]) if rope else n, (256,) if k > 8192 else (512, 256, 128))
    rope_blocks = 0
    in_specs = [pl.BlockSpec((tm, k), lambda i, j: (i, 0))]
    args = [x]
    if norm:
        in_specs.append(pl.BlockSpec((1, k), lambda i, j: (0, 0)))
        args.append(g.reshape(1, k).astype(F32))
    in_specs.append(pl.BlockSpec((None, k, tn), lambda i, j: (layer, 0, j)))
    args.append(w)
    if rope is not None:
        cos, sin, n_cols = rope
        assert n_cols % tn == 0 and tn % HEAD_DIM == 0
        rope_blocks = n_cols // tn
        in_specs += [pl.BlockSpec((tm, HEAD_DIM), lambda i, j: (i, 0))] * 2
        args += [cos, sin]
    scratch = [pltpu.VMEM((tm, k), BF16)] if norm else []
    return pl.pallas_call(
        functools.partial(_mm_kernel, norm=norm, rope_blocks=rope_blocks, act=act),
        out_shape=jax.ShapeDtypeStruct((m, n), out_dtype),
        grid=(m // tm, n // tn),
        in_specs=in_specs,
        out_specs=pl.BlockSpec((tm, tn), lambda i, j: (i, j)),
        scratch_shapes=scratch,
        compiler_params=_params("arbitrary", "arbitrary"),
        name="mm",
    )(*args)


def _add_norm_kernel(x_ref, m_ref, g_ref, *refs):
    x = x_ref[...] + _rms(m_ref[...].astype(F32), g_ref[...])
    if len(refs) == 1:
        refs[0][...] = x
    else:
        gn_ref, o_ref, on_ref = refs
        o_ref[...] = x
        on_ref[...] = _rms(x, gn_ref[...]).astype(on_ref.dtype)


def _add_norm(x, mix, g, g_next=None):
    m, d = x.shape
    tr = min(256, m)
    row = pl.BlockSpec((tr, d), lambda i: (i, 0))
    vec = pl.BlockSpec((1, d), lambda i: (0, 0))
    with_next = g_next is not None
    return pl.pallas_call(
        _add_norm_kernel,
        out_shape=((jax.ShapeDtypeStruct((m, d), F32), jax.ShapeDtypeStruct((m, d), BF16)) if with_next
                   else jax.ShapeDtypeStruct((m, d), F32)),
        grid=(m // tr,),
        in_specs=[row, row, vec] + ([vec] if with_next else []),
        out_specs=(row, row) if with_next else row,
        compiler_params=_params("arbitrary"),
        name="add_norm",
    )(x, mix, g.reshape(1, d), *([g_next.reshape(1, d)] if with_next else []))


def _attn_kernel(sink_ref, q_ref, *refs, n_pieces, n_kv, band_mask):
    k_refs = refs[:n_pieces]
    v_refs = refs[n_pieces:2 * n_pieces]
    o_ref = refs[2 * n_pieces]
    tq = q_ref.shape[0]
    c = pl.program_id(1)
    scale = HEAD_DIM ** -0.5
    scores = []
    for kv in range(n_kv):
        lo, hi = kv * HEAD_DIM, (kv + 1) * HEAD_DIM
        q4 = jnp.concatenate(
            [q_ref[:, (kv * N_GROUPS + r) * HEAD_DIM:(kv * N_GROUPS + r + 1) * HEAD_DIM] for r in range(N_GROUPS)],
            axis=0)
        kb = jnp.concatenate([kr[:, lo:hi].astype(BF16) for kr in k_refs], axis=0)
        scores.append(lax.dot_general(q4, kb, (((1,), (1,)), ((), ())), preferred_element_type=F32))
    s = jnp.concatenate(scores, axis=0) * scale
    if band_mask:
        col = lax.broadcasted_iota(jnp.int32, s.shape, 1)
        s = jnp.where(col >= CHUNK * (n_pieces - 1 - c), s, -1e30)
    sink = jnp.concatenate([jnp.full((tq, 1), sink_ref[h], F32) for h in range(n_kv * N_GROUPS)], axis=0)
    mx = jnp.maximum(jnp.max(s, axis=-1, keepdims=True), sink)
    p = jnp.exp(s - mx)
    denom = jnp.sum(p, axis=-1, keepdims=True) + jnp.exp(sink - mx)
    pb = p.astype(BF16)
    inv = 1.0 / denom
    rows = N_GROUPS * tq
    for kv in range(n_kv):
        vb = jnp.concatenate([vr[:, kv * HEAD_DIM:(kv + 1) * HEAD_DIM].astype(BF16) for vr in v_refs], axis=0)
        o = jnp.dot(pb[kv * rows:(kv + 1) * rows], vb, preferred_element_type=F32) * inv[kv * rows:(kv + 1) * rows]
        for r in range(N_GROUPS):
            h = kv * N_GROUPS + r
            o_ref[:, h * HEAD_DIM:(h + 1) * HEAD_DIM] = o[r * tq:(r + 1) * tq].astype(o_ref.dtype)


def _attn_prompt(qkv, sinks, b, s, d):
    kv_dim = d // N_GROUPS
    n_kv = kv_dim // HEAD_DIM
    nc = s // CHUNK
    n_band = 3
    kcol, vcol = d // kv_dim, d // kv_dim + 1

    def piece(jj, col):
        return pl.BlockSpec((CHUNK, kv_dim),
                            lambda bi, c, sk: (bi * nc + jnp.maximum(c - (n_band - 1) + jj, 0), col))

    in_specs = [pl.BlockSpec((CHUNK, d), lambda bi, c, sk: (bi * nc + c, 0))]
    in_specs += [piece(jj, kcol) for jj in range(n_band)] + [piece(jj, vcol) for jj in range(n_band)]
    return pl.pallas_call(
        functools.partial(_attn_kernel, n_pieces=n_band, n_kv=n_kv, band_mask=True),
        out_shape=jax.ShapeDtypeStruct((b * s, d), BF16),
        grid_spec=pltpu.PrefetchScalarGridSpec(
            num_scalar_prefetch=1, grid=(b, nc), in_specs=in_specs,
            out_specs=pl.BlockSpec((CHUNK, d), lambda bi, c, sk: (bi * nc + c, 0))),
        compiler_params=_params("arbitrary", "arbitrary"),
        name="attn_prompt",
    )(sinks, *([qkv] * (1 + 2 * n_band)))


def _attn_sample(qkv, cache_k, cache_v, sinks, b, t, d):
    kv_dim = d // N_GROUPS
    n_kv = kv_dim // HEAD_DIM
    win = cache_k.shape[1]
    kcol, vcol = d // kv_dim, d // kv_dim + 1
    in_specs = [
        pl.BlockSpec((t, d), lambda bi, c, sk: (bi, 0)),
        pl.BlockSpec((None, win, kv_dim), lambda bi, c, sk: (bi, 0, 0)),
        pl.BlockSpec((t, kv_dim), lambda bi, c, sk: (bi, kcol)),
        pl.BlockSpec((None, win, kv_dim), lambda bi, c, sk: (bi, 0, 0)),
        pl.BlockSpec((t, kv_dim), lambda bi, c, sk: (bi, vcol)),
    ]
    return pl.pallas_call(
        functools.partial(_attn_kernel, n_pieces=2, n_kv=n_kv, band_mask=False),
        out_shape=jax.ShapeDtypeStruct((b * t, d), BF16),
        grid_spec=pltpu.PrefetchScalarGridSpec(
            num_scalar_prefetch=1, grid=(b, 1), in_specs=in_specs,
            out_specs=pl.BlockSpec((t, d), lambda bi, c, sk: (bi, 0))),
        compiler_params=_params("arbitrary", "arbitrary"),
        name="attn_sample",
    )(sinks, qkv, cache_k, qkv, cache_v, qkv)


def _pool_kernel(x_ref, halo_ref, g_ref, w_ref, sc_ref, o_ref, st_ref, *, halo_is_x, pos0):
    si = pl.program_id(1)
    ts, d = x_ref.shape
    gw = d // len(POOL_WINDOWS)
    g = g_ref[...]
    un = _rms(x_ref[...], g)
    if halo_is_x:
        halo = jnp.where(si > 0, _rms(halo_ref[...], g), 0.0)
    else:
        halo = halo_ref[...]
    full = jnp.concatenate([halo, un], axis=0)
    pos = pos0 + si * ts + lax.broadcasted_iota(jnp.int32, (ts, 1), 0)
    for gi, win in enumerate(POOL_WINDOWS):
        lo, hi = gi * gw, (gi + 1) * gw
        acc = full[:, lo:hi]
        span = 1
        while span < win:
            acc = acc + pltpu.roll(acc, span, 0)
            span *= 2
        cnt = jnp.minimum(win, pos + 1).astype(F32)
        dlt = (acc[HALO_ROWS:] / cnt - un[:, lo:hi]).astype(BF16)
        y = jnp.dot(dlt, w_ref[gi], preferred_element_type=F32)
        o_ref[:, lo:hi] = (y * sc_ref[:, lo:hi]).astype(o_ref.dtype)
    st_ref[...] = full[ts:]


def _pool(x, halo, g, w, scale, b, s, *, halo_is_x, pos0):
    d = x.shape[1]
    ts = min(256, s)
    ns = s // ts
    hb = ts // HALO_ROWS
    if halo_is_x:
        halo_spec = pl.BlockSpec((HALO_ROWS, d), lambda bi, si: (jnp.maximum((bi * ns + si) * hb - 1, 0), 0))
    else:
        halo_spec = pl.BlockSpec((None, HALO_ROWS, d), lambda bi, si: (bi, 0, 0))
    ng, gw = w.shape[0], w.shape[1]
    return pl.pallas_call(
        functools.partial(_pool_kernel, halo_is_x=halo_is_x, pos0=pos0),
        out_shape=(jax.ShapeDtypeStruct((b * s, d), BF16), jax.ShapeDtypeStruct((b, HALO_ROWS, d), F32)),
        grid=(b, ns),
        in_specs=[pl.BlockSpec((ts, d), lambda bi, si: (bi * ns + si, 0)),
                  halo_spec,
                  pl.BlockSpec((1, d), lambda bi, si: (0, 0)),
                  pl.BlockSpec((ng, gw, gw), lambda bi, si: (0, 0, 0)),
                  pl.BlockSpec((1, d), lambda bi, si: (0, 0))],
        out_specs=(pl.BlockSpec((ts, d), lambda bi, si: (bi * ns + si, 0)),
                   pl.BlockSpec((None, HALO_ROWS, d), lambda bi, si: (bi, 0, 0))),
        compiler_params=_params("arbitrary", "arbitrary"),
        name="pool",
    )(x, halo, g.reshape(1, d), w, scale.reshape(1, d))


def _rw_mix_kernel(x_ref, halo_ref, g_ref, mix_ref, *o_refs, halo_is_x):
    si = pl.program_id(1)
    g = g_ref[...]
    un = _rms(x_ref[...], g)
    n_h = halo_ref.shape[0]
    if halo_is_x:
        prev_row = jnp.where(si > 0, _rms(halo_ref[n_h - 1:n_h, :], g), 0.0)
    else:
        prev_row = halo_ref[n_h - 1:n_h, :]
    row = lax.broadcasted_iota(jnp.int32, (un.shape[0], 1), 0)
    prev = jnp.where(row == 0, prev_row, pltpu.roll(un, 1, 0))
    xx = prev - un
    for jm in range(6):
        o_refs[jm][...] = (un + xx * mix_ref[jm:jm + 1, :]).astype(BF16)
    st_ref = o_refs[6]
    st_ref[...] = un[un.shape[0] - st_ref.shape[0]:]


def _rw_mix(x, halo, g, mix, b, s, *, halo_is_x):
    d = x.shape[1]
    ts = min(256, s)
    ns = s // ts
    hr = 8
    if halo_is_x:
        halo_spec = pl.BlockSpec((hr, d), lambda bi, si: (jnp.maximum((bi * ns + si) * (ts // hr) - 1, 0), 0))
    else:
        halo_spec = pl.BlockSpec((None, hr, d), lambda bi, si: (bi, 0, 0))
    row_spec = pl.BlockSpec((ts, d), lambda bi, si: (bi * ns + si, 0))
    return pl.pallas_call(
        functools.partial(_rw_mix_kernel, halo_is_x=halo_is_x),
        out_shape=tuple([jax.ShapeDtypeStruct((b * s, d), BF16)] * 6 + [jax.ShapeDtypeStruct((b, hr, d), F32)]),
        grid=(b, ns),
        in_specs=[row_spec, halo_spec,
                  pl.BlockSpec((1, d), lambda bi, si: (0, 0)),
                  pl.BlockSpec((8, d), lambda bi, si: (0, 0))],
        out_specs=tuple([row_spec] * 6 + [pl.BlockSpec((None, hr, d), lambda bi, si: (bi, 0, 0))]),
        compiler_params=_params("arbitrary", "arbitrary"),
        name="rw_mix",
    )(x, halo, g.reshape(1, d), jnp.pad(mix, ((0, 2), (0, 0))))


def _dot_nt(a, b):
    return lax.dot_general(a, b, (((1,), (1,)), ((), ())), preferred_element_type=F32)


def _rw_chunk(r, ld, k, v, kk, kka, zs):
    c, width = r.shape
    n = RW_HEAD
    tiles = [slice(g * LANES, (g + 1) * LANES) for g in range(width // LANES)]
    lanes = lambda parts: jnp.concatenate(parts, axis=1) if len(parts) > 1 else parts[0]
    row = lax.broadcasted_iota(jnp.int32, (c, 2 * c), 0)
    col = lax.broadcasted_iota(jnp.int32, (c, 2 * c), 1)
    s_idx = col % c
    first = col < c
    strict = s_idx < row
    incl = s_idx <= row
    head0 = lax.broadcasted_iota(jnp.int32, (c, width), 1) % LANES < n

    tri = (lax.broadcasted_iota(jnp.int32, (c, 3 * c), 1) % c <= lax.broadcasted_iota(jnp.int32, (c, 3 * c), 0))
    hi = ld.astype(BF16)
    mid = (ld - hi.astype(F32)).astype(BF16)
    lo = (ld - hi.astype(F32) - mid.astype(F32)).astype(BF16)
    cum = jnp.dot(tri.astype(BF16), jnp.concatenate([hi, mid, lo], axis=0), preferred_element_type=F32)

    e_cum = jnp.exp(cum)
    e_inv = jnp.exp(-cum)
    at = -kk * jnp.exp(cum - ld)
    rt = r * e_cum
    bt = kka * e_inv
    kt = k * e_inv
    w_last = e_cum[c - 1:c, :]

    zero = jnp.zeros_like(at)
    a0r0 = jnp.concatenate([jnp.where(head0, at, zero), jnp.where(head0, rt, zero)], axis=0).astype(BF16)
    a1 = jnp.where(head0, zero, at).astype(BF16)
    r1 = jnp.where(head0, zero, rt).astype(BF16)
    ar = jnp.concatenate([at, rt], axis=0).astype(BF16)
    bk = jnp.concatenate([bt, kt], axis=0).astype(BF16)
    kb = jnp.concatenate([kt, bt], axis=0).astype(BF16)
    vb = v.astype(BF16)
    vv = jnp.concatenate([vb, vb], axis=0)

    l2, g0r, g1r, mv, p = [], [], [], [], []
    for t, z in zip(tiles, zs):
        g0 = _dot_nt(a0r0[:, t], bk[:, t])
        g1a = _dot_nt(a1[:, t], kb[:, t])
        g1r.append(_dot_nt(r1[:, t], bk[:, t]))
        g0a = g0[:c]
        g0r.append(g0[c:])
        l2.append(jnp.where(strict, jnp.where(first, g0a, g1a), 0.0))
        mak = jnp.concatenate([jnp.where(strict & ~first, g0a, 0.0), jnp.where(strict & first, g1a, 0.0)], axis=0)
        mv.append(jnp.dot(mak.astype(BF16), vv[:, t], preferred_element_type=F32))
        p.append(jnp.dot(ar[:, t], z.astype(BF16), preferred_element_type=F32))
    mv, p = lanes(mv), lanes(p)
    u0 = p[:c] + jnp.where(head0, mv[:c], mv[c:])

    sb = SOLVE_BLOCK
    nb = c // sb
    spread = (lax.broadcasted_iota(jnp.int32, (2 * c, LANES), 0) // c
              == lax.broadcasted_iota(jnp.int32, (2 * c, LANES), 1) // n).astype(BF16)
    lane_b = lax.broadcasted_iota(jnp.int32, (sb, 2 * c), 1) % c
    first_b = lax.broadcasted_iota(jnp.int32, (sb, 2 * c), 1) < c
    head0_b = lax.broadcasted_iota(jnp.int32, (sb, width), 1) % LANES < n
    lcol = lanes([
        jnp.dot(jnp.concatenate([jnp.where(lane_b == blk * sb + s, l2g[blk * sb:(blk + 1) * sb], 0.0).astype(BF16)
                                 for blk in range(nb) for s in range(sb - 1)], axis=0),
                spread, preferred_element_type=F32) for l2g in l2])
    solved = []
    for blk in range(nb):
        ub = u0[blk * sb:(blk + 1) * sb]
        if blk > 0:
            done = jnp.concatenate(solved + [jnp.zeros((c - blk * sb, width), F32)], axis=0).astype(BF16)
            done = jnp.concatenate([done, done], axis=0)
            off = []
            for t, l2g in zip(tiles, l2):
                band = l2g[blk * sb:(blk + 1) * sb]
                lhs = jnp.concatenate([jnp.where(first_b, band, 0.0), jnp.where(first_b, 0.0, band)], axis=0)
                off.append(jnp.dot(lhs.astype(BF16), done[:, t], preferred_element_type=F32))
            off = lanes(off)
            ub = ub + jnp.where(head0_b, off[:sb], off[sb:])
        for s in range(sb - 1):
            i = blk * (sb - 1) + s
            ub = ub + lcol[i * sb:(i + 1) * sb] * ub[s:s + 1, :]
        solved.append(ub)
    u = jnp.concatenate(solved, axis=0)

    uv = jnp.concatenate([u, v], axis=0).astype(BF16)
    bkw = jnp.concatenate([bt * w_last, kt * w_last], axis=0)
    same_head = (lax.broadcasted_iota(jnp.int32, (LANES, LANES), 0) // n
                 == lax.broadcasted_iota(jnp.int32, (LANES, LANES), 1) // n)
    ny, z_new = [], []
    for t, z, g0rg, g1rg in zip(tiles, zs, g0r, g1r):
        nmat = jnp.concatenate([jnp.where(incl, g0rg, 0.0), jnp.where(incl, g1rg, 0.0)], axis=0).astype(BF16)
        ny.append(jnp.dot(nmat, uv[:, t], preferred_element_type=F32))
        upd_z = jnp.dot(bkw[:, t].T.astype(BF16), uv[:, t], preferred_element_type=F32)
        w_col = jnp.broadcast_to(w_last[:, t], (LANES, LANES)).T
        z_new.append(z * w_col + jnp.where(same_head, upd_z, 0.0))
    ny = lanes(ny)
    return p[c:] + jnp.where(head0, ny[:c], ny[c:]), z_new


def _seg_sum_wide(p, jj):
    parts = [_seg_sum(p[:, g * LANES:(g + 1) * LANES], jj) for g in range(p.shape[1] // LANES)]
    return jnp.concatenate(parts, axis=1) if len(parts) > 1 else parts[0]


_PV_W0, _PV_A0, _PV_KK, _PV_KA, _PV_RK, _PV_LN_G, _PV_LN_B = range(7)


def _rw_scan_kernel(r_ref, k_ref, v_ref, lw_ref, la_ref, g_ref, pv_ref, s0_ref, o_ref, sT_ref, z_ref):
    @pl.when(pl.program_id(2) == 0)
    def _():
        z_ref[...] = s0_ref[...]

    pv = lambda i: pv_ref[i:i + 1, :]
    jj = _seg_ones(2 * LANES)
    r = r_ref[...].astype(F32)
    k = k_ref[...].astype(F32)
    v = v_ref[...].astype(F32)
    a = jax.nn.sigmoid(pv(_PV_A0) + la_ref[...])
    ld = LOG_DECAY_MIN * jax.nn.sigmoid(pv(_PV_W0) + lw_ref[...])
    k2 = k * (1.0 + (a - 1.0) * pv(_PV_KA))
    kkr = k * pv(_PV_KK)
    kk = kkr / jnp.maximum(jnp.sqrt(_seg_sum_wide(kkr * kkr, jj)), 1e-12)

    tg = z_ref.shape[0]
    y, zs = _rw_chunk(r, ld, k2, v, kk, kk * a, [z_ref[g] for g in range(tg)])
    for g in range(tg):
        z_ref[g] = zs[g]
    sT_ref[...] = z_ref[...]

    yc = y - _seg_sum_wide(y, jj) * (1.0 / RW_HEAD)
    var = _seg_sum_wide(yc * yc, jj) * (1.0 / RW_HEAD)
    o = yc * lax.rsqrt(var + LNX_EPS) * pv(_PV_LN_G) + pv(_PV_LN_B)
    bonus = _seg_sum_wide(r * k2 * pv(_PV_RK), jj) * v
    o_ref[...] = ((o + bonus) * g_ref[...].astype(F32)).astype(o_ref.dtype)


def _rw_scan(r, k, v, lw, la, gate, pvec, s0, b, s):
    d = r.shape[1]
    ntile = d // LANES
    tg = min(RW_TILES_PER_STEP, ntile)
    tc = min(RW_CHUNK, s)
    nch = s // tc
    seq = pl.BlockSpec((tc, tg * LANES), lambda bi, gi, ci: (bi * nch + ci, gi))
    st = pl.BlockSpec((None, tg, LANES, LANES), lambda bi, gi, ci: (bi, gi, 0, 0))
    return pl.pallas_call(
        _rw_scan_kernel,
        out_shape=(jax.ShapeDtypeStruct((b * s, d), BF16), jax.ShapeDtypeStruct(s0.shape, F32)),
        grid=(b, ntile // tg, nch),
        in_specs=[seq] * 6 + [pl.BlockSpec((8, tg * LANES), lambda bi, gi, ci: (0, gi)), st],
        out_specs=(seq, st),
        scratch_shapes=[pltpu.VMEM((tg, LANES, LANES), F32)],
        compiler_params=_params("arbitrary", "arbitrary", "arbitrary"),
        name="rw_scan",
    )(r, k, v, lw, la, gate, pvec, s0)


def _gelu_tanh(x):
    return 0.5 * x * (1.0 + jnp.tanh(math.sqrt(2.0 / math.pi) * (x + 0.044715 * (x * x * x))))


def _conv_taps(full, cw, cb):
    return cb + full * cw[2:3] + pltpu.roll(full, 1, 0) * cw[1:2] + pltpu.roll(full, 2, 0) * cw[0:1]


def _conv_act_kernel(u_ref, h_ref, cw_ref, cb_ref, o_ref, *, tf):
    f = o_ref.shape[1]
    for j in range(f // tf):
        def conv(cols):
            full = jnp.concatenate([h_ref[:, cols], u_ref[:, cols].astype(F32)], axis=0)
            return _conv_taps(full, cw_ref[:, cols], cb_ref[:, cols])[HALO_ROWS:]
        gate = conv(slice(j * tf, (j + 1) * tf))
        val = conv(slice(f + j * tf, f + (j + 1) * tf))
        o_ref[:, j * tf:(j + 1) * tf] = (_gelu_tanh(gate) * val).astype(o_ref.dtype)


def _conv_act(up, halo, cw, cb, layer, b, s, tf):
    f = up.shape[1] // 2
    return pl.pallas_call(
        functools.partial(_conv_act_kernel, tf=tf),
        out_shape=jax.ShapeDtypeStruct((b * s, f), BF16),
        grid=(b,),
        in_specs=[pl.BlockSpec((s, 2 * f), lambda bi: (bi, 0)),
                  pl.BlockSpec((None, HALO_ROWS, 2 * f), lambda bi: (bi, 0, 0)),
                  pl.BlockSpec((None, 8, 2 * f), lambda bi: (layer, 0, 0)),
                  pl.BlockSpec((None, 1, 2 * f), lambda bi: (layer, 0, 0))],
        out_specs=pl.BlockSpec((s, f), lambda bi: (bi, 0)),
        compiler_params=_params("arbitrary"),
        name="conv_act",
    )(up, halo, cw, cb)


def _ffn_up_kernel(x_ref, wg_ref, wv_ref, cwg_ref, cwv_ref, cbg_ref, cbv_ref, h_ref, tg_ref, tv_ref, up_ref, *, rows):
    s = x_ref.shape[0]
    tf = h_ref.shape[1]
    n_pass = s // rows

    def project(k):
        xk = x_ref[k * rows:(k + 1) * rows, :]
        up_ref[k % 2, 0, 8:, :] = jnp.dot(xk, wg_ref[...], preferred_element_type=F32)
        up_ref[k % 2, 1, 8:, :] = jnp.dot(xk, wv_ref[...], preferred_element_type=F32)

    def conv(slot, part, cw_ref, cb_ref):
        buf = up_ref.at[slot, part]
        return (cb_ref[...] + buf[8:8 + rows, :] * cw_ref[2:3, :] + buf[7:7 + rows, :] * cw_ref[1:2, :]
                + buf[6:6 + rows, :] * cw_ref[0:1, :])

    zero = jnp.zeros((8, tf), F32)
    up_ref[0, 0, 0:8, :] = zero
    up_ref[0, 1, 0:8, :] = zero
    project(0)
    for k in range(n_pass):
        slot = k % 2
        if k + 1 < n_pass:
            project(k + 1)
            up_ref[1 - slot, 0, 0:8, :] = up_ref[slot, 0, rows:rows + 8, :]
            up_ref[1 - slot, 1, 0:8, :] = up_ref[slot, 1, rows:rows + 8, :]
        gate = conv(slot, 0, cwg_ref, cbg_ref)
        val = conv(slot, 1, cwv_ref, cbv_ref)
        h_ref[k * rows:(k + 1) * rows, :] = (_gelu_tanh(gate) * val).astype(h_ref.dtype)
    last = (n_pass - 1) % 2
    tg_ref[...] = up_ref[last, 0, rows:rows + 8, :]
    tv_ref[...] = up_ref[last, 1, rows:rows + 8, :]


def _ffn_up(xn, w, cw, cb, layer, b, s):
    d = xn.shape[1]
    f = w.shape[2] // 2
    tf = _pick(f, (FFN_COLS, LANES))
    nf = f // tf
    rows = min(FFN_ROWS, s)

    def cols(block_rows, off):
        return pl.BlockSpec((None, block_rows, tf), lambda bi, j: (layer, 0, off + j))

    tail = pl.BlockSpec((None, 8, tf), lambda bi, j: (bi, 0, j))
    h, tail_g, tail_v = pl.pallas_call(
        functools.partial(_ffn_up_kernel, rows=rows),
        out_shape=(jax.ShapeDtypeStruct((b * s, f), BF16), jax.ShapeDtypeStruct((b, 8, f), F32),
                   jax.ShapeDtypeStruct((b, 8, f), F32)),
        grid=(b, nf),
        in_specs=[pl.BlockSpec((s, d), lambda bi, j: (bi, 0)),
                  cols(d, 0), cols(d, nf), cols(8, 0), cols(8, nf), cols(1, 0), cols(1, nf)],
        out_specs=(pl.BlockSpec((s, tf), lambda bi, j: (bi, j)), tail, tail),
        scratch_shapes=[pltpu.VMEM((2, 2, 8 + rows, tf), F32)],
        compiler_params=_params("arbitrary", "arbitrary"),
        name="ffn_up",
    )(xn, w, w, cw, cw, cb, cb)
    return h, jnp.concatenate([tail_g, tail_v], axis=2)


def _rope_tables(pos):
    half = HEAD_DIM // 2
    inv = ROPE_THETA ** (-jnp.arange(half, dtype=F32) / half)
    ang = pos.astype(F32)[:, None] * inv[None, :]
    cos, sin = jnp.cos(ang), jnp.sin(ang)
    return jnp.concatenate([cos, cos], axis=1), jnp.concatenate([-sin, sin], axis=1)


def _state_to_tiles(wkv):
    b, h, n, _ = wkv.shape
    t = wkv.reshape(b, h // 2, 2, n, n).transpose(0, 1, 2, 4, 3)
    z = t[:, :, :, :, None, :] * jnp.eye(2, dtype=wkv.dtype)[None, None, :, None, :, None]
    return z.reshape(b, h // 2, 2 * n, 2 * n)


def _tiles_to_state(z, h):
    b = z.shape[0]
    n = RW_HEAD
    z = z.reshape(b, h // 2, 2, n, 2, n)
    t = jnp.stack([z[:, :, 0, :, 0, :], z[:, :, 1, :, 1, :]], axis=2)
    return t.transpose(0, 1, 2, 4, 3).reshape(b, h, n, n)


def _attn_layer(x, xn, b, s, g, w_qkv, w_o, sinks, pos, cache=None):
    d = x.shape[1]
    kv_dim = d // N_GROUPS
    cos, sin = _rope_tables(pos)
    cos, sin = jnp.tile(cos, (b, 1)), jnp.tile(sin, (b, 1))
    if xn is None:
        qkv = _mm(x, w_qkv, BF16, g=g, rope=(cos, sin, d + kv_dim))
    else:
        qkv = _mm(xn, w_qkv, BF16, rope=(cos, sin, d + kv_dim))
    if cache is None:
        o = _attn_prompt(qkv, sinks, b, s, d)
    else:
        o = _attn_sample(qkv, cache[0].reshape(b, -1, kv_dim), cache[1].reshape(b, -1, kv_dim), sinks, b, s, d)
    k3 = qkv[:, d:d + kv_dim].reshape(b, s, kv_dim // HEAD_DIM, HEAD_DIM).astype(F32)
    v3 = qkv[:, d + kv_dim:].reshape(b, s, kv_dim // HEAD_DIM, HEAD_DIM).astype(F32)
    return _mm(o, w_o, BF16), k3, v3


def _rwkv_layer(x, b, s, g, shift, wkv, p):
    d = x.shape[1]
    heads = d // RW_HEAD
    if shift is None:
        mixed = _rw_mix(x, x, g, p["mix"], b, s, halo_is_x=True)
        s0 = jnp.zeros((b, heads // 2, 2 * RW_HEAD, 2 * RW_HEAD), F32)
    else:
        halo = jnp.pad(shift.reshape(b, 1, d), ((0, 0), (7, 0), (0, 0)))
        mixed = _rw_mix(x, halo, g, p["mix"], b, s, halo_is_x=False)
        s0 = _state_to_tiles(wkv.astype(F32))
    xr, xw, xk, xv, xa, xg, tail = mixed
    r = _mm(xr, p["w_r"], BF16)
    k = _mm(xk, p["w_k"], BF16)
    v = _mm(xv, p["w_v"], BF16)
    lw = _mm(_mm(xw, p["w1"], BF16, act="tanh"), p["w2"], F32)
    la = _mm(_mm(xa, p["a1"], BF16), p["a2"], F32)
    gate = _mm(_mm(xg, p["g1"], BF16, act="sigmoid"), p["g2"], BF16)
    pvec = jnp.stack([p["w0"], p["a0"], p["k_k"], p["k_a"], p["r_k"].reshape(-1), p["lnx"][0], p["lnx"][1],
                      jnp.zeros_like(p["w0"])])
    o, s_t = _rw_scan(r, k, v, lw, la, gate, pvec, s0, b, s)
    return _mm(o, p["w_o"], BF16), tail[:, 7:8, :], _tiles_to_state(s_t, heads)


def _ffn_layer(xn, b, s, hist, w_up, cw, cb, w_down, layer):
    f = w_down.shape[1]
    if hist is None:
        h, tail = _ffn_up(xn, w_up, cw, cb, layer, b, s)
        tail = tail[:, 8 - (CONV_W - 1):, :]
    else:
        up = _mm(xn, (w_up, layer), BF16)
        halo = jnp.pad(hist, ((0, 0), (HALO_ROWS - (CONV_W - 1), 0), (0, 0)))
        h = _conv_act(up, halo, cw, cb, layer, b, s, _pick(f, (FFN_COLS, LANES)))
        tail = up.reshape(b, s, 2 * f)[:, s - (CONV_W - 1):, :].astype(F32)
    return _mm(h, (w_down, layer), BF16), tail


def kernel(x_prompt, x_sample, cache_k, cache_v, state_pool, state_shift, state_wkv, state_conv, norm_g, attn_w_qkv, attn_w_o, attn_sinks, pool_w, pool_scale, rw_mix, rw_w_r, rw_w_k, rw_w_v, rw_w_o, rw_w0, rw_w1, rw_w2, rw_a0, rw_a1, rw_a2, rw_g1, rw_g2, rw_k_k, rw_k_a, rw_r_k, rw_lnx, ffn_w_up, ffn_conv_w, ffn_conv_b, ffn_w_down):
    bp, sp, d = x_prompt.shape
    bs, ss, _ = x_sample.shape
    depth = norm_g.shape[0]
    window = cache_k.shape[2]
    xp = x_prompt.reshape(bp * sp, d)
    xs = x_sample.reshape(bs * ss, d)
    outs = {n: [] for n in ("kp", "vp", "kn", "vn", "poolp", "pools", "shp", "shs", "wkvp", "wkvs", "convp", "convs")}
    xnp = xns = None
    bf = lambda a: a.astype(BF16)
    attn_w_qkv, attn_w_o, ffn_w_up, ffn_w_down = bf(attn_w_qkv), bf(attn_w_o), bf(ffn_w_up), bf(ffn_w_down)
    rw_w = {n: bf(a) for n, a in dict(w_r=rw_w_r, w_k=rw_w_k, w_v=rw_w_v, w_o=rw_w_o, w1=rw_w1, w2=rw_w2,
                                      a1=rw_a1, a2=rw_a2, g1=rw_g1, g2=rw_g2).items()}
    ffn_cw = jnp.pad(ffn_conv_w, ((0, 0), (0, 8 - CONV_W), (0, 0)))
    ffn_cb = ffn_conv_b[:, None, :]
    for i in range(depth):
        kind, j = i % 3, i // 3
        g = norm_g[i]
        g_after = norm_g[i + 1, 0] if i + 1 < depth else None
        if kind == 0:
            wq, wo = (attn_w_qkv, j), (attn_w_o, j)
            mp, kp, vp = _attn_layer(xp, xnp, bp, sp, g[0], wq, wo, attn_sinks[j], jnp.arange(sp))
            ms, kn, vn = _attn_layer(xs, xns, bs, ss, g[0], wq, wo, attn_sinks[j], PAST_LEN + jnp.arange(ss),
                                     cache=(cache_k[j], cache_v[j]))
            outs["kp"].append(kp[:, sp - window:])
            outs["vp"].append(vp[:, sp - window:])
            outs["kn"].append(kn)
            outs["vn"].append(vn)
        elif kind == 1:
            wp = pool_w[j].astype(BF16)
            mp, hp = _pool(xp, xp, g[0], wp, pool_scale[j], bp, sp, halo_is_x=True, pos0=0)
            halo = jnp.pad(state_pool[j], ((0, 0), (HALO_ROWS - POOL_HIST, 0), (0, 0)))
            ms, hs = _pool(xs, halo, g[0], wp, pool_scale[j], bs, ss, halo_is_x=False, pos0=PAST_LEN)
            outs["poolp"].append(hp[:, HALO_ROWS - POOL_HIST:])
            outs["pools"].append(hs[:, HALO_ROWS - POOL_HIST:])
        else:
            p = dict(mix=rw_mix[j], w0=rw_w0[j], a0=rw_a0[j], k_k=rw_k_k[j], k_a=rw_k_a[j], r_k=rw_r_k[j],
                     lnx=rw_lnx[j], **{n: (a, j) for n, a in rw_w.items()})
            mp, shp, wkvp = _rwkv_layer(xp, bp, sp, g[0], None, None, p)
            ms, shs, wkvs = _rwkv_layer(xs, bs, ss, g[0], state_shift[j], state_wkv[j], p)
            outs["shp"].append(shp)
            outs["shs"].append(shs)
            outs["wkvp"].append(wkvp)
            outs["wkvs"].append(wkvs)
        xp, xnp = _add_norm(xp, mp, g[1], g[2])
        xs, xns = _add_norm(xs, ms, g[1], g[2])
        fp, cp = _ffn_layer(xnp, bp, sp, None, ffn_w_up, ffn_cw, ffn_cb, ffn_w_down, i)
        fs, cs = _ffn_layer(xns, bs, ss, state_conv[i], ffn_w_up, ffn_cw, ffn_cb, ffn_w_down, i)
        outs["convp"].append(cp)
        outs["convs"].append(cs)
        if g_after is not None and (i + 1) % 3 == 0:
            xp, xnp = _add_norm(xp, fp, g[3], g_after)
            xs, xns = _add_norm(xs, fs, g[3], g_after)
        else:
            xp, xnp = _add_norm(xp, fp, g[3]), None
            xs, xns = _add_norm(xs, fs, g[3]), None
    st = lambda n: jnp.stack(outs[n])
    return (xp.reshape(bp, sp, d), xs.reshape(bs, ss, d),
            st("kp"), st("vp"), st("poolp"), st("shp"), st("wkvp"), st("convp"),
            st("kn"), st("vn"), st("pools"), st("shs"), st("wkvs"), st("convs"))
```

```python
import functools
import math

import jax
import jax.numpy as jnp
from jax import lax
from jax.experimental import pallas as pl
from jax.experimental.pallas import tpu as pltpu

BF16 = jnp.bfloat16
F32 = jnp.float32

CHUNK = 64
HEAD_DIM = 128
N_GROUPS = 4
ROPE_THETA = 10000.0
POOL_WINDOWS = (2, 4, 8, 16)
POOL_HIST = 15
RW_HEAD = 64
LNX_EPS = 64e-5
NORM_EPS = 1e-6
PAST_LEN = 2048
CONV_W = 3
LOG_DECAY_MIN = -math.exp(-0.5)
RW_CHUNK = 64
RW_TILES_PER_STEP = 16
SOLVE_BLOCK = 16
CONV_ACT_BLOCK_BYTES = 2 * 1024 * 1024
FFN_COLS = 256
FFN_ROWS = 512

LANES = 128
VMEM_LIMIT_BYTES = 56 * 1024 * 1024
HALO_ROWS = 16


def _params(*sem):
    return pltpu.CompilerParams(dimension_semantics=sem, vmem_limit_bytes=VMEM_LIMIT_BYTES)


def _pick(n, prefs):
    for p in prefs:
        if n % p == 0:
            return p
    return n


def _rms(xf, g):
    return xf * lax.rsqrt(jnp.mean(xf * xf, axis=-1, keepdims=True) + NORM_EPS) * g


def _seg_ones(rows):
    r = lax.broadcasted_iota(jnp.int32, (rows, LANES), 0)
    c = lax.broadcasted_iota(jnp.int32, (rows, LANES), 1)
    return ((r % LANES) // RW_HEAD == c // RW_HEAD).astype(BF16)


def _seg_sum(p, jj):
    hi = p.astype(BF16)
    lo = (p - hi.astype(F32)).astype(BF16)
    return jnp.dot(jnp.concatenate([hi, lo], axis=1), jj, preferred_element_type=F32)


def _mm_kernel(*refs, norm, rope_blocks):
    it = iter(refs)
    x_ref = next(it)
    g_ref = next(it) if norm else None
    w_ref = next(it)
    cos_ref = next(it) if rope_blocks else None
    sin_ref = next(it) if rope_blocks else None
    o_ref = next(it)
    xn_ref = next(it) if norm else None
    j = pl.program_id(1)

    if norm:
        @pl.when(j == 0)
        def _():
            xn_ref[...] = _rms(x_ref[...].astype(F32), g_ref[...]).astype(BF16)
        lhs = xn_ref[...]
    else:
        lhs = x_ref[...]
    acc = jnp.dot(lhs, w_ref[...], preferred_element_type=F32)

    if rope_blocks:
        @pl.when(j < rope_blocks)
        def _():
            cos = cos_ref[...]
            sin = sin_ref[...]
            for h in range(acc.shape[1] // HEAD_DIM):
                xh = acc[:, h * HEAD_DIM:(h + 1) * HEAD_DIM]
                o_ref[:, h * HEAD_DIM:(h + 1) * HEAD_DIM] = (
                    xh * cos + pltpu.roll(xh, HEAD_DIM // 2, 1) * sin).astype(o_ref.dtype)

        @pl.when(j >= rope_blocks)
        def _():
            o_ref[...] = acc.astype(o_ref.dtype)
    else:
        o_ref[...] = acc.astype(o_ref.dtype)


def _mm(x, w, out_dtype, *, g=None, rope=None):
    w, layer = w
    m, k = x.shape
    n = w.shape[2]
    norm = g is not None
    if norm or k > 8192:
        tm = min(512, m)
    else:
        tm = min(1024, m)
    tn = _pick(math.gcd(n, rope[2]) if rope else n, (512, 256, 128))
    rope_blocks = 0
    in_specs = [pl.BlockSpec((tm, k), lambda i, j: (i, 0))]
    args = [x]
    if norm:
        in_specs.append(pl.BlockSpec((1, k), lambda i, j: (0, 0)))
        args.append(g.reshape(1, k).astype(F32))
    in_specs.append(pl.BlockSpec((None, k, tn), lambda i, j: (layer, 0, j)))
    args.append(w)
    if rope is not None:
        cos, sin, n_cols = rope
        assert n_cols % tn == 0 and tn % HEAD_DIM == 0
        rope_blocks = n_cols // tn
        in_specs += [pl.BlockSpec((tm, HEAD_DIM), lambda i, j: (i, 0))] * 2
        args += [cos, sin]
    scratch = [pltpu.VMEM((tm, k), BF16)] if norm else []
    return pl.pallas_call(
        functools.partial(_mm_kernel, norm=norm, rope_blocks=rope_blocks),
        out_shape=jax.ShapeDtypeStruct((m, n), out_dtype),
        grid=(m // tm, n // tn),
        in_specs=in_specs,
        out_specs=pl.BlockSpec((tm, tn), lambda i, j: (i, j)),
        scratch_shapes=scratch,
        compiler_params=_params("arbitrary", "arbitrary"),
        name="mm",
    )(*args)


def _lora_kernel(x_ref, w1_ref, w2_ref, o_ref, *, act):
    h = jnp.dot(x_ref[...], w1_ref[...], preferred_element_type=F32)
    if act == "tanh":
        h = jnp.tanh(h)
    elif act == "sigmoid":
        h = jax.nn.sigmoid(h)
    o_ref[...] = jnp.dot(h.astype(BF16), w2_ref[...], preferred_element_type=F32).astype(o_ref.dtype)


def _lora(x, w1, w2, act=None):
    (w1, layer), (w2, _) = w1, w2
    m, k = x.shape
    r, n = w2.shape[1], w2.shape[2]
    tm = min(512, m)
    return pl.pallas_call(
        functools.partial(_lora_kernel, act=act),
        out_shape=jax.ShapeDtypeStruct((m, n), BF16),
        grid=(m // tm,),
        in_specs=[pl.BlockSpec((tm, k), lambda i: (i, 0)),
                  pl.BlockSpec((None, k, r), lambda i: (layer, 0, 0)),
                  pl.BlockSpec((None, r, n), lambda i: (layer, 0, 0))],
        out_specs=pl.BlockSpec((tm, n), lambda i: (i, 0)),
        compiler_params=_params("arbitrary"),
        name="lora",
    )(x, w1, w2)


def _add_norm_kernel(x_ref, m_ref, g_ref, *refs):
    x = x_ref[...] + _rms(m_ref[...].astype(F32), g_ref[...])
    if len(refs) == 1:
        refs[0][...] = x
    else:
        gn_ref, o_ref, on_ref = refs
        o_ref[...] = x
        on_ref[...] = _rms(x, gn_ref[...]).astype(on_ref.dtype)


def _add_norm(x, mix, g, g_next=None):
    m, d = x.shape
    tr = min(256, m)
    row = pl.BlockSpec((tr, d), lambda i: (i, 0))
    vec = pl.BlockSpec((1, d), lambda i: (0, 0))
    with_next = g_next is not None
    return pl.pallas_call(
        _add_norm_kernel,
        out_shape=((jax.ShapeDtypeStruct((m, d), F32), jax.ShapeDtypeStruct((m, d), BF16)) if with_next
                   else jax.ShapeDtypeStruct((m, d), F32)),
        grid=(m // tr,),
        in_specs=[row, row, vec] + ([vec] if with_next else []),
        out_specs=(row, row) if with_next else row,
        compiler_params=_params("arbitrary"),
        name="add_norm",
    )(x, mix, g.reshape(1, d), *([g_next.reshape(1, d)] if with_next else []))


def _attn_kernel(sink_ref, q_ref, *refs, n_pieces, n_kv, band_mask):
    k_refs = refs[:n_pieces]
    v_refs = refs[n_pieces:2 * n_pieces]
    o_ref = refs[2 * n_pieces]
    tq = q_ref.shape[0]
    c = pl.program_id(1)
    scale = HEAD_DIM ** -0.5
    scores = []
    for kv in range(n_kv):
        lo, hi = kv * HEAD_DIM, (kv + 1) * HEAD_DIM
        q4 = jnp.concatenate(
            [q_ref[:, (kv * N_GROUPS + r) * HEAD_DIM:(kv * N_GROUPS + r + 1) * HEAD_DIM] for r in range(N_GROUPS)],
            axis=0)
        kb = jnp.concatenate([kr[:, lo:hi].astype(BF16) for kr in k_refs], axis=0)
        scores.append(lax.dot_general(q4, kb, (((1,), (1,)), ((), ())), preferred_element_type=F32))
    s = jnp.concatenate(scores, axis=0) * scale
    if band_mask:
        col = lax.broadcasted_iota(jnp.int32, s.shape, 1)
        s = jnp.where(col >= CHUNK * (n_pieces - 1 - c), s, -1e30)
    sink = jnp.concatenate([jnp.full((tq, 1), sink_ref[h], F32) for h in range(n_kv * N_GROUPS)], axis=0)
    mx = jnp.maximum(jnp.max(s, axis=-1, keepdims=True), sink)
    p = jnp.exp(s - mx)
    denom = jnp.sum(p, axis=-1, keepdims=True) + jnp.exp(sink - mx)
    pb = p.astype(BF16)
    inv = 1.0 / denom
    rows = N_GROUPS * tq
    for kv in range(n_kv):
        vb = jnp.concatenate([vr[:, kv * HEAD_DIM:(kv + 1) * HEAD_DIM].astype(BF16) for vr in v_refs], axis=0)
        o = jnp.dot(pb[kv * rows:(kv + 1) * rows], vb, preferred_element_type=F32) * inv[kv * rows:(kv + 1) * rows]
        for r in range(N_GROUPS):
            h = kv * N_GROUPS + r
            o_ref[:, h * HEAD_DIM:(h + 1) * HEAD_DIM] = o[r * tq:(r + 1) * tq].astype(o_ref.dtype)


def _attn_prompt(qkv, sinks, b, s, d):
    kv_dim = d // N_GROUPS
    n_kv = kv_dim // HEAD_DIM
    nc = s // CHUNK
    n_band = 3
    kcol, vcol = d // kv_dim, d // kv_dim + 1

    def piece(jj, col):
        return pl.BlockSpec((CHUNK, kv_dim),
                            lambda bi, c, sk: (bi * nc + jnp.maximum(c - (n_band - 1) + jj, 0), col))

    in_specs = [pl.BlockSpec((CHUNK, d), lambda bi, c, sk: (bi * nc + c, 0))]
    in_specs += [piece(jj, kcol) for jj in range(n_band)] + [piece(jj, vcol) for jj in range(n_band)]
    return pl.pallas_call(
        functools.partial(_attn_kernel, n_pieces=n_band, n_kv=n_kv, band_mask=True),
        out_shape=jax.ShapeDtypeStruct((b * s, d), BF16),
        grid_spec=pltpu.PrefetchScalarGridSpec(
            num_scalar_prefetch=1, grid=(b, nc), in_specs=in_specs,
            out_specs=pl.BlockSpec((CHUNK, d), lambda bi, c, sk: (bi * nc + c, 0))),
        compiler_params=_params("arbitrary", "arbitrary"),
        name="attn_prompt",
    )(sinks, *([qkv] * (1 + 2 * n_band)))


def _attn_sample(qkv, cache_k, cache_v, sinks, b, t, d):
    kv_dim = d // N_GROUPS
    n_kv = kv_dim // HEAD_DIM
    win = cache_k.shape[1]
    kcol, vcol = d // kv_dim, d // kv_dim + 1
    in_specs = [
        pl.BlockSpec((t, d), lambda bi, c, sk: (bi, 0)),
        pl.BlockSpec((None, win, kv_dim), lambda bi, c, sk: (bi, 0, 0)),
        pl.BlockSpec((t, kv_dim), lambda bi, c, sk: (bi, kcol)),
        pl.BlockSpec((None, win, kv_dim), lambda bi, c, sk: (bi, 0, 0)),
        pl.BlockSpec((t, kv_dim), lambda bi, c, sk: (bi, vcol)),
    ]
    return pl.pallas_call(
        functools.partial(_attn_kernel, n_pieces=2, n_kv=n_kv, band_mask=False),
        out_shape=jax.ShapeDtypeStruct((b * t, d), BF16),
        grid_spec=pltpu.PrefetchScalarGridSpec(
            num_scalar_prefetch=1, grid=(b, 1), in_specs=in_specs,
            out_specs=pl.BlockSpec((t, d), lambda bi, c, sk: (bi, 0))),
        compiler_params=_params("arbitrary", "arbitrary"),
        name="attn_sample",
    )(sinks, qkv, cache_k, qkv, cache_v, qkv)


def _pool_kernel(x_ref, halo_ref, g_ref, w_ref, sc_ref, o_ref, st_ref, *, halo_is_x, pos0):
    si = pl.program_id(1)
    ts, d = x_ref.shape
    gw = d // len(POOL_WINDOWS)
    g = g_ref[...]
    un = _rms(x_ref[...], g)
    if halo_is_x:
        halo = jnp.where(si > 0, _rms(halo_ref[...], g), 0.0)
    else:
        halo = halo_ref[...]
    full = jnp.concatenate([halo, un], axis=0)
    pos = pos0 + si * ts + lax.broadcasted_iota(jnp.int32, (ts, 1), 0)
    for gi, win in enumerate(POOL_WINDOWS):
        lo, hi = gi * gw, (gi + 1) * gw
        acc = full[:, lo:hi]
        span = 1
        while span < win:
            acc = acc + pltpu.roll(acc, span, 0)
            span *= 2
        cnt = jnp.minimum(win, pos + 1).astype(F32)
        dlt = (acc[HALO_ROWS:] / cnt - un[:, lo:hi]).astype(BF16)
        y = jnp.dot(dlt, w_ref[gi], preferred_element_type=F32)
        o_ref[:, lo:hi] = (y * sc_ref[:, lo:hi]).astype(o_ref.dtype)
    st_ref[...] = full[ts:]


def _pool(x, halo, g, w, scale, b, s, *, halo_is_x, pos0):
    d = x.shape[1]
    ts = min(256, s)
    ns = s // ts
    hb = ts // HALO_ROWS
    if halo_is_x:
        halo_spec = pl.BlockSpec((HALO_ROWS, d), lambda bi, si: (jnp.maximum((bi * ns + si) * hb - 1, 0), 0))
    else:
        halo_spec = pl.BlockSpec((None, HALO_ROWS, d), lambda bi, si: (bi, 0, 0))
    ng, gw = w.shape[0], w.shape[1]
    return pl.pallas_call(
        functools.partial(_pool_kernel, halo_is_x=halo_is_x, pos0=pos0),
        out_shape=(jax.ShapeDtypeStruct((b * s, d), BF16), jax.ShapeDtypeStruct((b, HALO_ROWS, d), F32)),
        grid=(b, ns),
        in_specs=[pl.BlockSpec((ts, d), lambda bi, si: (bi * ns + si, 0)),
                  halo_spec,
                  pl.BlockSpec((1, d), lambda bi, si: (0, 0)),
                  pl.BlockSpec((ng, gw, gw), lambda bi, si: (0, 0, 0)),
                  pl.BlockSpec((1, d), lambda bi, si: (0, 0))],
        out_specs=(pl.BlockSpec((ts, d), lambda bi, si: (bi * ns + si, 0)),
                   pl.BlockSpec((None, HALO_ROWS, d), lambda bi, si: (bi, 0, 0))),
        compiler_params=_params("arbitrary", "arbitrary"),
        name="pool",
    )(x, halo, g.reshape(1, d), w, scale.reshape(1, d))


def _rw_mix_kernel(x_ref, halo_ref, g_ref, mix_ref, *o_refs, halo_is_x):
    si = pl.program_id(1)
    g = g_ref[...]
    un = _rms(x_ref[...], g)
    n_h = halo_ref.shape[0]
    if halo_is_x:
        prev_row = jnp.where(si > 0, _rms(halo_ref[n_h - 1:n_h, :], g), 0.0)
    else:
        prev_row = halo_ref[n_h - 1:n_h, :]
    row = lax.broadcasted_iota(jnp.int32, (un.shape[0], 1), 0)
    prev = jnp.where(row == 0, prev_row, pltpu.roll(un, 1, 0))
    xx = prev - un
    for jm in range(6):
        o_refs[jm][...] = (un + xx * mix_ref[jm:jm + 1, :]).astype(BF16)
    st_ref = o_refs[6]
    st_ref[...] = un[un.shape[0] - st_ref.shape[0]:]


def _rw_mix(x, halo, g, mix, b, s, *, halo_is_x):
    d = x.shape[1]
    ts = min(256, s)
    ns = s // ts
    hr = 8
    if halo_is_x:
        halo_spec = pl.BlockSpec((hr, d), lambda bi, si: (jnp.maximum((bi * ns + si) * (ts // hr) - 1, 0), 0))
    else:
        halo_spec = pl.BlockSpec((None, hr, d), lambda bi, si: (bi, 0, 0))
    row_spec = pl.BlockSpec((ts, d), lambda bi, si: (bi * ns + si, 0))
    return pl.pallas_call(
        functools.partial(_rw_mix_kernel, halo_is_x=halo_is_x),
        out_shape=tuple([jax.ShapeDtypeStruct((b * s, d), BF16)] * 6 + [jax.ShapeDtypeStruct((b, hr, d), F32)]),
        grid=(b, ns),
        in_specs=[row_spec, halo_spec,
                  pl.BlockSpec((1, d), lambda bi, si: (0, 0)),
                  pl.BlockSpec((8, d), lambda bi, si: (0, 0))],
        out_specs=tuple([row_spec] * 6 + [pl.BlockSpec((None, hr, d), lambda bi, si: (bi, 0, 0))]),
        compiler_params=_params("arbitrary", "arbitrary"),
        name="rw_mix",
    )(x, halo, g.reshape(1, d), jnp.pad(mix, ((0, 2), (0, 0))))


def _dot_nt(a, b):
    return lax.dot_general(a, b, (((1,), (1,)), ((), ())), preferred_element_type=F32)


def _rw_chunk(r, ld, k, v, kk, kka, zs):
    c, width = r.shape
    n = RW_HEAD
    tiles = [slice(g * LANES, (g + 1) * LANES) for g in range(width // LANES)]
    lanes = lambda parts: jnp.concatenate(parts, axis=1) if len(parts) > 1 else parts[0]
    row = lax.broadcasted_iota(jnp.int32, (c, 2 * c), 0)
    col = lax.broadcasted_iota(jnp.int32, (c, 2 * c), 1)
    s_idx = col % c
    first = col < c
    strict = s_idx < row
    incl = s_idx <= row
    head0 = lax.broadcasted_iota(jnp.int32, (c, width), 1) % LANES < n

    tri = (lax.broadcasted_iota(jnp.int32, (c, 3 * c), 1) % c <= lax.broadcasted_iota(jnp.int32, (c, 3 * c), 0))
    hi = ld.astype(BF16)
    mid = (ld - hi.astype(F32)).astype(BF16)
    lo = (ld - hi.astype(F32) - mid.astype(F32)).astype(BF16)
    cum = jnp.dot(tri.astype(BF16), jnp.concatenate([hi, mid, lo], axis=0), preferred_element_type=F32)

    e_cum = jnp.exp(cum)
    e_inv = jnp.exp(-cum)
    at = -kk * jnp.exp(cum - ld)
    rt = r * e_cum
    bt = kka * e_inv
    kt = k * e_inv
    w_last = e_cum[c - 1:c, :]

    zero = jnp.zeros_like(at)
    a0r0 = jnp.concatenate([jnp.where(head0, at, zero), jnp.where(head0, rt, zero)], axis=0).astype(BF16)
    a1 = jnp.where(head0, zero, at).astype(BF16)
    r1 = jnp.where(head0, zero, rt).astype(BF16)
    ar = jnp.concatenate([at, rt], axis=0).astype(BF16)
    bk = jnp.concatenate([bt, kt], axis=0).astype(BF16)
    kb = jnp.concatenate([kt, bt], axis=0).astype(BF16)
    vb = v.astype(BF16)
    vv = jnp.concatenate([vb, vb], axis=0)

    l2, g0r, g1r, mv, p = [], [], [], [], []
    for t, z in zip(tiles, zs):
        g0 = _dot_nt(a0r0[:, t], bk[:, t])
        g1a = _dot_nt(a1[:, t], kb[:, t])
        g1r.append(_dot_nt(r1[:, t], bk[:, t]))
        g0a = g0[:c]
        g0r.append(g0[c:])
        l2.append(jnp.where(strict, jnp.where(first, g0a, g1a), 0.0))
        mak = jnp.concatenate([jnp.where(strict & ~first, g0a, 0.0), jnp.where(strict & first, g1a, 0.0)], axis=0)
        mv.append(jnp.dot(mak.astype(BF16), vv[:, t], preferred_element_type=F32))
        p.append(jnp.dot(ar[:, t], z.astype(BF16), preferred_element_type=F32))
    mv, p = lanes(mv), lanes(p)
    u0 = p[:c] + jnp.where(head0, mv[:c], mv[c:])

    sb = SOLVE_BLOCK
    nb = c // sb
    spread = (lax.broadcasted_iota(jnp.int32, (2 * c, LANES), 0) // c
              == lax.broadcasted_iota(jnp.int32, (2 * c, LANES), 1) // n).astype(BF16)
    lane_b = lax.broadcasted_iota(jnp.int32, (sb, 2 * c), 1) % c
    first_b = lax.broadcasted_iota(jnp.int32, (sb, 2 * c), 1) < c
    head0_b = lax.broadcasted_iota(jnp.int32, (sb, width), 1) % LANES < n
    lcol = lanes([
        jnp.dot(jnp.concatenate([jnp.where(lane_b == blk * sb + s, l2g[blk * sb:(blk + 1) * sb], 0.0)
                                 for blk in range(nb) for s in range(sb - 1)], axis=0).astype(BF16),
                spread, preferred_element_type=F32) for l2g in l2])
    solved = []
    for blk in range(nb):
        ub = u0[blk * sb:(blk + 1) * sb]
        if blk > 0:
            done = jnp.concatenate(solved + [jnp.zeros((c - blk * sb, width), F32)], axis=0).astype(BF16)
            done = jnp.concatenate([done, done], axis=0)
            off = []
            for t, l2g in zip(tiles, l2):
                band = l2g[blk * sb:(blk + 1) * sb]
                lhs = jnp.concatenate([jnp.where(first_b, band, 0.0), jnp.where(first_b, 0.0, band)], axis=0)
                off.append(jnp.dot(lhs.astype(BF16), done[:, t], preferred_element_type=F32))
            off = lanes(off)
            ub = ub + jnp.where(head0_b, off[:sb], off[sb:])
        for s in range(sb - 1):
            i = blk * (sb - 1) + s
            ub = ub + lcol[i * sb:(i + 1) * sb] * ub[s:s + 1, :]
        solved.append(ub)
    u = jnp.concatenate(solved, axis=0)

    uv = jnp.concatenate([u, v], axis=0).astype(BF16)
    bkw = jnp.concatenate([bt * w_last, kt * w_last], axis=0)
    same_head = (lax.broadcasted_iota(jnp.int32, (LANES, LANES), 0) // n
                 == lax.broadcasted_iota(jnp.int32, (LANES, LANES), 1) // n)
    ny, z_new = [], []
    for t, z, g0rg, g1rg in zip(tiles, zs, g0r, g1r):
        nmat = jnp.concatenate([jnp.where(incl, g0rg, 0.0), jnp.where(incl, g1rg, 0.0)], axis=0).astype(BF16)
        ny.append(jnp.dot(nmat, uv[:, t], preferred_element_type=F32))
        upd_z = jnp.dot(bkw[:, t].T.astype(BF16), uv[:, t], preferred_element_type=F32)
        w_col = jnp.broadcast_to(w_last[:, t], (LANES, LANES)).T
        z_new.append(z * w_col + jnp.where(same_head, upd_z, 0.0))
    ny = lanes(ny)
    return p[c:] + jnp.where(head0, ny[:c], ny[c:]), z_new


def _seg_sum_wide(p, jj):
    parts = [_seg_sum(p[:, g * LANES:(g + 1) * LANES], jj) for g in range(p.shape[1] // LANES)]
    return jnp.concatenate(parts, axis=1) if len(parts) > 1 else parts[0]


_PV_W0, _PV_A0, _PV_KK, _PV_KA, _PV_RK, _PV_LN_G, _PV_LN_B = range(7)


def _rw_scan_kernel(r_ref, k_ref, v_ref, lw_ref, la_ref, g_ref, pv_ref, s0_ref, o_ref, sT_ref, z_ref):
    @pl.when(pl.program_id(2) == 0)
    def _():
        z_ref[...] = s0_ref[...]

    pv = lambda i: pv_ref[i:i + 1, :]
    jj = _seg_ones(2 * LANES)
    r = r_ref[...].astype(F32)
    k = k_ref[...].astype(F32)
    v = v_ref[...].astype(F32)
    a = jax.nn.sigmoid(pv(_PV_A0) + la_ref[...].astype(F32))
    ld = LOG_DECAY_MIN * jax.nn.sigmoid(pv(_PV_W0) + lw_ref[...].astype(F32))
    k2 = k * (1.0 + (a - 1.0) * pv(_PV_KA))
    kkr = k * pv(_PV_KK)
    kk = kkr / jnp.maximum(jnp.sqrt(_seg_sum_wide(kkr * kkr, jj)), 1e-12)

    tg = z_ref.shape[0]
    y, zs = _rw_chunk(r, ld, k2, v, kk, kk * a, [z_ref[g] for g in range(tg)])
    for g in range(tg):
        z_ref[g] = zs[g]
    sT_ref[...] = z_ref[...]

    yc = y - _seg_sum_wide(y, jj) * (1.0 / RW_HEAD)
    var = _seg_sum_wide(yc * yc, jj) * (1.0 / RW_HEAD)
    o = yc * lax.rsqrt(var + LNX_EPS) * pv(_PV_LN_G) + pv(_PV_LN_B)
    bonus = _seg_sum_wide(r * k2 * pv(_PV_RK), jj) * v
    o_ref[...] = ((o + bonus) * g_ref[...].astype(F32)).astype(o_ref.dtype)


def _rw_scan(r, k, v, lw, la, gate, pvec, s0, b, s):
    d = r.shape[1]
    ntile = d // LANES
    tg = min(RW_TILES_PER_STEP, ntile)
    tc = min(RW_CHUNK, s)
    nch = s // tc
    seq = pl.BlockSpec((tc, tg * LANES), lambda bi, gi, ci: (bi * nch + ci, gi))
    st = pl.BlockSpec((None, tg, LANES, LANES), lambda bi, gi, ci: (bi, gi, 0, 0))
    return pl.pallas_call(
        _rw_scan_kernel,
        out_shape=(jax.ShapeDtypeStruct((b * s, d), BF16), jax.ShapeDtypeStruct(s0.shape, F32)),
        grid=(b, ntile // tg, nch),
        in_specs=[seq] * 6 + [pl.BlockSpec((8, tg * LANES), lambda bi, gi, ci: (0, gi)), st],
        out_specs=(seq, st),
        scratch_shapes=[pltpu.VMEM((tg, LANES, LANES), F32)],
        compiler_params=_params("arbitrary", "arbitrary", "arbitrary"),
        name="rw_scan",
    )(r, k, v, lw, la, gate, pvec, s0)


def _gelu_tanh(x):
    return 0.5 * x * (1.0 + jnp.tanh(math.sqrt(2.0 / math.pi) * (x + 0.044715 * (x * x * x))))


def _conv_taps(full, cw, cb):
    return cb + full * cw[2:3] + pltpu.roll(full, 1, 0) * cw[1:2] + pltpu.roll(full, 2, 0) * cw[0:1]


def _conv_act_kernel(u_ref, h_ref, cw_ref, cb_ref, o_ref, *, tf):
    f = o_ref.shape[1]
    for j in range(f // tf):
        def conv(cols):
            full = jnp.concatenate([h_ref[:, cols], u_ref[:, cols].astype(F32)], axis=0)
            return _conv_taps(full, cw_ref[:, cols], cb_ref[:, cols])[HALO_ROWS:]
        gate = conv(slice(j * tf, (j + 1) * tf))
        val = conv(slice(f + j * tf, f + (j + 1) * tf))
        o_ref[:, j * tf:(j + 1) * tf] = (_gelu_tanh(gate) * val).astype(o_ref.dtype)


def _conv_act(up, halo, cw, cb, layer, b, s, tf):
    f = up.shape[1] // 2
    return pl.pallas_call(
        functools.partial(_conv_act_kernel, tf=tf),
        out_shape=jax.ShapeDtypeStruct((b * s, f), BF16),
        grid=(b,),
        in_specs=[pl.BlockSpec((s, 2 * f), lambda bi: (bi, 0)),
                  pl.BlockSpec((None, HALO_ROWS, 2 * f), lambda bi: (bi, 0, 0)),
                  pl.BlockSpec((None, 8, 2 * f), lambda bi: (layer, 0, 0)),
                  pl.BlockSpec((None, 1, 2 * f), lambda bi: (layer, 0, 0))],
        out_specs=pl.BlockSpec((s, f), lambda bi: (bi, 0)),
        compiler_params=_params("arbitrary"),
        name="conv_act",
    )(up, halo, cw, cb)


def _ffn_up_kernel(x_ref, wg_ref, wv_ref, cwg_ref, cwv_ref, cbg_ref, cbv_ref, h_ref, tg_ref, tv_ref, up_ref, *, rows):
    s = x_ref.shape[0]
    tf = h_ref.shape[1]
    n_pass = s // rows

    def project(k):
        xk = x_ref[k * rows:(k + 1) * rows, :]
        up_ref[k % 2, 0, 8:, :] = jnp.dot(xk, wg_ref[...], preferred_element_type=F32)
        up_ref[k % 2, 1, 8:, :] = jnp.dot(xk, wv_ref[...], preferred_element_type=F32)

    def conv(slot, part, cw_ref, cb_ref):
        buf = up_ref.at[slot, part]
        return (cb_ref[...] + buf[8:8 + rows, :] * cw_ref[2:3, :] + buf[7:7 + rows, :] * cw_ref[1:2, :]
                + buf[6:6 + rows, :] * cw_ref[0:1, :])

    zero = jnp.zeros((8, tf), F32)
    up_ref[0, 0, 0:8, :] = zero
    up_ref[0, 1, 0:8, :] = zero
    project(0)
    for k in range(n_pass):
        slot = k % 2
        if k + 1 < n_pass:
            project(k + 1)
            up_ref[1 - slot, 0, 0:8, :] = up_ref[slot, 0, rows:rows + 8, :]
            up_ref[1 - slot, 1, 0:8, :] = up_ref[slot, 1, rows:rows + 8, :]
        gate = conv(slot, 0, cwg_ref, cbg_ref)
        val = conv(slot, 1, cwv_ref, cbv_ref)
        h_ref[k * rows:(k + 1) * rows, :] = (_gelu_tanh(gate) * val).astype(h_ref.dtype)
    last = (n_pass - 1) % 2
    tg_ref[...] = up_ref[last, 0, rows:rows + 8, :]
    tv_ref[...] = up_ref[last, 1, rows:rows + 8, :]


def _ffn_up(xn, w, cw, cb, layer, b, s):
    d = xn.shape[1]
    f = w.shape[2] // 2
    tf = _pick(f, (FFN_COLS, LANES))
    nf = f // tf
    rows = min(FFN_ROWS, s)

    def cols(block_rows, off):
        return pl.BlockSpec((None, block_rows, tf), lambda bi, j: (layer, 0, off + j))

    tail = pl.BlockSpec((None, 8, tf), lambda bi, j: (bi, 0, j))
    h, tail_g, tail_v = pl.pallas_call(
        functools.partial(_ffn_up_kernel, rows=rows),
        out_shape=(jax.ShapeDtypeStruct((b * s, f), BF16), jax.ShapeDtypeStruct((b, 8, f), F32),
                   jax.ShapeDtypeStruct((b, 8, f), F32)),
        grid=(b, nf),
        in_specs=[pl.BlockSpec((s, d), lambda bi, j: (bi, 0)),
                  cols(d, 0), cols(d, nf), cols(8, 0), cols(8, nf), cols(1, 0), cols(1, nf)],
        out_specs=(pl.BlockSpec((s, tf), lambda bi, j: (bi, j)), tail, tail),
        scratch_shapes=[pltpu.VMEM((2, 2, 8 + rows, tf), F32)],
        compiler_params=_params("arbitrary", "arbitrary"),
        name="ffn_up",
    )(xn, w, w, cw, cw, cb, cb)
    return h, jnp.concatenate([tail_g, tail_v], axis=2)


def _rope_tables(pos):
    half = HEAD_DIM // 2
    inv = ROPE_THETA ** (-jnp.arange(half, dtype=F32) / half)
    ang = pos.astype(F32)[:, None] * inv[None, :]
    cos, sin = jnp.cos(ang), jnp.sin(ang)
    return jnp.concatenate([cos, cos], axis=1), jnp.concatenate([-sin, sin], axis=1)


def _state_to_tiles(wkv):
    b, h, n, _ = wkv.shape
    t = wkv.reshape(b, h // 2, 2, n, n).transpose(0, 1, 2, 4, 3)
    z = t[:, :, :, :, None, :] * jnp.eye(2, dtype=wkv.dtype)[None, None, :, None, :, None]
    return z.reshape(b, h // 2, 2 * n, 2 * n)


def _tiles_to_state(z, h):
    b = z.shape[0]
    n = RW_HEAD
    z = z.reshape(b, h // 2, 2, n, 2, n)
    t = jnp.stack([z[:, :, 0, :, 0, :], z[:, :, 1, :, 1, :]], axis=2)
    return t.transpose(0, 1, 2, 4, 3).reshape(b, h, n, n)


def _attn_layer(x, xn, b, s, g, w_qkv, w_o, sinks, pos, cache=None):
    d = x.shape[1]
    kv_dim = d // N_GROUPS
    cos, sin = _rope_tables(pos)
    cos, sin = jnp.tile(cos, (b, 1)), jnp.tile(sin, (b, 1))
    if xn is None:
        qkv = _mm(x, w_qkv, BF16, g=g, rope=(cos, sin, d + kv_dim))
    else:
        qkv = _mm(xn, w_qkv, BF16, rope=(cos, sin, d + kv_dim))
    if cache is None:
        o = _attn_prompt(qkv, sinks, b, s, d)
    else:
        o = _attn_sample(qkv, cache[0].reshape(b, -1, kv_dim), cache[1].reshape(b, -1, kv_dim), sinks, b, s, d)
    k3 = qkv[:, d:d + kv_dim].reshape(b, s, kv_dim // HEAD_DIM, HEAD_DIM).astype(F32)
    v3 = qkv[:, d + kv_dim:].reshape(b, s, kv_dim // HEAD_DIM, HEAD_DIM).astype(F32)
    return _mm(o, w_o, BF16), k3, v3


def _rwkv_layer(x, b, s, g, shift, wkv, p):
    d = x.shape[1]
    heads = d // RW_HEAD
    if shift is None:
        mixed = _rw_mix(x, x, g, p["mix"], b, s, halo_is_x=True)
        s0 = jnp.zeros((b, heads // 2, 2 * RW_HEAD, 2 * RW_HEAD), F32)
    else:
        halo = jnp.pad(shift.reshape(b, 1, d), ((0, 0), (7, 0), (0, 0)))
        mixed = _rw_mix(x, halo, g, p["mix"], b, s, halo_is_x=False)
        s0 = _state_to_tiles(wkv.astype(F32))
    xr, xw, xk, xv, xa, xg, tail = mixed
    r = _mm(xr, p["w_r"], BF16)
    k = _mm(xk, p["w_k"], BF16)
    v = _mm(xv, p["w_v"], BF16)
    lw = _lora(xw, p["w1"], p["w2"], act="tanh")
    la = _lora(xa, p["a1"], p["a2"])
    gate = _lora(xg, p["g1"], p["g2"], act="sigmoid")
    pvec = jnp.stack([p["w0"], p["a0"], p["k_k"], p["k_a"], p["r_k"].reshape(-1), p["lnx"][0], p["lnx"][1],
                      jnp.zeros_like(p["w0"])])
    o, s_t = _rw_scan(r, k, v, lw, la, gate, pvec, s0, b, s)
    return _mm(o, p["w_o"], BF16), tail[:, 7:8, :], _tiles_to_state(s_t, heads)


def _ffn_layer(xn, b, s, hist, w_up, cw, cb, w_down, layer):
    f = w_down.shape[1]
    if hist is None:
        h, tail = _ffn_up(xn, w_up, cw, cb, layer, b, s)
        tail = tail[:, 8 - (CONV_W - 1):, :]
    else:
        up = _mm(xn, (w_up, layer), BF16)
        halo = jnp.pad(hist, ((0, 0), (HALO_ROWS - (CONV_W - 1), 0), (0, 0)))
        h = _conv_act(up, halo, cw, cb, layer, b, s, _pick(f, (FFN_COLS, LANES)))
        tail = up.reshape(b, s, 2 * f)[:, s - (CONV_W - 1):, :].astype(F32)
    return _mm(h, (w_down, layer), BF16), tail


def kernel(x_prompt, x_sample, cache_k, cache_v, state_pool, state_shift, state_wkv, state_conv, norm_g, attn_w_qkv, attn_w_o, attn_sinks, pool_w, pool_scale, rw_mix, rw_w_r, rw_w_k, rw_w_v, rw_w_o, rw_w0, rw_w1, rw_w2, rw_a0, rw_a1, rw_a2, rw_g1, rw_g2, rw_k_k, rw_k_a, rw_r_k, rw_lnx, ffn_w_up, ffn_conv_w, ffn_conv_b, ffn_w_down):
    bp, sp, d = x_prompt.shape
    bs, ss, _ = x_sample.shape
    depth = norm_g.shape[0]
    window = cache_k.shape[2]
    xp = x_prompt.reshape(bp * sp, d)
    xs = x_sample.reshape(bs * ss, d)
    outs = {n: [] for n in ("kp", "vp", "kn", "vn", "poolp", "pools", "shp", "shs", "wkvp", "wkvs", "convp", "convs")}
    xnp = xns = None
    bf = lambda a: a.astype(BF16)
    attn_w_qkv, attn_w_o, ffn_w_up, ffn_w_down = bf(attn_w_qkv), bf(attn_w_o), bf(ffn_w_up), bf(ffn_w_down)
    rw_w = {n: bf(a) for n, a in dict(w_r=rw_w_r, w_k=rw_w_k, w_v=rw_w_v, w_o=rw_w_o, w1=rw_w1, w2=rw_w2,
                                      a1=rw_a1, a2=rw_a2, g1=rw_g1, g2=rw_g2).items()}
    ffn_cw = jnp.pad(ffn_conv_w, ((0, 0), (0, 8 - CONV_W), (0, 0)))
    ffn_cb = ffn_conv_b[:, None, :]
    for i in range(depth):
        kind, j = i % 3, i // 3
        g = norm_g[i]
        g_after = norm_g[i + 1, 0] if i + 1 < depth else None
        if kind == 0:
            wq, wo = (attn_w_qkv, j), (attn_w_o, j)
            mp, kp, vp = _attn_layer(xp, xnp, bp, sp, g[0], wq, wo, attn_sinks[j], jnp.arange(sp))
            ms, kn, vn = _attn_layer(xs, xns, bs, ss, g[0], wq, wo, attn_sinks[j], PAST_LEN + jnp.arange(ss),
                                     cache=(cache_k[j], cache_v[j]))
            outs["kp"].append(kp[:, sp - window:])
            outs["vp"].append(vp[:, sp - window:])
            outs["kn"].append(kn)
            outs["vn"].append(vn)
        elif kind == 1:
            wp = pool_w[j].astype(BF16)
            mp, hp = _pool(xp, xp, g[0], wp, pool_scale[j], bp, sp, halo_is_x=True, pos0=0)
            halo = jnp.pad(state_pool[j], ((0, 0), (HALO_ROWS - POOL_HIST, 0), (0, 0)))
            ms, hs = _pool(xs, halo, g[0], wp, pool_scale[j], bs, ss, halo_is_x=False, pos0=PAST_LEN)
            outs["poolp"].append(hp[:, HALO_ROWS - POOL_HIST:])
            outs["pools"].append(hs[:, HALO_ROWS - POOL_HIST:])
        else:
            p = dict(mix=rw_mix[j], w0=rw_w0[j], a0=rw_a0[j], k_k=rw_k_k[j], k_a=rw_k_a[j], r_k=rw_r_k[j],
                     lnx=rw_lnx[j], **{n: (a, j) for n, a in rw_w.items()})
            mp, shp, wkvp = _rwkv_layer(xp, bp, sp, g[0], None, None, p)
            ms, shs, wkvs = _rwkv_layer(xs, bs, ss, g[0], state_shift[j], state_wkv[j], p)
            outs["shp"].append(shp)
            outs["shs"].append(shs)
            outs["wkvp"].append(wkvp)
            outs["wkvs"].append(wkvs)
        xp, xnp = _add_norm(xp, mp, g[1], g[2])
        xs, xns = _add_norm(xs, ms, g[1], g[2])
        fp, cp = _ffn_layer(xnp, bp, sp, None, ffn_w_up, ffn_cw, ffn_cb, ffn_w_down, i)
        fs, cs = _ffn_layer(xns, bs, ss, state_conv[i], ffn_w_up, ffn_cw, ffn_cb, ffn_w_down, i)
        outs["convp"].append(cp)
        outs["convs"].append(cs)
        if g_after is not None and (i + 1) % 3 == 0:
            xp, xnp = _add_norm(xp, fp, g[3], g_after)
            xs, xns = _add_norm(xs, fs, g[3], g_after)
        else:
            xp, xnp = _add_norm(xp, fp, g[3]), None
            xs, xns = _add_norm(xs, fs, g[3]), None
    st = lambda n: jnp.stack(outs[n])
    return (xp.reshape(bp, sp, d), xs.reshape(bs, ss, d),
            st("kp"), st("vp"), st("poolp"), st("shp"), st("wkvp"), st("convp"),
            st("kn"), st("vn"), st("pools"), st("shs"), st("wkvs"), st("convs"))
```

```python
import functools
import math

import jax
import jax.numpy as jnp
from jax import lax
from jax.experimental import pallas as pl
from jax.experimental.pallas import tpu as pltpu

BF16 = jnp.bfloat16
F32 = jnp.float32

CHUNK = 64
HEAD_DIM = 128
N_GROUPS = 4
ROPE_THETA = 10000.0
POOL_WINDOWS = (2, 4, 8, 16)
POOL_HIST = 15
RW_HEAD = 64
LNX_EPS = 64e-5
NORM_EPS = 1e-6
PAST_LEN = 2048
CONV_W = 3
LOG_DECAY_MIN = -math.exp(-0.5)
RW_CHUNK = 64
RW_TILES_PER_STEP = 16
SOLVE_BLOCK = 16
CONV_ACT_BLOCK_BYTES = 2 * 1024 * 1024
MM_ROWS = 1024
MM_DEEP_K = 8192
MM_ROWS_DEEP_K = 512
FFN_COLS = 256
FFN_ROWS = 512

LANES = 128
VMEM_LIMIT_BYTES = 56 * 1024 * 1024
HALO_ROWS = 16


def _params(*sem):
    return pltpu.CompilerParams(dimension_semantics=sem, vmem_limit_bytes=VMEM_LIMIT_BYTES)


def _pick(n, prefs):
    for p in prefs:
        if n % p == 0:
            return p
    return n


def _rms(xf, g):
    return xf * lax.rsqrt(jnp.mean(xf * xf, axis=-1, keepdims=True) + NORM_EPS) * g


def _seg_ones(rows):
    r = lax.broadcasted_iota(jnp.int32, (rows, LANES), 0)
    c = lax.broadcasted_iota(jnp.int32, (rows, LANES), 1)
    return ((r % LANES) // RW_HEAD == c // RW_HEAD).astype(BF16)


def _seg_sum(p, jj):
    hi = p.astype(BF16)
    lo = (p - hi.astype(F32)).astype(BF16)
    return jnp.dot(jnp.concatenate([hi, lo], axis=1), jj, preferred_element_type=F32)


def _mm_kernel(x_ref, w_ref, *refs, rope_blocks):
    o_ref = refs[-1]
    j = pl.program_id(1)
    acc = jnp.dot(x_ref[...], w_ref[...], preferred_element_type=F32)

    if rope_blocks:
        @pl.when(j < rope_blocks)
        def _():
            cos = refs[0][...]
            sin = refs[1][...]
            for h in range(acc.shape[1] // HEAD_DIM):
                xh = acc[:, h * HEAD_DIM:(h + 1) * HEAD_DIM]
                o_ref[:, h * HEAD_DIM:(h + 1) * HEAD_DIM] = (
                    xh * cos + pltpu.roll(xh, HEAD_DIM // 2, 1) * sin).astype(o_ref.dtype)

        @pl.when(j >= rope_blocks)
        def _():
            o_ref[...] = acc.astype(o_ref.dtype)
    else:
        o_ref[...] = acc.astype(o_ref.dtype)


def _mm(x, w, out_dtype, *, rope=None):
    w, layer = w
    m, k = x.shape
    n = w.shape[2]
    tm = min(MM_ROWS_DEEP_K if k > MM_DEEP_K else MM_ROWS, m)
    tn = _pick(math.gcd(n, rope[2]) if rope else n, (512, 256, 128))
    rope_blocks = 0
    in_specs = [pl.BlockSpec((tm, k), lambda i, j: (i, 0)),
                pl.BlockSpec((None, k, tn), lambda i, j: (layer, 0, j))]
    args = [x, w]
    if rope is not None:
        cos, sin, n_cols = rope
        assert n_cols % tn == 0 and tn % HEAD_DIM == 0
        rope_blocks = n_cols // tn
        in_specs += [pl.BlockSpec((tm, HEAD_DIM), lambda i, j: (i, 0))] * 2
        args += [cos, sin]
    return pl.pallas_call(
        functools.partial(_mm_kernel, rope_blocks=rope_blocks),
        out_shape=jax.ShapeDtypeStruct((m, n), out_dtype),
        grid=(m // tm, n // tn),
        in_specs=in_specs,
        out_specs=pl.BlockSpec((tm, tn), lambda i, j: (i, j)),
        compiler_params=_params("arbitrary", "arbitrary"),
        name="mm",
    )(*args)


def _norm_kernel(x_ref, g_ref, o_ref):
    o_ref[...] = _rms(x_ref[...], g_ref[...]).astype(o_ref.dtype)


def _norm(x, g):
    m, d = x.shape
    tr = min(256, m)
    return pl.pallas_call(
        _norm_kernel,
        out_shape=jax.ShapeDtypeStruct((m, d), BF16),
        grid=(m // tr,),
        in_specs=[pl.BlockSpec((tr, d), lambda i: (i, 0)), pl.BlockSpec((1, d), lambda i: (0, 0))],
        out_specs=pl.BlockSpec((tr, d), lambda i: (i, 0)),
        compiler_params=_params("arbitrary"),
        name="norm",
    )(x, g.reshape(1, d))


def _add_norm_kernel(x_ref, m_ref, g_ref, *refs):
    x = x_ref[...] + _rms(m_ref[...].astype(F32), g_ref[...])
    if len(refs) == 1:
        refs[0][...] = x
    else:
        gn_ref, o_ref, on_ref = refs
        o_ref[...] = x
        on_ref[...] = _rms(x, gn_ref[...]).astype(on_ref.dtype)


def _add_norm(x, mix, g, g_next=None):
    m, d = x.shape
    tr = min(256, m)
    row = pl.BlockSpec((tr, d), lambda i: (i, 0))
    vec = pl.BlockSpec((1, d), lambda i: (0, 0))
    with_next = g_next is not None
    return pl.pallas_call(
        _add_norm_kernel,
        out_shape=((jax.ShapeDtypeStruct((m, d), F32), jax.ShapeDtypeStruct((m, d), BF16)) if with_next
                   else jax.ShapeDtypeStruct((m, d), F32)),
        grid=(m // tr,),
        in_specs=[row, row, vec] + ([vec] if with_next else []),
        out_specs=(row, row) if with_next else row,
        compiler_params=_params("arbitrary"),
        name="add_norm",
    )(x, mix, g.reshape(1, d), *([g_next.reshape(1, d)] if with_next else []))


def _attn_kernel(sink_ref, q_ref, *refs, n_pieces, n_kv, band_mask):
    k_refs = refs[:n_pieces]
    v_refs = refs[n_pieces:2 * n_pieces]
    o_ref = refs[2 * n_pieces]
    tq = q_ref.shape[0]
    c = pl.program_id(1)
    scale = HEAD_DIM ** -0.5
    scores = []
    for kv in range(n_kv):
        lo, hi = kv * HEAD_DIM, (kv + 1) * HEAD_DIM
        q4 = jnp.concatenate(
            [q_ref[:, (kv * N_GROUPS + r) * HEAD_DIM:(kv * N_GROUPS + r + 1) * HEAD_DIM] for r in range(N_GROUPS)],
            axis=0)
        kb = jnp.concatenate([kr[:, lo:hi].astype(BF16) for kr in k_refs], axis=0)
        scores.append(lax.dot_general(q4, kb, (((1,), (1,)), ((), ())), preferred_element_type=F32))
    s = jnp.concatenate(scores, axis=0) * scale
    if band_mask:
        col = lax.broadcasted_iota(jnp.int32, s.shape, 1)
        s = jnp.where(col >= CHUNK * (n_pieces - 1 - c), s, -1e30)
    sink = jnp.concatenate([jnp.full((tq, 1), sink_ref[h], F32) for h in range(n_kv * N_GROUPS)], axis=0)
    mx = jnp.maximum(jnp.max(s, axis=-1, keepdims=True), sink)
    p = jnp.exp(s - mx)
    denom = jnp.sum(p, axis=-1, keepdims=True) + jnp.exp(sink - mx)
    pb = p.astype(BF16)
    inv = 1.0 / denom
    rows = N_GROUPS * tq
    for kv in range(n_kv):
        vb = jnp.concatenate([vr[:, kv * HEAD_DIM:(kv + 1) * HEAD_DIM].astype(BF16) for vr in v_refs], axis=0)
        o = jnp.dot(pb[kv * rows:(kv + 1) * rows], vb, preferred_element_type=F32) * inv[kv * rows:(kv + 1) * rows]
        for r in range(N_GROUPS):
            h = kv * N_GROUPS + r
            o_ref[:, h * HEAD_DIM:(h + 1) * HEAD_DIM] = o[r * tq:(r + 1) * tq].astype(o_ref.dtype)


def _attn_prompt(qkv, sinks, b, s, d):
    kv_dim = d // N_GROUPS
    n_kv = kv_dim // HEAD_DIM
    nc = s // CHUNK
    n_band = 3
    kcol, vcol = d // kv_dim, d // kv_dim + 1

    def piece(jj, col):
        return pl.BlockSpec((CHUNK, kv_dim),
                            lambda bi, c, sk: (bi * nc + jnp.maximum(c - (n_band - 1) + jj, 0), col))

    in_specs = [pl.BlockSpec((CHUNK, d), lambda bi, c, sk: (bi * nc + c, 0))]
    in_specs += [piece(jj, kcol) for jj in range(n_band)] + [piece(jj, vcol) for jj in range(n_band)]
    return pl.pallas_call(
        functools.partial(_attn_kernel, n_pieces=n_band, n_kv=n_kv, band_mask=True),
        out_shape=jax.ShapeDtypeStruct((b * s, d), BF16),
        grid_spec=pltpu.PrefetchScalarGridSpec(
            num_scalar_prefetch=1, grid=(b, nc), in_specs=in_specs,
            out_specs=pl.BlockSpec((CHUNK, d), lambda bi, c, sk: (bi * nc + c, 0))),
        compiler_params=_params("arbitrary", "arbitrary"),
        name="attn_prompt",
    )(sinks, *([qkv] * (1 + 2 * n_band)))


def _attn_sample(qkv, cache_k, cache_v, sinks, b, t, d):
    kv_dim = d // N_GROUPS
    n_kv = kv_dim // HEAD_DIM
    win = cache_k.shape[1]
    kcol, vcol = d // kv_dim, d // kv_dim + 1
    in_specs = [
        pl.BlockSpec((t, d), lambda bi, c, sk: (bi, 0)),
        pl.BlockSpec((None, win, kv_dim), lambda bi, c, sk: (bi, 0, 0)),
        pl.BlockSpec((t, kv_dim), lambda bi, c, sk: (bi, kcol)),
        pl.BlockSpec((None, win, kv_dim), lambda bi, c, sk: (bi, 0, 0)),
        pl.BlockSpec((t, kv_dim), lambda bi, c, sk: (bi, vcol)),
    ]
    return pl.pallas_call(
        functools.partial(_attn_kernel, n_pieces=2, n_kv=n_kv, band_mask=False),
        out_shape=jax.ShapeDtypeStruct((b * t, d), BF16),
        grid_spec=pltpu.PrefetchScalarGridSpec(
            num_scalar_prefetch=1, grid=(b, 1), in_specs=in_specs,
            out_specs=pl.BlockSpec((t, d), lambda bi, c, sk: (bi, 0))),
        compiler_params=_params("arbitrary", "arbitrary"),
        name="attn_sample",
    )(sinks, qkv, cache_k, qkv, cache_v, qkv)


def _pool_kernel(x_ref, halo_ref, g_ref, w_ref, sc_ref, o_ref, st_ref, *, halo_is_x, pos0):
    si = pl.program_id(1)
    ts, d = x_ref.shape
    gw = d // len(POOL_WINDOWS)
    g = g_ref[...]
    un = _rms(x_ref[...], g)
    if halo_is_x:
        halo = jnp.where(si > 0, _rms(halo_ref[...], g), 0.0)
    else:
        halo = halo_ref[...]
    full = jnp.concatenate([halo, un], axis=0)
    pos = pos0 + si * ts + lax.broadcasted_iota(jnp.int32, (ts, 1), 0)
    for gi, win in enumerate(POOL_WINDOWS):
        lo, hi = gi * gw, (gi + 1) * gw
        acc = full[:, lo:hi]
        span = 1
        while span < win:
            acc = acc + pltpu.roll(acc, span, 0)
            span *= 2
        cnt = jnp.minimum(win, pos + 1).astype(F32)
        dlt = (acc[HALO_ROWS:] / cnt - un[:, lo:hi]).astype(BF16)
        y = jnp.dot(dlt, w_ref[gi], preferred_element_type=F32)
        o_ref[:, lo:hi] = (y * sc_ref[:, lo:hi]).astype(o_ref.dtype)
    st_ref[...] = full[ts:]


def _pool(x, halo, g, w, scale, b, s, *, halo_is_x, pos0):
    d = x.shape[1]
    ts = min(256, s)
    ns = s // ts
    hb = ts // HALO_ROWS
    if halo_is_x:
        halo_spec = pl.BlockSpec((HALO_ROWS, d), lambda bi, si: (jnp.maximum((bi * ns + si) * hb - 1, 0), 0))
    else:
        halo_spec = pl.BlockSpec((None, HALO_ROWS, d), lambda bi, si: (bi, 0, 0))
    ng, gw = w.shape[0], w.shape[1]
    return pl.pallas_call(
        functools.partial(_pool_kernel, halo_is_x=halo_is_x, pos0=pos0),
        out_shape=(jax.ShapeDtypeStruct((b * s, d), BF16), jax.ShapeDtypeStruct((b, HALO_ROWS, d), F32)),
        grid=(b, ns),
        in_specs=[pl.BlockSpec((ts, d), lambda bi, si: (bi * ns + si, 0)),
                  halo_spec,
                  pl.BlockSpec((1, d), lambda bi, si: (0, 0)),
                  pl.BlockSpec((ng, gw, gw), lambda bi, si: (0, 0, 0)),
                  pl.BlockSpec((1, d), lambda bi, si: (0, 0))],
        out_specs=(pl.BlockSpec((ts, d), lambda bi, si: (bi * ns + si, 0)),
                   pl.BlockSpec((None, HALO_ROWS, d), lambda bi, si: (bi, 0, 0))),
        compiler_params=_params("arbitrary", "arbitrary"),
        name="pool",
    )(x, halo, g.reshape(1, d), w, scale.reshape(1, d))


_MIX_R, _MIX_W, _MIX_K, _MIX_V, _MIX_A, _MIX_G = range(6)


def _rw_mix_kernel(x_ref, halo_ref, g_ref, mix_ref, w1_ref, a1_ref, g1_ref,
                   xr_ref, xk_ref, xv_ref, hw_ref, ha_ref, hg_ref, st_ref, *, halo_is_x):
    si = pl.program_id(1)
    g = g_ref[...]
    un = _rms(x_ref[...], g)
    n_h = halo_ref.shape[0]
    if halo_is_x:
        prev_row = jnp.where(si > 0, _rms(halo_ref[n_h - 1:n_h, :], g), 0.0)
    else:
        prev_row = halo_ref[n_h - 1:n_h, :]
    row = lax.broadcasted_iota(jnp.int32, (un.shape[0], 1), 0)
    prev = jnp.where(row == 0, prev_row, pltpu.roll(un, 1, 0))
    xx = prev - un
    mixed = lambda jm: (un + xx * mix_ref[jm:jm + 1, :]).astype(BF16)
    xr_ref[...] = mixed(_MIX_R)
    xk_ref[...] = mixed(_MIX_K)
    xv_ref[...] = mixed(_MIX_V)
    narrow = lambda jm, w_ref: jnp.dot(mixed(jm), w_ref[...], preferred_element_type=F32)
    hw_ref[...] = jnp.tanh(narrow(_MIX_W, w1_ref)).astype(BF16)
    ha_ref[...] = narrow(_MIX_A, a1_ref).astype(BF16)
    hg_ref[...] = jax.nn.sigmoid(narrow(_MIX_G, g1_ref)).astype(BF16)
    st_ref[...] = un[un.shape[0] - st_ref.shape[0]:]


def _rw_mix(x, halo, g, mix, w1, a1, g1, b, s, *, halo_is_x):
    d = x.shape[1]
    ts = min(256, s)
    ns = s // ts
    hr = 8
    if halo_is_x:
        halo_spec = pl.BlockSpec((hr, d), lambda bi, si: (jnp.maximum((bi * ns + si) * (ts // hr) - 1, 0), 0))
    else:
        halo_spec = pl.BlockSpec((None, hr, d), lambda bi, si: (bi, 0, 0))
    row_spec = pl.BlockSpec((ts, d), lambda bi, si: (bi * ns + si, 0))
    lows = [w1, a1, g1]
    ranks = [w[0].shape[2] for w in lows]
    return pl.pallas_call(
        functools.partial(_rw_mix_kernel, halo_is_x=halo_is_x),
        out_shape=tuple([jax.ShapeDtypeStruct((b * s, d), BF16)] * 3
                        + [jax.ShapeDtypeStruct((b * s, r), BF16) for r in ranks]
                        + [jax.ShapeDtypeStruct((b, hr, d), F32)]),
        grid=(b, ns),
        in_specs=[row_spec, halo_spec,
                  pl.BlockSpec((1, d), lambda bi, si: (0, 0)),
                  pl.BlockSpec((8, d), lambda bi, si: (0, 0))]
                 + [pl.BlockSpec((None, d, r), functools.partial(lambda bi, si, l: (l, 0, 0), l=w[1]))
                    for w, r in zip(lows, ranks)],
        out_specs=tuple([row_spec] * 3
                        + [pl.BlockSpec((ts, r), lambda bi, si: (bi * ns + si, 0)) for r in ranks]
                        + [pl.BlockSpec((None, hr, d), lambda bi, si: (bi, 0, 0))]),
        compiler_params=_params("arbitrary", "arbitrary"),
        name="rw_mix",
    )(x, halo, g.reshape(1, d), jnp.pad(mix, ((0, 2), (0, 0))), *[w[0] for w in lows])


def _dot_nt(a, b):
    return lax.dot_general(a, b, (((1,), (1,)), ((), ())), preferred_element_type=F32)


def _rw_chunk(r, ld, k, v, kk, kka, zs):
    c, width = r.shape
    n = RW_HEAD
    tiles = [slice(g * LANES, (g + 1) * LANES) for g in range(width // LANES)]
    lanes = lambda parts: jnp.concatenate(parts, axis=1) if len(parts) > 1 else parts[0]
    row = lax.broadcasted_iota(jnp.int32, (c, 2 * c), 0)
    col = lax.broadcasted_iota(jnp.int32, (c, 2 * c), 1)
    s_idx = col % c
    first = col < c
    strict = s_idx < row
    incl = s_idx <= row
    head0 = lax.broadcasted_iota(jnp.int32, (c, width), 1) % LANES < n

    tri = (lax.broadcasted_iota(jnp.int32, (c, 3 * c), 1) % c <= lax.broadcasted_iota(jnp.int32, (c, 3 * c), 0))
    hi = ld.astype(BF16)
    mid = (ld - hi.astype(F32)).astype(BF16)
    lo = (ld - hi.astype(F32) - mid.astype(F32)).astype(BF16)
    cum = jnp.dot(tri.astype(BF16), jnp.concatenate([hi, mid, lo], axis=0), preferred_element_type=F32)

    e_cum = jnp.exp(cum)
    e_inv = jnp.exp(-cum)
    at = -kk * jnp.exp(cum - ld)
    rt = r * e_cum
    bt = kka * e_inv
    kt = k * e_inv
    w_last = e_cum[c - 1:c, :]

    zero = jnp.zeros_like(at)
    a0r0 = jnp.concatenate([jnp.where(head0, at, zero), jnp.where(head0, rt, zero)], axis=0).astype(BF16)
    a1 = jnp.where(head0, zero, at).astype(BF16)
    r1 = jnp.where(head0, zero, rt).astype(BF16)
    ar = jnp.concatenate([at, rt], axis=0).astype(BF16)
    bk = jnp.concatenate([bt, kt], axis=0).astype(BF16)
    kb = jnp.concatenate([kt, bt], axis=0).astype(BF16)
    vb = v.astype(BF16)
    vv = jnp.concatenate([vb, vb], axis=0)

    l2, g0r, g1r, mv, p = [], [], [], [], []
    for t, z in zip(tiles, zs):
        g0 = _dot_nt(a0r0[:, t], bk[:, t])
        g1a = _dot_nt(a1[:, t], kb[:, t])
        g1r.append(_dot_nt(r1[:, t], bk[:, t]))
        g0a = g0[:c]
        g0r.append(g0[c:])
        l2.append(jnp.where(strict, jnp.where(first, g0a, g1a), 0.0))
        mak = jnp.concatenate([jnp.where(strict & ~first, g0a, 0.0), jnp.where(strict & first, g1a, 0.0)], axis=0)
        mv.append(jnp.dot(mak.astype(BF16), vv[:, t], preferred_element_type=F32))
        p.append(jnp.dot(ar[:, t], z.astype(BF16), preferred_element_type=F32))
    mv, p = lanes(mv), lanes(p)
    u0 = p[:c] + jnp.where(head0, mv[:c], mv[c:])

    sb = SOLVE_BLOCK
    nb = c // sb
    spread = (lax.broadcasted_iota(jnp.int32, (2 * c, LANES), 0) // c
              == lax.broadcasted_iota(jnp.int32, (2 * c, LANES), 1) // n).astype(BF16)
    lane_b = lax.broadcasted_iota(jnp.int32, (sb, 2 * c), 1) % c
    first_b = lax.broadcasted_iota(jnp.int32, (sb, 2 * c), 1) < c
    head0_b = lax.broadcasted_iota(jnp.int32, (sb, width), 1) % LANES < n
    lcol = lanes([
        jnp.dot(jnp.concatenate([jnp.where(lane_b == blk * sb + s, l2g[blk * sb:(blk + 1) * sb], 0.0)
                                 for blk in range(nb) for s in range(sb - 1)], axis=0).astype(BF16),
                spread, preferred_element_type=F32) for l2g in l2])
    solved = []
    for blk in range(nb):
        ub = u0[blk * sb:(blk + 1) * sb]
        if blk > 0:
            done = jnp.concatenate(solved + [jnp.zeros((c - blk * sb, width), F32)], axis=0).astype(BF16)
            done = jnp.concatenate([done, done], axis=0)
            off = []
            for t, l2g in zip(tiles, l2):
                band = l2g[blk * sb:(blk + 1) * sb]
                lhs = jnp.concatenate([jnp.where(first_b, band, 0.0), jnp.where(first_b, 0.0, band)], axis=0)
                off.append(jnp.dot(lhs.astype(BF16), done[:, t], preferred_element_type=F32))
            off = lanes(off)
            ub = ub + jnp.where(head0_b, off[:sb], off[sb:])
        for s in range(sb - 1):
            i = blk * (sb - 1) + s
            ub = ub + lcol[i * sb:(i + 1) * sb] * ub[s:s + 1, :]
        solved.append(ub)
    u = jnp.concatenate(solved, axis=0)

    uv = jnp.concatenate([u, v], axis=0).astype(BF16)
    bkw = jnp.concatenate([bt * w_last, kt * w_last], axis=0)
    same_head = (lax.broadcasted_iota(jnp.int32, (LANES, LANES), 0) // n
                 == lax.broadcasted_iota(jnp.int32, (LANES, LANES), 1) // n)
    ny, z_new = [], []
    for t, z, g0rg, g1rg in zip(tiles, zs, g0r, g1r):
        nmat = jnp.concatenate([jnp.where(incl, g0rg, 0.0), jnp.where(incl, g1rg, 0.0)], axis=0).astype(BF16)
        ny.append(jnp.dot(nmat, uv[:, t], preferred_element_type=F32))
        upd_z = jnp.dot(bkw[:, t].T.astype(BF16), uv[:, t], preferred_element_type=F32)
        w_col = jnp.broadcast_to(w_last[:, t], (LANES, LANES)).T
        z_new.append(z * w_col + jnp.where(same_head, upd_z, 0.0))
    ny = lanes(ny)
    return p[c:] + jnp.where(head0, ny[:c], ny[c:]), z_new


def _seg_sum_wide(p, jj):
    parts = [_seg_sum(p[:, g * LANES:(g + 1) * LANES], jj) for g in range(p.shape[1] // LANES)]
    return jnp.concatenate(parts, axis=1) if len(parts) > 1 else parts[0]


_PV_W0, _PV_A0, _PV_KK, _PV_KA, _PV_RK, _PV_LN_G, _PV_LN_B = range(7)


def _rw_scan_kernel(r_ref, k_ref, v_ref, hw_ref, ha_ref, hg_ref, w2_ref, a2_ref, g2_ref, pv_ref, s0_ref,
                    o_ref, sT_ref, z_ref):
    @pl.when(pl.program_id(2) == 0)
    def _():
        z_ref[...] = s0_ref[...]

    pv = lambda i: pv_ref[i:i + 1, :]
    wide = lambda h_ref, w_ref: jnp.dot(h_ref[...], w_ref[...], preferred_element_type=F32)
    jj = _seg_ones(2 * LANES)
    r = r_ref[...].astype(F32)
    k = k_ref[...].astype(F32)
    v = v_ref[...].astype(F32)
    a = jax.nn.sigmoid(pv(_PV_A0) + wide(ha_ref, a2_ref))
    ld = LOG_DECAY_MIN * jax.nn.sigmoid(pv(_PV_W0) + wide(hw_ref, w2_ref))
    k2 = k * (1.0 + (a - 1.0) * pv(_PV_KA))
    kkr = k * pv(_PV_KK)
    kk = kkr / jnp.maximum(jnp.sqrt(_seg_sum_wide(kkr * kkr, jj)), 1e-12)

    tg = z_ref.shape[0]
    y, zs = _rw_chunk(r, ld, k2, v, kk, kk * a, [z_ref[g] for g in range(tg)])
    for g in range(tg):
        z_ref[g] = zs[g]
    sT_ref[...] = z_ref[...]

    yc = y - _seg_sum_wide(y, jj) * (1.0 / RW_HEAD)
    var = _seg_sum_wide(yc * yc, jj) * (1.0 / RW_HEAD)
    o = yc * lax.rsqrt(var + LNX_EPS) * pv(_PV_LN_G) + pv(_PV_LN_B)
    bonus = _seg_sum_wide(r * k2 * pv(_PV_RK), jj) * v
    o_ref[...] = ((o + bonus) * wide(hg_ref, g2_ref)).astype(o_ref.dtype)


def _rw_scan(r, k, v, narrow, second, pvec, s0, b, s):
    d = r.shape[1]
    ntile = d // LANES
    tg = min(RW_TILES_PER_STEP, ntile)
    tc = min(RW_CHUNK, s)
    nch = s // tc
    seq = pl.BlockSpec((tc, tg * LANES), lambda bi, gi, ci: (bi * nch + ci, gi))
    st = pl.BlockSpec((None, tg, LANES, LANES), lambda bi, gi, ci: (bi, gi, 0, 0))
    ranks = [h.shape[1] for h in narrow]
    return pl.pallas_call(
        _rw_scan_kernel,
        out_shape=(jax.ShapeDtypeStruct((b * s, d), BF16), jax.ShapeDtypeStruct(s0.shape, F32)),
        grid=(b, ntile // tg, nch),
        in_specs=[seq] * 3
                 + [pl.BlockSpec((tc, rk), lambda bi, gi, ci: (bi * nch + ci, 0)) for rk in ranks]
                 + [pl.BlockSpec((None, rk, tg * LANES), functools.partial(lambda bi, gi, ci, l: (l, 0, gi), l=w[1]))
                    for w, rk in zip(second, ranks)]
                 + [pl.BlockSpec((8, tg * LANES), lambda bi, gi, ci: (0, gi)), st],
        out_specs=(seq, st),
        scratch_shapes=[pltpu.VMEM((tg, LANES, LANES), F32)],
        compiler_params=_params("arbitrary", "arbitrary", "arbitrary"),
        name="rw_scan",
    )(r, k, v, *narrow, *[w[0] for w in second], pvec, s0)


def _gelu_tanh(x):
    return 0.5 * x * (1.0 + jnp.tanh(math.sqrt(2.0 / math.pi) * (x + 0.044715 * (x * x * x))))


def _conv_taps(full, cw, cb):
    return cb + full * cw[2:3] + pltpu.roll(full, 1, 0) * cw[1:2] + pltpu.roll(full, 2, 0) * cw[0:1]


def _conv_act_kernel(u_ref, h_ref, cw_ref, cb_ref, o_ref, *, tf):
    f = o_ref.shape[1]
    for j in range(f // tf):
        def conv(cols):
            full = jnp.concatenate([h_ref[:, cols], u_ref[:, cols].astype(F32)], axis=0)
            return _conv_taps(full, cw_ref[:, cols], cb_ref[:, cols])[HALO_ROWS:]
        gate = conv(slice(j * tf, (j + 1) * tf))
        val = conv(slice(f + j * tf, f + (j + 1) * tf))
        o_ref[:, j * tf:(j + 1) * tf] = (_gelu_tanh(gate) * val).astype(o_ref.dtype)


def _conv_act(up, halo, cw, cb, layer, b, s, tf):
    f = up.shape[1] // 2
    return pl.pallas_call(
        functools.partial(_conv_act_kernel, tf=tf),
        out_shape=jax.ShapeDtypeStruct((b * s, f), BF16),
        grid=(b,),
        in_specs=[pl.BlockSpec((s, 2 * f), lambda bi: (bi, 0)),
                  pl.BlockSpec((None, HALO_ROWS, 2 * f), lambda bi: (bi, 0, 0)),
                  pl.BlockSpec((None, 8, 2 * f), lambda bi: (layer, 0, 0)),
                  pl.BlockSpec((None, 1, 2 * f), lambda bi: (layer, 0, 0))],
        out_specs=pl.BlockSpec((s, f), lambda bi: (bi, 0)),
        compiler_params=_params("arbitrary"),
        name="conv_act",
    )(up, halo, cw, cb)


def _ffn_up_kernel(x_ref, wg_ref, wv_ref, cwg_ref, cwv_ref, cbg_ref, cbv_ref, *refs, rows, cast_next):
    if cast_next:
        nu_ref, nd_ref, h_ref, tg_ref, tv_ref, nu_o, nd_o, up_ref = refs
        nu_o[...] = nu_ref[...].astype(nu_o.dtype)
        nd_o[...] = nd_ref[...].astype(nd_o.dtype)
    else:
        h_ref, tg_ref, tv_ref, up_ref = refs
    s = x_ref.shape[0]
    tf = h_ref.shape[1]
    n_pass = s // rows

    def project(k):
        xk = x_ref[k * rows:(k + 1) * rows, :]
        up_ref[k % 2, 0, 8:, :] = jnp.dot(xk, wg_ref[...], preferred_element_type=F32)
        up_ref[k % 2, 1, 8:, :] = jnp.dot(xk, wv_ref[...], preferred_element_type=F32)

    def conv(slot, part, cw_ref, cb_ref):
        buf = up_ref.at[slot, part]
        return (cb_ref[...] + buf[8:8 + rows, :] * cw_ref[2:3, :] + buf[7:7 + rows, :] * cw_ref[1:2, :]
                + buf[6:6 + rows, :] * cw_ref[0:1, :])

    zero = jnp.zeros((8, tf), F32)
    up_ref[0, 0, 0:8, :] = zero
    up_ref[0, 1, 0:8, :] = zero
    project(0)
    for k in range(n_pass):
        slot = k % 2
        if k + 1 < n_pass:
            project(k + 1)
            up_ref[1 - slot, 0, 0:8, :] = up_ref[slot, 0, rows:rows + 8, :]
            up_ref[1 - slot, 1, 0:8, :] = up_ref[slot, 1, rows:rows + 8, :]
        gate = conv(slot, 0, cwg_ref, cbg_ref)
        val = conv(slot, 1, cwv_ref, cbv_ref)
        h_ref[k * rows:(k + 1) * rows, :] = (_gelu_tanh(gate) * val).astype(h_ref.dtype)
    last = (n_pass - 1) % 2
    tg_ref[...] = up_ref[last, 0, rows:rows + 8, :]
    tv_ref[...] = up_ref[last, 1, rows:rows + 8, :]


def _ffn_up(xn, w, cw, cb, layer, b, s, cast_next=None):
    d = xn.shape[1]
    f = w.shape[2] // 2
    tf = _pick(f, (FFN_COLS, LANES))
    nf = f // tf
    rows = min(FFN_ROWS, s)

    def cols(block_rows, off, lead):
        return pl.BlockSpec((None, block_rows, tf), lambda bi, j: (lead, 0, off + j))

    tail = pl.BlockSpec((None, 8, tf), lambda bi, j: (bi, 0, j))
    in_specs = [pl.BlockSpec((s, d), lambda bi, j: (bi, 0)), cols(d, 0, 0), cols(d, nf, 0),
                cols(8, 0, layer), cols(8, nf, layer), cols(1, 0, layer), cols(1, nf, layer)]
    args = [xn, w, w, cw, cw, cb, cb]
    out_shape = [jax.ShapeDtypeStruct((b * s, f), BF16), jax.ShapeDtypeStruct((b, 8, f), F32),
                 jax.ShapeDtypeStruct((b, 8, f), F32)]
    out_specs = [pl.BlockSpec((s, tf), lambda bi, j: (bi, j)), tail, tail]
    if cast_next is not None:
        wu32, wd32, nxt = cast_next
        assert d % b == 0 and (d // b) % HALO_ROWS == 0 and (2 * tf) % LANES == 0
        for stack, blk, imap in ((wu32, (None, d // b, 2 * tf), lambda bi, j: (nxt, bi, j)),
                                 (wd32, (None, tf, d // b), lambda bi, j: (nxt, j, bi))):
            in_specs.append(pl.BlockSpec(blk, imap))
            args.append(stack)
            out_shape.append(jax.ShapeDtypeStruct((1,) + stack.shape[1:], BF16))
            out_specs.append(pl.BlockSpec(blk, functools.partial(lambda bi, j, m: (0,) + m(bi, j)[1:], m=imap)))
    res = pl.pallas_call(
        functools.partial(_ffn_up_kernel, rows=rows, cast_next=cast_next is not None),
        out_shape=tuple(out_shape),
        grid=(b, nf),
        in_specs=in_specs,
        out_specs=tuple(out_specs),
        scratch_shapes=[pltpu.VMEM((2, 2, 8 + rows, tf), F32)],
        compiler_params=_params("arbitrary", "arbitrary"),
        name="ffn_up",
    )(*args)
    return (res[0], jnp.concatenate([res[1], res[2]], axis=2)) + tuple(res[3:])


def _rope_tables(pos):
    half = HEAD_DIM // 2
    inv = ROPE_THETA ** (-jnp.arange(half, dtype=F32) / half)
    ang = pos.astype(F32)[:, None] * inv[None, :]
    cos, sin = jnp.cos(ang), jnp.sin(ang)
    return jnp.concatenate([cos, cos], axis=1), jnp.concatenate([-sin, sin], axis=1)


def _state_to_tiles(wkv):
    b, h, n, _ = wkv.shape
    t = wkv.reshape(b, h // 2, 2, n, n).transpose(0, 1, 2, 4, 3)
    z = t[:, :, :, :, None, :] * jnp.eye(2, dtype=wkv.dtype)[None, None, :, None, :, None]
    return z.reshape(b, h // 2, 2 * n, 2 * n)


def _tiles_to_state(z, h):
    b = z.shape[0]
    n = RW_HEAD
    z = z.reshape(b, h // 2, 2, n, 2, n)
    t = jnp.stack([z[:, :, 0, :, 0, :], z[:, :, 1, :, 1, :]], axis=2)
    return t.transpose(0, 1, 2, 4, 3).reshape(b, h, n, n)


def _attn_layer(xn, b, s, w_qkv, w_o, sinks, pos, keep, cache=None):
    d = xn.shape[1]
    kv_dim = d // N_GROUPS
    cos, sin = _rope_tables(pos)
    cos, sin = jnp.tile(cos, (b, 1)), jnp.tile(sin, (b, 1))
    qkv = _mm(xn, w_qkv, BF16, rope=(cos, sin, d + kv_dim))
    if cache is None:
        o = _attn_prompt(qkv, sinks, b, s, d)
    else:
        o = _attn_sample(qkv, cache[0].reshape(b, -1, kv_dim), cache[1].reshape(b, -1, kv_dim), sinks, b, s, d)
    kv = qkv.reshape(b, s, d + 2 * kv_dim)[:, s - keep:, d:].astype(F32)
    k3 = kv[:, :, :kv_dim].reshape(b, keep, kv_dim // HEAD_DIM, HEAD_DIM)
    v3 = kv[:, :, kv_dim:].reshape(b, keep, kv_dim // HEAD_DIM, HEAD_DIM)
    return _mm(o, w_o, BF16), k3, v3


def _rwkv_layer(x, b, s, g, shift, wkv, p):
    d = x.shape[1]
    heads = d // RW_HEAD
    first = (p["w1"], p["a1"], p["g1"])
    if shift is None:
        mixed = _rw_mix(x, x, g, p["mix"], *first, b, s, halo_is_x=True)
        s0 = jnp.zeros((b, heads // 2, 2 * RW_HEAD, 2 * RW_HEAD), F32)
    else:
        halo = jnp.pad(shift.reshape(b, 1, d), ((0, 0), (7, 0), (0, 0)))
        mixed = _rw_mix(x, halo, g, p["mix"], *first, b, s, halo_is_x=False)
        s0 = _state_to_tiles(wkv.astype(F32))
    xr, xk, xv, hw, ha, hg, tail = mixed
    r = _mm(xr, p["w_r"], BF16)
    k = _mm(xk, p["w_k"], BF16)
    v = _mm(xv, p["w_v"], BF16)
    pvec = jnp.stack([p["w0"], p["a0"], p["k_k"], p["k_a"], p["r_k"].reshape(-1), p["lnx"][0], p["lnx"][1],
                      jnp.zeros_like(p["w0"])])
    o, s_t = _rw_scan(r, k, v, (hw, ha, hg), (p["w2"], p["a2"], p["g2"]), pvec, s0, b, s)
    return _mm(o, p["w_o"], BF16), tail[:, 7:8, :], _tiles_to_state(s_t, heads)


def _ffn_layer(xn, b, s, hist, w_up, w_down, cw, cb, layer, cast_next=None):
    f = w_down.shape[1]
    nxt = ()
    if hist is None:
        h, tail, *nxt = _ffn_up(xn, w_up, cw, cb, layer, b, s, cast_next)
        tail = tail[:, 8 - (CONV_W - 1):, :]
    else:
        up = _mm(xn, (w_up, 0), BF16)
        halo = jnp.pad(hist, ((0, 0), (HALO_ROWS - (CONV_W - 1), 0), (0, 0)))
        h = _conv_act(up, halo, cw, cb, layer, b, s, _pick(f, (FFN_COLS, LANES)))
        tail = up.reshape(b, s, 2 * f)[:, s - (CONV_W - 1):, :].astype(F32)
    return _mm(h, (w_down, 0), BF16), tail, tuple(nxt)


def kernel(x_prompt, x_sample, cache_k, cache_v, state_pool, state_shift, state_wkv, state_conv, norm_g, attn_w_qkv, attn_w_o, attn_sinks, pool_w, pool_scale, rw_mix, rw_w_r, rw_w_k, rw_w_v, rw_w_o, rw_w0, rw_w1, rw_w2, rw_a0, rw_a1, rw_a2, rw_g1, rw_g2, rw_k_k, rw_k_a, rw_r_k, rw_lnx, ffn_w_up, ffn_conv_w, ffn_conv_b, ffn_w_down):
    bp, sp, d = x_prompt.shape
    bs, ss, _ = x_sample.shape
    depth = norm_g.shape[0]
    window = cache_k.shape[2]
    xp = x_prompt.reshape(bp * sp, d)
    xs = x_sample.reshape(bs * ss, d)
    outs = {n: [] for n in ("kp", "vp", "kn", "vn", "poolp", "pools", "shp", "shs", "wkvp", "wkvs", "convp", "convs")}
    xnp = xns = None
    bf = lambda a: a.astype(BF16)
    attn_w_qkv, attn_w_o = bf(attn_w_qkv), bf(attn_w_o)
    w_up, w_down = bf(ffn_w_up[:1]), bf(ffn_w_down[:1])
    rw_w = {n: bf(a) for n, a in dict(w_r=rw_w_r, w_k=rw_w_k, w_v=rw_w_v, w_o=rw_w_o, w1=rw_w1, w2=rw_w2,
                                      a1=rw_a1, a2=rw_a2, g1=rw_g1, g2=rw_g2).items()}
    ffn_cw = jnp.pad(ffn_conv_w, ((0, 0), (0, 8 - CONV_W), (0, 0)))
    ffn_cb = ffn_conv_b[:, None, :]
    for i in range(depth):
        kind, j = i % 3, i // 3
        g = norm_g[i]
        g_after = norm_g[i + 1, 0] if i + 1 < depth else None
        if kind == 0:
            wq, wo = (attn_w_qkv, j), (attn_w_o, j)
            if xnp is None:
                xnp, xns = _norm(xp, g[0]), _norm(xs, g[0])
            mp, kp, vp = _attn_layer(xnp, bp, sp, wq, wo, attn_sinks[j], jnp.arange(sp), window)
            ms, kn, vn = _attn_layer(xns, bs, ss, wq, wo, attn_sinks[j], PAST_LEN + jnp.arange(ss), ss,
                                     cache=(cache_k[j], cache_v[j]))
            outs["kp"].append(kp)
            outs["vp"].append(vp)
            outs["kn"].append(kn)
            outs["vn"].append(vn)
        elif kind == 1:
            wp = pool_w[j].astype(BF16)
            mp, hp = _pool(xp, xp, g[0], wp, pool_scale[j], bp, sp, halo_is_x=True, pos0=0)
            halo = jnp.pad(state_pool[j], ((0, 0), (HALO_ROWS - POOL_HIST, 0), (0, 0)))
            ms, hs = _pool(xs, halo, g[0], wp, pool_scale[j], bs, ss, halo_is_x=False, pos0=PAST_LEN)
            outs["poolp"].append(hp[:, HALO_ROWS - POOL_HIST:])
            outs["pools"].append(hs[:, HALO_ROWS - POOL_HIST:])
        else:
            p = dict(mix=rw_mix[j], w0=rw_w0[j], a0=rw_a0[j], k_k=rw_k_k[j], k_a=rw_k_a[j], r_k=rw_r_k[j],
                     lnx=rw_lnx[j], **{n: (a, j) for n, a in rw_w.items()})
            mp, shp, wkvp = _rwkv_layer(xp, bp, sp, g[0], None, None, p)
            ms, shs, wkvs = _rwkv_layer(xs, bs, ss, g[0], state_shift[j], state_wkv[j], p)
            outs["shp"].append(shp)
            outs["shs"].append(shs)
            outs["wkvp"].append(wkvp)
            outs["wkvs"].append(wkvs)
        xp, xnp = _add_norm(xp, mp, g[1], g[2])
        xs, xns = _add_norm(xs, ms, g[1], g[2])
        cast_next = (ffn_w_up, ffn_w_down, i + 1) if i + 1 < depth else None
        fp, cp, w_next = _ffn_layer(xnp, bp, sp, None, w_up, w_down, ffn_cw, ffn_cb, i, cast_next)
        fs, cs, _ = _ffn_layer(xns, bs, ss, state_conv[i], w_up, w_down, ffn_cw, ffn_cb, i)
        if w_next:
            w_up, w_down = w_next
        outs["convp"].append(cp)
        outs["convs"].append(cs)
        if g_after is not None and (i + 1) % 3 == 0:
            xp, xnp = _add_norm(xp, fp, g[3], g_after)
            xs, xns = _add_norm(xs, fs, g[3], g_after)
        else:
            xp, xnp = _add_norm(xp, fp, g[3]), None
            xs, xns = _add_norm(xs, fs, g[3]), None
    st = lambda n: jnp.stack(outs[n])
    return (xp.reshape(bp, sp, d), xs.reshape(bs, ss, d),
            st("kp"), st("vp"), st("poolp"), st("shp"), st("wkvp"), st("convp"),
            st("kn"), st("vn"), st("pools"), st("shs"), st("wkvs"), st("convs"))
```

```python
import functools
import math

import jax
import jax.numpy as jnp
from jax import lax
from jax.experimental import pallas as pl
from jax.experimental.pallas import tpu as pltpu

BF16 = jnp.bfloat16
F32 = jnp.float32

CHUNK = 64
HEAD_DIM = 128
N_GROUPS = 4
ROPE_THETA = 10000.0
POOL_WINDOWS = (2, 4, 8, 16)
POOL_HIST = 15
RW_HEAD = 64
LNX_EPS = 64e-5
NORM_EPS = 1e-6
PAST_LEN = 2048
CONV_W = 3
LOG_DECAY_MIN = -math.exp(-0.5)
RW_CHUNK = 64
RW_TILES_PER_STEP = 32
SOLVE_BLOCK = 16
CONV_ACT_BLOCK_BYTES = 2 * 1024 * 1024
ATTN_CHUNKS_PER_STEP = 4
MM_ROWS = 1024
MM_DEEP_K = 8192
MM_ROWS_DEEP_K = 512
FFN_COLS = 256
FFN_ROWS = 512

LANES = 128
VMEM_LIMIT_BYTES = 56 * 1024 * 1024
HALO_ROWS = 16


def _params(*sem):
    return pltpu.CompilerParams(dimension_semantics=sem, vmem_limit_bytes=VMEM_LIMIT_BYTES)


def _pick(n, prefs):
    for p in prefs:
        if n % p == 0:
            return p
    return n


def _rms(xf, g):
    return xf * lax.rsqrt(jnp.mean(xf * xf, axis=-1, keepdims=True) + NORM_EPS) * g


def _seg_ones(rows):
    r = lax.broadcasted_iota(jnp.int32, (rows, LANES), 0)
    c = lax.broadcasted_iota(jnp.int32, (rows, LANES), 1)
    return ((r % LANES) // RW_HEAD == c // RW_HEAD).astype(BF16)


def _seg_sum(p, jj):
    hi = p.astype(BF16)
    lo = (p - hi.astype(F32)).astype(BF16)
    return jnp.dot(jnp.concatenate([hi, lo], axis=1), jj, preferred_element_type=F32)


def _mm_kernel(x_ref, w_ref, *refs, rope_blocks):
    o_ref = refs[-1]
    j = pl.program_id(1)
    acc = jnp.dot(x_ref[...], w_ref[...], preferred_element_type=F32)

    if rope_blocks:
        @pl.when(j < rope_blocks)
        def _():
            cos = refs[0][...]
            sin = refs[1][...]
            for h in range(acc.shape[1] // HEAD_DIM):
                xh = acc[:, h * HEAD_DIM:(h + 1) * HEAD_DIM]
                o_ref[:, h * HEAD_DIM:(h + 1) * HEAD_DIM] = (
                    xh * cos + pltpu.roll(xh, HEAD_DIM // 2, 1) * sin).astype(o_ref.dtype)

        @pl.when(j >= rope_blocks)
        def _():
            o_ref[...] = acc.astype(o_ref.dtype)
    else:
        o_ref[...] = acc.astype(o_ref.dtype)


def _mm(x, w, out_dtype, *, rope=None):
    w, layer = w
    m, k = x.shape
    n = w.shape[2]
    tm = min(MM_ROWS_DEEP_K if k > MM_DEEP_K else MM_ROWS, m)
    tn = _pick(math.gcd(n, rope[2]) if rope else n, (512, 256, 128))
    rope_blocks = 0
    in_specs = [pl.BlockSpec((tm, k), lambda i, j: (i, 0)),
                pl.BlockSpec((None, k, tn), lambda i, j: (layer, 0, j))]
    args = [x, w]
    if rope is not None:
        cos, sin, n_cols = rope
        assert n_cols % tn == 0 and tn % HEAD_DIM == 0
        rope_blocks = n_cols // tn
        in_specs += [pl.BlockSpec((tm, HEAD_DIM), lambda i, j: (i, 0))] * 2
        args += [cos, sin]
    return pl.pallas_call(
        functools.partial(_mm_kernel, rope_blocks=rope_blocks),
        out_shape=jax.ShapeDtypeStruct((m, n), out_dtype),
        grid=(m // tm, n // tn),
        in_specs=in_specs,
        out_specs=pl.BlockSpec((tm, tn), lambda i, j: (i, j)),
        compiler_params=_params("arbitrary", "arbitrary"),
        name="mm",
    )(*args)


def _norm_kernel(x_ref, g_ref, o_ref):
    o_ref[...] = _rms(x_ref[...], g_ref[...]).astype(o_ref.dtype)


def _norm(x, g):
    m, d = x.shape
    tr = min(256, m)
    return pl.pallas_call(
        _norm_kernel,
        out_shape=jax.ShapeDtypeStruct((m, d), BF16),
        grid=(m // tr,),
        in_specs=[pl.BlockSpec((tr, d), lambda i: (i, 0)), pl.BlockSpec((1, d), lambda i: (0, 0))],
        out_specs=pl.BlockSpec((tr, d), lambda i: (i, 0)),
        compiler_params=_params("arbitrary"),
        name="norm",
    )(x, g.reshape(1, d))


def _add_norm_kernel(x_ref, m_ref, g_ref, *refs):
    x = x_ref[...] + _rms(m_ref[...].astype(F32), g_ref[...])
    if len(refs) == 1:
        refs[0][...] = x
    else:
        gn_ref, o_ref, on_ref = refs
        o_ref[...] = x
        on_ref[...] = _rms(x, gn_ref[...]).astype(on_ref.dtype)


def _add_norm(x, mix, g, g_next=None):
    m, d = x.shape
    tr = min(256, m)
    row = pl.BlockSpec((tr, d), lambda i: (i, 0))
    vec = pl.BlockSpec((1, d), lambda i: (0, 0))
    with_next = g_next is not None
    return pl.pallas_call(
        _add_norm_kernel,
        out_shape=((jax.ShapeDtypeStruct((m, d), F32), jax.ShapeDtypeStruct((m, d), BF16)) if with_next
                   else jax.ShapeDtypeStruct((m, d), F32)),
        grid=(m // tr,),
        in_specs=[row, row, vec] + ([vec] if with_next else []),
        out_specs=(row, row) if with_next else row,
        compiler_params=_params("arbitrary"),
        name="add_norm",
    )(x, mix, g.reshape(1, d), *([g_next.reshape(1, d)] if with_next else []))


def _attn_kernel(sink_ref, q_ref, *refs, n_pieces, n_sub, n_kv, band_mask):
    k_refs = refs[:n_pieces]
    v_refs = refs[n_pieces:2 * n_pieces]
    o_ref = refs[2 * n_pieces]
    tq = q_ref.shape[0] // n_sub
    n_band = n_pieces - n_sub + 1
    c = pl.program_id(1)
    scale = HEAD_DIM ** -0.5
    rows = N_GROUPS * tq
    scores = []
    for u in range(n_sub):
        for kv in range(n_kv):
            lo, hi = kv * HEAD_DIM, (kv + 1) * HEAD_DIM
            q4 = jnp.concatenate(
                [q_ref[u * tq:(u + 1) * tq, (kv * N_GROUPS + r) * HEAD_DIM:(kv * N_GROUPS + r + 1) * HEAD_DIM]
                 for r in range(N_GROUPS)], axis=0)
            kb = jnp.concatenate([kr[:, lo:hi].astype(BF16) for kr in k_refs[u:u + n_band]], axis=0)
            scores.append(lax.dot_general(q4, kb, (((1,), (1,)), ((), ())), preferred_element_type=F32))
    s = jnp.concatenate(scores, axis=0) * scale
    if band_mask:
        col = lax.broadcasted_iota(jnp.int32, s.shape, 1)
        sub = lax.broadcasted_iota(jnp.int32, s.shape, 0) // (n_kv * rows)
        s = jnp.where(col >= CHUNK * (n_band - 1 - n_sub * c - sub), s, -1e30)
    sink = jnp.concatenate([jnp.full((tq, 1), sink_ref[h], F32) for h in range(n_kv * N_GROUPS)] * n_sub, axis=0)
    mx = jnp.maximum(jnp.max(s, axis=-1, keepdims=True), sink)
    p = jnp.exp(s - mx)
    denom = jnp.sum(p, axis=-1, keepdims=True) + jnp.exp(sink - mx)
    pb = p.astype(BF16)
    inv = 1.0 / denom
    for u in range(n_sub):
        for kv in range(n_kv):
            vb = jnp.concatenate([vr[:, kv * HEAD_DIM:(kv + 1) * HEAD_DIM].astype(BF16)
                                  for vr in v_refs[u:u + n_band]], axis=0)
            blk = slice((u * n_kv + kv) * rows, (u * n_kv + kv + 1) * rows)
            o = jnp.dot(pb[blk], vb, preferred_element_type=F32) * inv[blk]
            for r in range(N_GROUPS):
                h = kv * N_GROUPS + r
                o_ref[u * tq:(u + 1) * tq, h * HEAD_DIM:(h + 1) * HEAD_DIM] = (
                    o[r * tq:(r + 1) * tq].astype(o_ref.dtype))


def _attn_prompt(qkv, sinks, b, s, d):
    kv_dim = d // N_GROUPS
    n_kv = kv_dim // HEAD_DIM
    nc = s // CHUNK
    n_band = 3
    n_sub = ATTN_CHUNKS_PER_STEP if nc % ATTN_CHUNKS_PER_STEP == 0 else 1
    n_pieces = n_band + n_sub - 1
    ns = nc // n_sub
    kcol, vcol = d // kv_dim, d // kv_dim + 1

    def piece(jj, col):
        return pl.BlockSpec((CHUNK, kv_dim),
                            lambda bi, c, sk: (bi * nc + jnp.maximum(n_sub * c - (n_band - 1) + jj, 0), col))

    q_spec = pl.BlockSpec((n_sub * CHUNK, d), lambda bi, c, sk: (bi * ns + c, 0))
    in_specs = [q_spec] + [piece(jj, kcol) for jj in range(n_pieces)] + [piece(jj, vcol) for jj in range(n_pieces)]
    return pl.pallas_call(
        functools.partial(_attn_kernel, n_pieces=n_pieces, n_sub=n_sub, n_kv=n_kv, band_mask=True),
        out_shape=jax.ShapeDtypeStruct((b * s, d), BF16),
        grid_spec=pltpu.PrefetchScalarGridSpec(
            num_scalar_prefetch=1, grid=(b, ns), in_specs=in_specs, out_specs=q_spec),
        compiler_params=_params("arbitrary", "arbitrary"),
        name="attn_prompt",
    )(sinks, *([qkv] * (1 + 2 * n_pieces)))


def _attn_sample(qkv, cache_k, cache_v, sinks, b, t, d):
    kv_dim = d // N_GROUPS
    n_kv = kv_dim // HEAD_DIM
    win = cache_k.shape[1]
    kcol, vcol = d // kv_dim, d // kv_dim + 1
    in_specs = [
        pl.BlockSpec((t, d), lambda bi, c, sk: (bi, 0)),
        pl.BlockSpec((None, win, kv_dim), lambda bi, c, sk: (bi, 0, 0)),
        pl.BlockSpec((t, kv_dim), lambda bi, c, sk: (bi, kcol)),
        pl.BlockSpec((None, win, kv_dim), lambda bi, c, sk: (bi, 0, 0)),
        pl.BlockSpec((t, kv_dim), lambda bi, c, sk: (bi, vcol)),
    ]
    return pl.pallas_call(
        functools.partial(_attn_kernel, n_pieces=2, n_sub=1, n_kv=n_kv, band_mask=False),
        out_shape=jax.ShapeDtypeStruct((b * t, d), BF16),
        grid_spec=pltpu.PrefetchScalarGridSpec(
            num_scalar_prefetch=1, grid=(b, 1), in_specs=in_specs,
            out_specs=pl.BlockSpec((t, d), lambda bi, c, sk: (bi, 0))),
        compiler_params=_params("arbitrary", "arbitrary"),
        name="attn_sample",
    )(sinks, qkv, cache_k, qkv, cache_v, qkv)


def _pool_kernel(x_ref, halo_ref, g_ref, w_ref, sc_ref, o_ref, st_ref, *, halo_is_x, pos0):
    si = pl.program_id(1)
    ts, d = x_ref.shape
    gw = d // len(POOL_WINDOWS)
    g = g_ref[...]
    un = _rms(x_ref[...], g)
    if halo_is_x:
        halo = jnp.where(si > 0, _rms(halo_ref[...], g), 0.0)
    else:
        halo = halo_ref[...]
    full = jnp.concatenate([halo, un], axis=0)
    pos = pos0 + si * ts + lax.broadcasted_iota(jnp.int32, (ts, 1), 0)
    for gi, win in enumerate(POOL_WINDOWS):
        lo, hi = gi * gw, (gi + 1) * gw
        acc = full[:, lo:hi]
        span = 1
        while span < win:
            acc = acc + pltpu.roll(acc, span, 0)
            span *= 2
        cnt = jnp.minimum(win, pos + 1).astype(F32)
        dlt = (acc[HALO_ROWS:] / cnt - un[:, lo:hi]).astype(BF16)
        y = jnp.dot(dlt, w_ref[gi], preferred_element_type=F32)
        o_ref[:, lo:hi] = (y * sc_ref[:, lo:hi]).astype(o_ref.dtype)
    st_ref[...] = full[ts:]


def _pool(x, halo, g, w, scale, b, s, *, halo_is_x, pos0):
    d = x.shape[1]
    ts = min(256, s)
    ns = s // ts
    hb = ts // HALO_ROWS
    if halo_is_x:
        halo_spec = pl.BlockSpec((HALO_ROWS, d), lambda bi, si: (jnp.maximum((bi * ns + si) * hb - 1, 0), 0))
    else:
        halo_spec = pl.BlockSpec((None, HALO_ROWS, d), lambda bi, si: (bi, 0, 0))
    ng, gw = w.shape[0], w.shape[1]
    return pl.pallas_call(
        functools.partial(_pool_kernel, halo_is_x=halo_is_x, pos0=pos0),
        out_shape=(jax.ShapeDtypeStruct((b * s, d), BF16), jax.ShapeDtypeStruct((b, HALO_ROWS, d), F32)),
        grid=(b, ns),
        in_specs=[pl.BlockSpec((ts, d), lambda bi, si: (bi * ns + si, 0)),
                  halo_spec,
                  pl.BlockSpec((1, d), lambda bi, si: (0, 0)),
                  pl.BlockSpec((ng, gw, gw), lambda bi, si: (0, 0, 0)),
                  pl.BlockSpec((1, d), lambda bi, si: (0, 0))],
        out_specs=(pl.BlockSpec((ts, d), lambda bi, si: (bi * ns + si, 0)),
                   pl.BlockSpec((None, HALO_ROWS, d), lambda bi, si: (bi, 0, 0))),
        compiler_params=_params("arbitrary", "arbitrary"),
        name="pool",
    )(x, halo, g.reshape(1, d), w, scale.reshape(1, d))


_MIX_R, _MIX_W, _MIX_K, _MIX_V, _MIX_A, _MIX_G = range(6)


def _rw_mix_kernel(x_ref, halo_ref, g_ref, mix_ref, w1_ref, a1_ref, g1_ref,
                   xr_ref, xk_ref, xv_ref, hw_ref, ha_ref, hg_ref, st_ref, *, halo_is_x):
    si = pl.program_id(1)
    g = g_ref[...]
    un = _rms(x_ref[...], g)
    n_h = halo_ref.shape[0]
    if halo_is_x:
        prev_row = jnp.where(si > 0, _rms(halo_ref[n_h - 1:n_h, :], g), 0.0)
    else:
        prev_row = halo_ref[n_h - 1:n_h, :]
    row = lax.broadcasted_iota(jnp.int32, (un.shape[0], 1), 0)
    prev = jnp.where(row == 0, prev_row, pltpu.roll(un, 1, 0))
    xx = prev - un
    mixed = lambda jm: (un + xx * mix_ref[jm:jm + 1, :]).astype(BF16)
    xr_ref[...] = mixed(_MIX_R)
    xk_ref[...] = mixed(_MIX_K)
    xv_ref[...] = mixed(_MIX_V)
    narrow = lambda jm, w_ref: jnp.dot(mixed(jm), w_ref[...], preferred_element_type=F32)
    hw_ref[...] = jnp.tanh(narrow(_MIX_W, w1_ref)).astype(BF16)
    ha_ref[...] = narrow(_MIX_A, a1_ref).astype(BF16)
    hg_ref[...] = jax.nn.sigmoid(narrow(_MIX_G, g1_ref)).astype(BF16)
    st_ref[...] = un[un.shape[0] - st_ref.shape[0]:]


def _rw_mix(x, halo, g, mix, w1, a1, g1, b, s, *, halo_is_x):
    d = x.shape[1]
    ts = min(256, s)
    ns = s // ts
    hr = 8
    if halo_is_x:
        halo_spec = pl.BlockSpec((hr, d), lambda bi, si: (jnp.maximum((bi * ns + si) * (ts // hr) - 1, 0), 0))
    else:
        halo_spec = pl.BlockSpec((None, hr, d), lambda bi, si: (bi, 0, 0))
    row_spec = pl.BlockSpec((ts, d), lambda bi, si: (bi * ns + si, 0))
    lows = [w1, a1, g1]
    ranks = [w[0].shape[2] for w in lows]
    return pl.pallas_call(
        functools.partial(_rw_mix_kernel, halo_is_x=halo_is_x),
        out_shape=tuple([jax.ShapeDtypeStruct((b * s, d), BF16)] * 3
                        + [jax.ShapeDtypeStruct((b * s, r), BF16) for r in ranks]
                        + [jax.ShapeDtypeStruct((b, hr, d), F32)]),
        grid=(b, ns),
        in_specs=[row_spec, halo_spec,
                  pl.BlockSpec((1, d), lambda bi, si: (0, 0)),
                  pl.BlockSpec((8, d), lambda bi, si: (0, 0))]
                 + [pl.BlockSpec((None, d, r), functools.partial(lambda bi, si, l: (l, 0, 0), l=w[1]))
                    for w, r in zip(lows, ranks)],
        out_specs=tuple([row_spec] * 3
                        + [pl.BlockSpec((ts, r), lambda bi, si: (bi * ns + si, 0)) for r in ranks]
                        + [pl.BlockSpec((None, hr, d), lambda bi, si: (bi, 0, 0))]),
        compiler_params=_params("arbitrary", "arbitrary"),
        name="rw_mix",
    )(x, halo, g.reshape(1, d), jnp.pad(mix, ((0, 2), (0, 0))), *[w[0] for w in lows])


def _dot_nt(a, b):
    return lax.dot_general(a, b, (((1,), (1,)), ((), ())), preferred_element_type=F32)


def _rw_chunk(r, ld, k, v, kk, kka, zs):
    c, width = r.shape
    n = RW_HEAD
    tiles = [slice(g * LANES, (g + 1) * LANES) for g in range(width // LANES)]
    lanes = lambda parts: jnp.concatenate(parts, axis=1) if len(parts) > 1 else parts[0]
    row = lax.broadcasted_iota(jnp.int32, (c, 2 * c), 0)
    col = lax.broadcasted_iota(jnp.int32, (c, 2 * c), 1)
    s_idx = col % c
    first = col < c
    strict = s_idx < row
    incl = s_idx <= row
    head0 = lax.broadcasted_iota(jnp.int32, (c, width), 1) % LANES < n

    tri = (lax.broadcasted_iota(jnp.int32, (c, 3 * c), 1) % c <= lax.broadcasted_iota(jnp.int32, (c, 3 * c), 0))
    hi = ld.astype(BF16)
    mid = (ld - hi.astype(F32)).astype(BF16)
    lo = (ld - hi.astype(F32) - mid.astype(F32)).astype(BF16)
    cum = jnp.dot(tri.astype(BF16), jnp.concatenate([hi, mid, lo], axis=0), preferred_element_type=F32)

    e_cum = jnp.exp(cum)
    e_inv = jnp.exp(-cum)
    at = -kk * jnp.exp(cum - ld)
    rt = r * e_cum
    bt = kka * e_inv
    kt = k * e_inv
    w_last = e_cum[c - 1:c, :]

    zero = jnp.zeros_like(at)
    a0r0 = jnp.concatenate([jnp.where(head0, at, zero), jnp.where(head0, rt, zero)], axis=0).astype(BF16)
    a1 = jnp.where(head0, zero, at).astype(BF16)
    r1 = jnp.where(head0, zero, rt).astype(BF16)
    ar = jnp.concatenate([at, rt], axis=0).astype(BF16)
    bk = jnp.concatenate([bt, kt], axis=0).astype(BF16)
    kb = jnp.concatenate([kt, bt], axis=0).astype(BF16)
    vb = v.astype(BF16)
    vv = jnp.concatenate([vb, vb], axis=0)

    l2, g0r, g1r, mv, p = [], [], [], [], []
    for t, z in zip(tiles, zs):
        g0 = _dot_nt(a0r0[:, t], bk[:, t])
        g1a = _dot_nt(a1[:, t], kb[:, t])
        g1r.append(_dot_nt(r1[:, t], bk[:, t]))
        g0a = g0[:c]
        g0r.append(g0[c:])
        l2.append(jnp.where(strict, jnp.where(first, g0a, g1a), 0.0))
        mak = jnp.concatenate([jnp.where(strict & ~first, g0a, 0.0), jnp.where(strict & first, g1a, 0.0)], axis=0)
        mv.append(jnp.dot(mak.astype(BF16), vv[:, t], preferred_element_type=F32))
        p.append(jnp.dot(ar[:, t], z.astype(BF16), preferred_element_type=F32))
    mv, p = lanes(mv), lanes(p)
    u0 = p[:c] + jnp.where(head0, mv[:c], mv[c:])

    sb = SOLVE_BLOCK
    nb = c // sb
    spread = (lax.broadcasted_iota(jnp.int32, (2 * c, LANES), 0) // c
              == lax.broadcasted_iota(jnp.int32, (2 * c, LANES), 1) // n).astype(BF16)
    lane_b = lax.broadcasted_iota(jnp.int32, (sb, 2 * c), 1) % c
    first_b = lax.broadcasted_iota(jnp.int32, (sb, 2 * c), 1) < c
    head0_b = lax.broadcasted_iota(jnp.int32, (sb, width), 1) % LANES < n
    lcol = lanes([
        jnp.dot(jnp.concatenate([jnp.where(lane_b == blk * sb + s, l2g[blk * sb:(blk + 1) * sb], 0.0)
                                 for blk in range(nb) for s in range(sb - 1)], axis=0).astype(BF16),
                spread, preferred_element_type=F32) for l2g in l2])
    solved = []
    for blk in range(nb):
        ub = u0[blk * sb:(blk + 1) * sb]
        if blk > 0:
            done = jnp.concatenate(solved + [jnp.zeros((c - blk * sb, width), F32)], axis=0).astype(BF16)
            done = jnp.concatenate([done, done], axis=0)
            off = []
            for t, l2g in zip(tiles, l2):
                band = l2g[blk * sb:(blk + 1) * sb]
                lhs = jnp.concatenate([jnp.where(first_b, band, 0.0), jnp.where(first_b, 0.0, band)], axis=0)
                off.append(jnp.dot(lhs.astype(BF16), done[:, t], preferred_element_type=F32))
            off = lanes(off)
            ub = ub + jnp.where(head0_b, off[:sb], off[sb:])
        for s in range(sb - 1):
            i = blk * (sb - 1) + s
            ub = ub + lcol[i * sb:(i + 1) * sb] * ub[s:s + 1, :]
        solved.append(ub)
    u = jnp.concatenate(solved, axis=0)

    uv = jnp.concatenate([u, v], axis=0).astype(BF16)
    bkw = jnp.concatenate([bt * w_last, kt * w_last], axis=0)
    same_head = (lax.broadcasted_iota(jnp.int32, (LANES, LANES), 0) // n
                 == lax.broadcasted_iota(jnp.int32, (LANES, LANES), 1) // n)
    ny, z_new = [], []
    for t, z, g0rg, g1rg in zip(tiles, zs, g0r, g1r):
        nmat = jnp.concatenate([jnp.where(incl, g0rg, 0.0), jnp.where(incl, g1rg, 0.0)], axis=0).astype(BF16)
        ny.append(jnp.dot(nmat, uv[:, t], preferred_element_type=F32))
        upd_z = jnp.dot(bkw[:, t].T.astype(BF16), uv[:, t], preferred_element_type=F32)
        w_col = jnp.broadcast_to(w_last[:, t], (LANES, LANES)).T
        z_new.append(z * w_col + jnp.where(same_head, upd_z, 0.0))
    ny = lanes(ny)
    return p[c:] + jnp.where(head0, ny[:c], ny[c:]), z_new


def _seg_sum_wide(p, jj):
    parts = [_seg_sum(p[:, g * LANES:(g + 1) * LANES], jj) for g in range(p.shape[1] // LANES)]
    return jnp.concatenate(parts, axis=1) if len(parts) > 1 else parts[0]


_PV_W0, _PV_A0, _PV_KK, _PV_KA, _PV_RK, _PV_LN_G, _PV_LN_B = range(7)


def _rw_scan_kernel(r_ref, k_ref, v_ref, hw_ref, ha_ref, hg_ref, w2_ref, a2_ref, g2_ref, pv_ref, s0_ref,
                    o_ref, sT_ref, z_ref):
    @pl.when(pl.program_id(2) == 0)
    def _():
        z_ref[...] = s0_ref[...]

    pv = lambda i: pv_ref[i:i + 1, :]
    wide = lambda h_ref, w_ref: jnp.dot(h_ref[...], w_ref[...], preferred_element_type=F32)
    jj = _seg_ones(2 * LANES)
    r = r_ref[...].astype(F32)
    k = k_ref[...].astype(F32)
    v = v_ref[...].astype(F32)
    a = jax.nn.sigmoid(pv(_PV_A0) + wide(ha_ref, a2_ref))
    ld = LOG_DECAY_MIN * jax.nn.sigmoid(pv(_PV_W0) + wide(hw_ref, w2_ref))
    k2 = k * (1.0 + (a - 1.0) * pv(_PV_KA))
    kkr = k * pv(_PV_KK)
    kk = kkr / jnp.maximum(jnp.sqrt(_seg_sum_wide(kkr * kkr, jj)), 1e-12)

    tg = z_ref.shape[0]
    y, zs = _rw_chunk(r, ld, k2, v, kk, kk * a, [z_ref[g] for g in range(tg)])
    for g in range(tg):
        z_ref[g] = zs[g]
    sT_ref[...] = z_ref[...]

    yc = y - _seg_sum_wide(y, jj) * (1.0 / RW_HEAD)
    var = _seg_sum_wide(yc * yc, jj) * (1.0 / RW_HEAD)
    o = yc * lax.rsqrt(var + LNX_EPS) * pv(_PV_LN_G) + pv(_PV_LN_B)
    bonus = _seg_sum_wide(r * k2 * pv(_PV_RK), jj) * v
    o_ref[...] = ((o + bonus) * wide(hg_ref, g2_ref)).astype(o_ref.dtype)


def _rw_scan(r, k, v, narrow, second, pvec, s0, b, s):
    d = r.shape[1]
    ntile = d // LANES
    tg = min(RW_TILES_PER_STEP, ntile)
    tc = min(RW_CHUNK, s)
    nch = s // tc
    seq = pl.BlockSpec((tc, tg * LANES), lambda bi, gi, ci: (bi * nch + ci, gi))
    st = pl.BlockSpec((None, tg, LANES, LANES), lambda bi, gi, ci: (bi, gi, 0, 0))
    ranks = [h.shape[1] for h in narrow]
    return pl.pallas_call(
        _rw_scan_kernel,
        out_shape=(jax.ShapeDtypeStruct((b * s, d), BF16), jax.ShapeDtypeStruct(s0.shape, F32)),
        grid=(b, ntile // tg, nch),
        in_specs=[seq] * 3
                 + [pl.BlockSpec((tc, rk), lambda bi, gi, ci: (bi * nch + ci, 0)) for rk in ranks]
                 + [pl.BlockSpec((None, rk, tg * LANES), functools.partial(lambda bi, gi, ci, l: (l, 0, gi), l=w[1]))
                    for w, rk in zip(second, ranks)]
                 + [pl.BlockSpec((8, tg * LANES), lambda bi, gi, ci: (0, gi)), st],
        out_specs=(seq, st),
        scratch_shapes=[pltpu.VMEM((tg, LANES, LANES), F32)],
        compiler_params=_params("arbitrary", "arbitrary", "arbitrary"),
        name="rw_scan",
    )(r, k, v, *narrow, *[w[0] for w in second], pvec, s0)


def _gelu_tanh(x):
    return 0.5 * x * (1.0 + jnp.tanh(math.sqrt(2.0 / math.pi) * (x + 0.044715 * (x * x * x))))


def _conv_taps(full, cw, cb):
    return cb + full * cw[2:3] + pltpu.roll(full, 1, 0) * cw[1:2] + pltpu.roll(full, 2, 0) * cw[0:1]


def _conv_act_kernel(u_ref, h_ref, cw_ref, cb_ref, o_ref, *, tf):
    f = o_ref.shape[1]
    for j in range(f // tf):
        def conv(cols):
            full = jnp.concatenate([h_ref[:, cols], u_ref[:, cols].astype(F32)], axis=0)
            return _conv_taps(full, cw_ref[:, cols], cb_ref[:, cols])[HALO_ROWS:]
        gate = conv(slice(j * tf, (j + 1) * tf))
        val = conv(slice(f + j * tf, f + (j + 1) * tf))
        o_ref[:, j * tf:(j + 1) * tf] = (_gelu_tanh(gate) * val).astype(o_ref.dtype)


def _conv_act(up, halo, cw, cb, layer, b, s, tf):
    f = up.shape[1] // 2
    return pl.pallas_call(
        functools.partial(_conv_act_kernel, tf=tf),
        out_shape=jax.ShapeDtypeStruct((b * s, f), BF16),
        grid=(b,),
        in_specs=[pl.BlockSpec((s, 2 * f), lambda bi: (bi, 0)),
                  pl.BlockSpec((None, HALO_ROWS, 2 * f), lambda bi: (bi, 0, 0)),
                  pl.BlockSpec((None, 8, 2 * f), lambda bi: (layer, 0, 0)),
                  pl.BlockSpec((None, 1, 2 * f), lambda bi: (layer, 0, 0))],
        out_specs=pl.BlockSpec((s, f), lambda bi: (bi, 0)),
        compiler_params=_params("arbitrary"),
        name="conv_act",
    )(up, halo, cw, cb)


def _ffn_up_kernel(x_ref, wg_ref, wv_ref, cwg_ref, cwv_ref, cbg_ref, cbv_ref, *refs, rows, cast_next):
    if cast_next:
        nu_ref, nd_ref, h_ref, tg_ref, tv_ref, nu_o, nd_o, up_ref = refs
        nu_o[...] = nu_ref[...].astype(nu_o.dtype)
        nd_o[...] = nd_ref[...].astype(nd_o.dtype)
    else:
        h_ref, tg_ref, tv_ref, up_ref = refs
    s = x_ref.shape[0]
    tf = h_ref.shape[1]
    n_pass = s // rows

    def project(k):
        xk = x_ref[k * rows:(k + 1) * rows, :]
        up_ref[k % 2, 0, 8:, :] = jnp.dot(xk, wg_ref[...], preferred_element_type=F32)
        up_ref[k % 2, 1, 8:, :] = jnp.dot(xk, wv_ref[...], preferred_element_type=F32)

    def conv(slot, part, cw_ref, cb_ref):
        buf = up_ref.at[slot, part]
        return (cb_ref[...] + buf[8:8 + rows, :] * cw_ref[2:3, :] + buf[7:7 + rows, :] * cw_ref[1:2, :]
                + buf[6:6 + rows, :] * cw_ref[0:1, :])

    zero = jnp.zeros((8, tf), F32)
    up_ref[0, 0, 0:8, :] = zero
    up_ref[0, 1, 0:8, :] = zero
    project(0)
    for k in range(n_pass):
        slot = k % 2
        if k + 1 < n_pass:
            project(k + 1)
            up_ref[1 - slot, 0, 0:8, :] = up_ref[slot, 0, rows:rows + 8, :]
            up_ref[1 - slot, 1, 0:8, :] = up_ref[slot, 1, rows:rows + 8, :]
        gate = conv(slot, 0, cwg_ref, cbg_ref)
        val = conv(slot, 1, cwv_ref, cbv_ref)
        h_ref[k * rows:(k + 1) * rows, :] = (_gelu_tanh(gate) * val).astype(h_ref.dtype)
    last = (n_pass - 1) % 2
    tg_ref[...] = up_ref[last, 0, rows:rows + 8, :]
    tv_ref[...] = up_ref[last, 1, rows:rows + 8, :]


def _ffn_up(xn, w, cw, cb, layer, b, s, cast_next=None):
    d = xn.shape[1]
    f = w.shape[2] // 2
    tf = _pick(f, (FFN_COLS, LANES))
    nf = f // tf
    rows = min(FFN_ROWS, s)

    def cols(block_rows, off, lead):
        return pl.BlockSpec((None, block_rows, tf), lambda bi, j: (lead, 0, off + j))

    tail = pl.BlockSpec((None, 8, tf), lambda bi, j: (bi, 0, j))
    in_specs = [pl.BlockSpec((s, d), lambda bi, j: (bi, 0)), cols(d, 0, 0), cols(d, nf, 0),
                cols(8, 0, layer), cols(8, nf, layer), cols(1, 0, layer), cols(1, nf, layer)]
    args = [xn, w, w, cw, cw, cb, cb]
    out_shape = [jax.ShapeDtypeStruct((b * s, f), BF16), jax.ShapeDtypeStruct((b, 8, f), F32),
                 jax.ShapeDtypeStruct((b, 8, f), F32)]
    out_specs = [pl.BlockSpec((s, tf), lambda bi, j: (bi, j)), tail, tail]
    if cast_next is not None:
        wu32, wd32, nxt = cast_next
        assert d % b == 0 and (d // b) % HALO_ROWS == 0 and (2 * tf) % LANES == 0
        for stack, blk, imap in ((wu32, (None, d // b, 2 * tf), lambda bi, j: (nxt, bi, j)),
                                 (wd32, (None, tf, d // b), lambda bi, j: (nxt, j, bi))):
            in_specs.append(pl.BlockSpec(blk, imap))
            args.append(stack)
            out_shape.append(jax.ShapeDtypeStruct((1,) + stack.shape[1:], BF16))
            out_specs.append(pl.BlockSpec(blk, functools.partial(lambda bi, j, m: (0,) + m(bi, j)[1:], m=imap)))
    res = pl.pallas_call(
        functools.partial(_ffn_up_kernel, rows=rows, cast_next=cast_next is not None),
        out_shape=tuple(out_shape),
        grid=(b, nf),
        in_specs=in_specs,
        out_specs=tuple(out_specs),
        scratch_shapes=[pltpu.VMEM((2, 2, 8 + rows, tf), F32)],
        compiler_params=_params("arbitrary", "arbitrary"),
        name="ffn_up",
    )(*args)
    return (res[0], jnp.concatenate([res[1], res[2]], axis=2)) + tuple(res[3:])


def _rope_tables(pos):
    half = HEAD_DIM // 2
    inv = ROPE_THETA ** (-jnp.arange(half, dtype=F32) / half)
    ang = pos.astype(F32)[:, None] * inv[None, :]
    cos, sin = jnp.cos(ang), jnp.sin(ang)
    return jnp.concatenate([cos, cos], axis=1), jnp.concatenate([-sin, sin], axis=1)


def _state_to_tiles(wkv):
    b, h, n, _ = wkv.shape
    t = wkv.reshape(b, h // 2, 2, n, n).transpose(0, 1, 2, 4, 3)
    z = t[:, :, :, :, None, :] * jnp.eye(2, dtype=wkv.dtype)[None, None, :, None, :, None]
    return z.reshape(b, h // 2, 2 * n, 2 * n)


def _tiles_to_state(z, h):
    b = z.shape[0]
    n = RW_HEAD
    z = z.reshape(b, h // 2, 2, n, 2, n)
    t = jnp.stack([z[:, :, 0, :, 0, :], z[:, :, 1, :, 1, :]], axis=2)
    return t.transpose(0, 1, 2, 4, 3).reshape(b, h, n, n)


def _attn_layer(xn, b, s, w_qkv, w_o, sinks, pos, keep, cache=None):
    d = xn.shape[1]
    kv_dim = d // N_GROUPS
    cos, sin = _rope_tables(pos)
    cos, sin = jnp.tile(cos, (b, 1)), jnp.tile(sin, (b, 1))
    qkv = _mm(xn, w_qkv, BF16, rope=(cos, sin, d + kv_dim))
    if cache is None:
        o = _attn_prompt(qkv, sinks, b, s, d)
    else:
        o = _attn_sample(qkv, cache[0].reshape(b, -1, kv_dim), cache[1].reshape(b, -1, kv_dim), sinks, b, s, d)
    kv = qkv.reshape(b, s, d + 2 * kv_dim)[:, s - keep:, d:].astype(F32)
    k3 = kv[:, :, :kv_dim].reshape(b, keep, kv_dim // HEAD_DIM, HEAD_DIM)
    v3 = kv[:, :, kv_dim:].reshape(b, keep, kv_dim // HEAD_DIM, HEAD_DIM)
    return _mm(o, w_o, BF16), k3, v3


def _rwkv_layer(x, b, s, g, shift, wkv, p):
    d = x.shape[1]
    heads = d // RW_HEAD
    first = (p["w1"], p["a1"], p["g1"])
    if shift is None:
        mixed = _rw_mix(x, x, g, p["mix"], *first, b, s, halo_is_x=True)
        s0 = jnp.zeros((b, heads // 2, 2 * RW_HEAD, 2 * RW_HEAD), F32)
    else:
        halo = jnp.pad(shift.reshape(b, 1, d), ((0, 0), (7, 0), (0, 0)))
        mixed = _rw_mix(x, halo, g, p["mix"], *first, b, s, halo_is_x=False)
        s0 = _state_to_tiles(wkv.astype(F32))
    xr, xk, xv, hw, ha, hg, tail = mixed
    r = _mm(xr, p["w_r"], BF16)
    k = _mm(xk, p["w_k"], BF16)
    v = _mm(xv, p["w_v"], BF16)
    pvec = jnp.stack([p["w0"], p["a0"], p["k_k"], p["k_a"], p["r_k"].reshape(-1), p["lnx"][0], p["lnx"][1],
                      jnp.zeros_like(p["w0"])])
    o, s_t = _rw_scan(r, k, v, (hw, ha, hg), (p["w2"], p["a2"], p["g2"]), pvec, s0, b, s)
    return _mm(o, p["w_o"], BF16), tail[:, 7:8, :], _tiles_to_state(s_t, heads)


def _ffn_layer(xn, b, s, hist, w_up, w_down, cw, cb, layer, cast_next=None):
    f = w_down.shape[1]
    nxt = ()
    if hist is None:
        h, tail, *nxt = _ffn_up(xn, w_up, cw, cb, layer, b, s, cast_next)
        tail = tail[:, 8 - (CONV_W - 1):, :]
    else:
        up = _mm(xn, (w_up, 0), BF16)
        halo = jnp.pad(hist, ((0, 0), (HALO_ROWS - (CONV_W - 1), 0), (0, 0)))
        h = _conv_act(up, halo, cw, cb, layer, b, s, _pick(f, (FFN_COLS, LANES)))
        tail = up.reshape(b, s, 2 * f)[:, s - (CONV_W - 1):, :].astype(F32)
    return _mm(h, (w_down, 0), BF16), tail, tuple(nxt)


def kernel(x_prompt, x_sample, cache_k, cache_v, state_pool, state_shift, state_wkv, state_conv, norm_g, attn_w_qkv, attn_w_o, attn_sinks, pool_w, pool_scale, rw_mix, rw_w_r, rw_w_k, rw_w_v, rw_w_o, rw_w0, rw_w1, rw_w2, rw_a0, rw_a1, rw_a2, rw_g1, rw_g2, rw_k_k, rw_k_a, rw_r_k, rw_lnx, ffn_w_up, ffn_conv_w, ffn_conv_b, ffn_w_down):
    bp, sp, d = x_prompt.shape
    bs, ss, _ = x_sample.shape
    depth = norm_g.shape[0]
    window = cache_k.shape[2]
    xp = x_prompt.reshape(bp * sp, d)
    xs = x_sample.reshape(bs * ss, d)
    outs = {n: [] for n in ("kp", "vp", "kn", "vn", "poolp", "pools", "shp", "shs", "wkvp", "wkvs", "convp", "convs")}
    xnp = xns = None
    bf = lambda a: a.astype(BF16)
    attn_w_qkv, attn_w_o = bf(attn_w_qkv), bf(attn_w_o)
    w_up, w_down = bf(ffn_w_up[:1]), bf(ffn_w_down[:1])
    rw_w = {n: bf(a) for n, a in dict(w_r=rw_w_r, w_k=rw_w_k, w_v=rw_w_v, w_o=rw_w_o, w1=rw_w1, w2=rw_w2,
                                      a1=rw_a1, a2=rw_a2, g1=rw_g1, g2=rw_g2).items()}
    ffn_cw = jnp.pad(ffn_conv_w, ((0, 0), (0, 8 - CONV_W), (0, 0)))
    ffn_cb = ffn_conv_b[:, None, :]
    for i in range(depth):
        kind, j = i % 3, i // 3
        g = norm_g[i]
        g_after = norm_g[i + 1, 0] if i + 1 < depth else None
        if kind == 0:
            wq, wo = (attn_w_qkv, j), (attn_w_o, j)
            if xnp is None:
                xnp, xns = _norm(xp, g[0]), _norm(xs, g[0])
            mp, kp, vp = _attn_layer(xnp, bp, sp, wq, wo, attn_sinks[j], jnp.arange(sp), window)
            ms, kn, vn = _attn_layer(xns, bs, ss, wq, wo, attn_sinks[j], PAST_LEN + jnp.arange(ss), ss,
                                     cache=(cache_k[j], cache_v[j]))
            outs["kp"].append(kp)
            outs["vp"].append(vp)
            outs["kn"].append(kn)
            outs["vn"].append(vn)
        elif kind == 1:
            wp = pool_w[j].astype(BF16)
            mp, hp = _pool(xp, xp, g[0], wp, pool_scale[j], bp, sp, halo_is_x=True, pos0=0)
            halo = jnp.pad(state_pool[j], ((0, 0), (HALO_ROWS - POOL_HIST, 0), (0, 0)))
            ms, hs = _pool(xs, halo, g[0], wp, pool_scale[j], bs, ss, halo_is_x=False, pos0=PAST_LEN)
            outs["poolp"].append(hp[:, HALO_ROWS - POOL_HIST:])
            outs["pools"].append(hs[:, HALO_ROWS - POOL_HIST:])
        else:
            p = dict(mix=rw_mix[j], w0=rw_w0[j], a0=rw_a0[j], k_k=rw_k_k[j], k_a=rw_k_a[j], r_k=rw_r_k[j],
                     lnx=rw_lnx[j], **{n: (a, j) for n, a in rw_w.items()})
            mp, shp, wkvp = _rwkv_layer(xp, bp, sp, g[0], None, None, p)
            ms, shs, wkvs = _rwkv_layer(xs, bs, ss, g[0], state_shift[j], state_wkv[j], p)
            outs["shp"].append(shp)
            outs["shs"].append(shs)
            outs["wkvp"].append(wkvp)
            outs["wkvs"].append(wkvs)
        xp, xnp = _add_norm(xp, mp, g[1], g[2])
        xs, xns = _add_norm(xs, ms, g[1], g[2])
        cast_next = (ffn_w_up, ffn_w_down, i + 1) if i + 1 < depth else None
        fp, cp, w_next = _ffn_layer(xnp, bp, sp, None, w_up, w_down, ffn_cw, ffn_cb, i, cast_next)
        fs, cs, _ = _ffn_layer(xns, bs, ss, state_conv[i], w_up, w_down, ffn_cw, ffn_cb, i)
        if w_next:
            w_up, w_down = w_next
        outs["convp"].append(cp)
        outs["convs"].append(cs)
        if g_after is not None and (i + 1) % 3 == 0:
            xp, xnp = _add_norm(xp, fp, g[3], g_after)
            xs, xns = _add_norm(xs, fs, g[3], g_after)
        else:
            xp, xnp = _add_norm(xp, fp, g[3]), None
            xs, xns = _add_norm(xs, fs, g[3]), None
    st = lambda n: jnp.stack(outs[n])
    return (xp.reshape(bp, sp, d), xs.reshape(bs, ss, d),
            st("kp"), st("vp"), st("poolp"), st("shp"), st("wkvp"), st("convp"),
            st("kn"), st("vn"), st("pools"), st("shs"), st("wkvs"), st("convs"))
```

```python
import functools
import math

import jax
import jax.numpy as jnp
from jax import lax
from jax.experimental import pallas as pl
from jax.experimental.pallas import tpu as pltpu

BF16 = jnp.bfloat16
F32 = jnp.float32

CHUNK = 64
HEAD_DIM = 128
N_GROUPS = 4
ROPE_THETA = 10000.0
POOL_WINDOWS = (2, 4, 8, 16)
POOL_HIST = 15
RW_HEAD = 64
LNX_EPS = 64e-5
NORM_EPS = 1e-6
PAST_LEN = 2048
CONV_W = 3
LOG_DECAY_MIN = -math.exp(-0.5)
RW_CHUNK = 64
RW_TILES_PER_STEP = 32
SOLVE_BLOCK = 16
ATTN_CHUNKS_PER_STEP = 4
MM_ROWS = 1024
MM_DEEP_K = 8192
MM_ROWS_DEEP_K = 512
FFN_COLS = 256
FFN_ROWS = 512

LANES = 128
VMEM_LIMIT_BYTES = 56 * 1024 * 1024
HALO_ROWS = 16


def _params(*sem):
    return pltpu.CompilerParams(dimension_semantics=sem, vmem_limit_bytes=VMEM_LIMIT_BYTES)


def _pick(n, prefs):
    for p in prefs:
        if n % p == 0:
            return p
    return n


def _rms(xf, g):
    return xf * lax.rsqrt(jnp.mean(xf * xf, axis=-1, keepdims=True) + NORM_EPS) * g


def _seg_ones(rows):
    r = lax.broadcasted_iota(jnp.int32, (rows, LANES), 0)
    c = lax.broadcasted_iota(jnp.int32, (rows, LANES), 1)
    return ((r % LANES) // RW_HEAD == c // RW_HEAD).astype(BF16)


def _seg_sum(p, jj):
    hi = p.astype(BF16)
    lo = (p - hi.astype(F32)).astype(BF16)
    return jnp.dot(jnp.concatenate([hi, lo], axis=1), jj, preferred_element_type=F32)


def _mm_kernel(x_ref, w_ref, *refs, rope_blocks):
    o_ref = refs[-1]
    j = pl.program_id(1)
    acc = jnp.dot(x_ref[...], w_ref[...], preferred_element_type=F32)

    if rope_blocks:
        @pl.when(j < rope_blocks)
        def _():
            cos = refs[0][...]
            sin = refs[1][...]
            for h in range(acc.shape[1] // HEAD_DIM):
                xh = acc[:, h * HEAD_DIM:(h + 1) * HEAD_DIM]
                o_ref[:, h * HEAD_DIM:(h + 1) * HEAD_DIM] = (
                    xh * cos + pltpu.roll(xh, HEAD_DIM // 2, 1) * sin).astype(o_ref.dtype)

        @pl.when(j >= rope_blocks)
        def _():
            o_ref[...] = acc.astype(o_ref.dtype)
    else:
        o_ref[...] = acc.astype(o_ref.dtype)


def _mm(x, w, out_dtype, *, rope=None):
    w, layer = w
    m, k = x.shape
    n = w.shape[2]
    tm = min(MM_ROWS_DEEP_K if k > MM_DEEP_K else MM_ROWS, m)
    tn = _pick(math.gcd(n, rope[2]) if rope else n, (512, 256, 128))
    rope_blocks = 0
    in_specs = [pl.BlockSpec((tm, k), lambda i, j: (i, 0)),
                pl.BlockSpec((None, k, tn), lambda i, j: (layer, 0, j))]
    args = [x, w]
    if rope is not None:
        cos, sin, n_cols = rope
        assert n_cols % tn == 0 and tn % HEAD_DIM == 0
        rope_blocks = n_cols // tn
        in_specs += [pl.BlockSpec((tm, HEAD_DIM), lambda i, j: (i, 0))] * 2
        args += [cos, sin]
    return pl.pallas_call(
        functools.partial(_mm_kernel, rope_blocks=rope_blocks),
        out_shape=jax.ShapeDtypeStruct((m, n), out_dtype),
        grid=(m // tm, n // tn),
        in_specs=in_specs,
        out_specs=pl.BlockSpec((tm, tn), lambda i, j: (i, j)),
        compiler_params=_params("arbitrary", "arbitrary"),
        name="mm",
    )(*args)


def _norm_kernel(x_ref, g_ref, o_ref):
    o_ref[...] = _rms(x_ref[...], g_ref[...]).astype(o_ref.dtype)


def _norm(x, g):
    m, d = x.shape
    tr = min(256, m)
    return pl.pallas_call(
        _norm_kernel,
        out_shape=jax.ShapeDtypeStruct((m, d), BF16),
        grid=(m // tr,),
        in_specs=[pl.BlockSpec((tr, d), lambda i: (i, 0)), pl.BlockSpec((1, d), lambda i: (0, 0))],
        out_specs=pl.BlockSpec((tr, d), lambda i: (i, 0)),
        compiler_params=_params("arbitrary"),
        name="norm",
    )(x, g.reshape(1, d))


def _add_norm_kernel(x_ref, m_ref, g_ref, *refs):
    x = x_ref[...] + _rms(m_ref[...].astype(F32), g_ref[...])
    if len(refs) == 1:
        refs[0][...] = x
    else:
        gn_ref, o_ref, on_ref = refs
        o_ref[...] = x
        on_ref[...] = _rms(x, gn_ref[...]).astype(on_ref.dtype)


def _add_norm(x, mix, g, g_next=None):
    m, d = x.shape
    tr = min(256, m)
    row = pl.BlockSpec((tr, d), lambda i: (i, 0))
    vec = pl.BlockSpec((1, d), lambda i: (0, 0))
    with_next = g_next is not None
    return pl.pallas_call(
        _add_norm_kernel,
        out_shape=((jax.ShapeDtypeStruct((m, d), F32), jax.ShapeDtypeStruct((m, d), BF16)) if with_next
                   else jax.ShapeDtypeStruct((m, d), F32)),
        grid=(m // tr,),
        in_specs=[row, row, vec] + ([vec] if with_next else []),
        out_specs=(row, row) if with_next else row,
        compiler_params=_params("arbitrary"),
        name="add_norm",
    )(x, mix, g.reshape(1, d), *([g_next.reshape(1, d)] if with_next else []))


def _attn_kernel(sink_ref, q_ref, *refs, n_pieces, n_sub, n_kv, band_mask):
    k_refs = refs[:n_pieces]
    v_refs = refs[n_pieces:2 * n_pieces]
    o_ref = refs[2 * n_pieces]
    tq = q_ref.shape[0] // n_sub
    n_band = n_pieces - n_sub + 1
    c = pl.program_id(1)
    scale = HEAD_DIM ** -0.5
    rows = N_GROUPS * tq
    scores = []
    for u in range(n_sub):
        for kv in range(n_kv):
            lo, hi = kv * HEAD_DIM, (kv + 1) * HEAD_DIM
            q4 = jnp.concatenate(
                [q_ref[u * tq:(u + 1) * tq, (kv * N_GROUPS + r) * HEAD_DIM:(kv * N_GROUPS + r + 1) * HEAD_DIM]
                 for r in range(N_GROUPS)], axis=0)
            kb = jnp.concatenate([kr[:, lo:hi].astype(BF16) for kr in k_refs[u:u + n_band]], axis=0)
            scores.append(lax.dot_general(q4, kb, (((1,), (1,)), ((), ())), preferred_element_type=F32))
    s = jnp.concatenate(scores, axis=0) * scale
    if band_mask:
        col = lax.broadcasted_iota(jnp.int32, s.shape, 1)
        sub = lax.broadcasted_iota(jnp.int32, s.shape, 0) // (n_kv * rows)
        s = jnp.where(col >= CHUNK * (n_band - 1 - n_sub * c - sub), s, -1e30)
    sink = jnp.concatenate([jnp.full((tq, 1), sink_ref[h], F32) for h in range(n_kv * N_GROUPS)] * n_sub, axis=0)
    mx = jnp.maximum(jnp.max(s, axis=-1, keepdims=True), sink)
    p = jnp.exp(s - mx)
    denom = jnp.sum(p, axis=-1, keepdims=True) + jnp.exp(sink - mx)
    pb = p.astype(BF16)
    inv = 1.0 / denom
    for u in range(n_sub):
        for kv in range(n_kv):
            vb = jnp.concatenate([vr[:, kv * HEAD_DIM:(kv + 1) * HEAD_DIM].astype(BF16)
                                  for vr in v_refs[u:u + n_band]], axis=0)
            blk = slice((u * n_kv + kv) * rows, (u * n_kv + kv + 1) * rows)
            o = jnp.dot(pb[blk], vb, preferred_element_type=F32) * inv[blk]
            for r in range(N_GROUPS):
                h = kv * N_GROUPS + r
                o_ref[u * tq:(u + 1) * tq, h * HEAD_DIM:(h + 1) * HEAD_DIM] = (
                    o[r * tq:(r + 1) * tq].astype(o_ref.dtype))


def _attn_prompt(qkv, sinks, b, s, d):
    kv_dim = d // N_GROUPS
    n_kv = kv_dim // HEAD_DIM
    nc = s // CHUNK
    n_band = 3
    n_sub = ATTN_CHUNKS_PER_STEP if nc % ATTN_CHUNKS_PER_STEP == 0 else 1
    n_pieces = n_band + n_sub - 1
    ns = nc // n_sub
    kcol, vcol = d // kv_dim, d // kv_dim + 1

    def piece(jj, col):
        return pl.BlockSpec((CHUNK, kv_dim),
                            lambda bi, c, sk: (bi * nc + jnp.maximum(n_sub * c - (n_band - 1) + jj, 0), col))

    q_spec = pl.BlockSpec((n_sub * CHUNK, d), lambda bi, c, sk: (bi * ns + c, 0))
    in_specs = [q_spec] + [piece(jj, kcol) for jj in range(n_pieces)] + [piece(jj, vcol) for jj in range(n_pieces)]
    return pl.pallas_call(
        functools.partial(_attn_kernel, n_pieces=n_pieces, n_sub=n_sub, n_kv=n_kv, band_mask=True),
        out_shape=jax.ShapeDtypeStruct((b * s, d), BF16),
        grid_spec=pltpu.PrefetchScalarGridSpec(
            num_scalar_prefetch=1, grid=(b, ns), in_specs=in_specs, out_specs=q_spec),
        compiler_params=_params("arbitrary", "arbitrary"),
        name="attn_prompt",
    )(sinks, *([qkv] * (1 + 2 * n_pieces)))


def _attn_sample(qkv, cache_k, cache_v, sinks, b, t, d):
    kv_dim = d // N_GROUPS
    n_kv = kv_dim // HEAD_DIM
    win = cache_k.shape[1]
    kcol, vcol = d // kv_dim, d // kv_dim + 1
    in_specs = [
        pl.BlockSpec((t, d), lambda bi, c, sk: (bi, 0)),
        pl.BlockSpec((None, win, kv_dim), lambda bi, c, sk: (bi, 0, 0)),
        pl.BlockSpec((t, kv_dim), lambda bi, c, sk: (bi, kcol)),
        pl.BlockSpec((None, win, kv_dim), lambda bi, c, sk: (bi, 0, 0)),
        pl.BlockSpec((t, kv_dim), lambda bi, c, sk: (bi, vcol)),
    ]
    return pl.pallas_call(
        functools.partial(_attn_kernel, n_pieces=2, n_sub=1, n_kv=n_kv, band_mask=False),
        out_shape=jax.ShapeDtypeStruct((b * t, d), BF16),
        grid_spec=pltpu.PrefetchScalarGridSpec(
            num_scalar_prefetch=1, grid=(b, 1), in_specs=in_specs,
            out_specs=pl.BlockSpec((t, d), lambda bi, c, sk: (bi, 0))),
        compiler_params=_params("arbitrary", "arbitrary"),
        name="attn_sample",
    )(sinks, qkv, cache_k, qkv, cache_v, qkv)


def _pool_kernel(x_ref, halo_ref, g_ref, w_ref, sc_ref, o_ref, st_ref, *, halo_is_x, pos0):
    si = pl.program_id(1)
    ts, d = x_ref.shape
    gw = d // len(POOL_WINDOWS)
    g = g_ref[...]
    un = _rms(x_ref[...], g)
    if halo_is_x:
        halo = jnp.where(si > 0, _rms(halo_ref[...], g), 0.0)
    else:
        halo = halo_ref[...]
    full = jnp.concatenate([halo, un], axis=0)
    pos = pos0 + si * ts + lax.broadcasted_iota(jnp.int32, (ts, 1), 0)
    for gi, win in enumerate(POOL_WINDOWS):
        lo, hi = gi * gw, (gi + 1) * gw
        acc = full[:, lo:hi]
        span = 1
        while span < win:
            acc = acc + pltpu.roll(acc, span, 0)
            span *= 2
        cnt = jnp.minimum(win, pos + 1).astype(F32)
        dlt = (acc[HALO_ROWS:] / cnt - un[:, lo:hi]).astype(BF16)
        y = jnp.dot(dlt, w_ref[gi], preferred_element_type=F32)
        o_ref[:, lo:hi] = (y * sc_ref[:, lo:hi]).astype(o_ref.dtype)
    st_ref[...] = full[ts:]


def _pool(x, halo, g, w, scale, b, s, *, halo_is_x, pos0):
    d = x.shape[1]
    ts = min(256, s)
    ns = s // ts
    hb = ts // HALO_ROWS
    if halo_is_x:
        halo_spec = pl.BlockSpec((HALO_ROWS, d), lambda bi, si: (jnp.maximum((bi * ns + si) * hb - 1, 0), 0))
    else:
        halo_spec = pl.BlockSpec((None, HALO_ROWS, d), lambda bi, si: (bi, 0, 0))
    ng, gw = w.shape[0], w.shape[1]
    return pl.pallas_call(
        functools.partial(_pool_kernel, halo_is_x=halo_is_x, pos0=pos0),
        out_shape=(jax.ShapeDtypeStruct((b * s, d), BF16), jax.ShapeDtypeStruct((b, HALO_ROWS, d), F32)),
        grid=(b, ns),
        in_specs=[pl.BlockSpec((ts, d), lambda bi, si: (bi * ns + si, 0)),
                  halo_spec,
                  pl.BlockSpec((1, d), lambda bi, si: (0, 0)),
                  pl.BlockSpec((ng, gw, gw), lambda bi, si: (0, 0, 0)),
                  pl.BlockSpec((1, d), lambda bi, si: (0, 0))],
        out_specs=(pl.BlockSpec((ts, d), lambda bi, si: (bi * ns + si, 0)),
                   pl.BlockSpec((None, HALO_ROWS, d), lambda bi, si: (bi, 0, 0))),
        compiler_params=_params("arbitrary", "arbitrary"),
        name="pool",
    )(x, halo, g.reshape(1, d), w, scale.reshape(1, d))


_MIX_R, _MIX_W, _MIX_K, _MIX_V, _MIX_A, _MIX_G = range(6)


def _rw_mix_kernel(x_ref, halo_ref, g_ref, mix_ref, w1_ref, a1_ref, g1_ref,
                   xr_ref, xk_ref, xv_ref, hw_ref, ha_ref, hg_ref, st_ref, *, halo_is_x):
    si = pl.program_id(1)
    g = g_ref[...]
    un = _rms(x_ref[...], g)
    n_h = halo_ref.shape[0]
    if halo_is_x:
        prev_row = jnp.where(si > 0, _rms(halo_ref[n_h - 1:n_h, :], g), 0.0)
    else:
        prev_row = halo_ref[n_h - 1:n_h, :]
    row = lax.broadcasted_iota(jnp.int32, (un.shape[0], 1), 0)
    prev = jnp.where(row == 0, prev_row, pltpu.roll(un, 1, 0))
    xx = prev - un
    mixed = lambda jm: (un + xx * mix_ref[jm:jm + 1, :]).astype(BF16)
    xr_ref[...] = mixed(_MIX_R)
    xk_ref[...] = mixed(_MIX_K)
    xv_ref[...] = mixed(_MIX_V)
    narrow = lambda jm, w_ref: jnp.dot(mixed(jm), w_ref[...], preferred_element_type=F32)
    hw_ref[...] = jnp.tanh(narrow(_MIX_W, w1_ref)).astype(BF16)
    ha_ref[...] = narrow(_MIX_A, a1_ref).astype(BF16)
    hg_ref[...] = jax.nn.sigmoid(narrow(_MIX_G, g1_ref)).astype(BF16)
    st_ref[...] = un[un.shape[0] - st_ref.shape[0]:]


def _rw_mix(x, halo, g, mix, w1, a1, g1, b, s, *, halo_is_x):
    d = x.shape[1]
    ts = min(256, s)
    ns = s // ts
    hr = 8
    if halo_is_x:
        halo_spec = pl.BlockSpec((hr, d), lambda bi, si: (jnp.maximum((bi * ns + si) * (ts // hr) - 1, 0), 0))
    else:
        halo_spec = pl.BlockSpec((None, hr, d), lambda bi, si: (bi, 0, 0))
    row_spec = pl.BlockSpec((ts, d), lambda bi, si: (bi * ns + si, 0))
    lows = [w1, a1, g1]
    ranks = [w[0].shape[2] for w in lows]
    return pl.pallas_call(
        functools.partial(_rw_mix_kernel, halo_is_x=halo_is_x),
        out_shape=tuple([jax.ShapeDtypeStruct((b * s, d), BF16)] * 3
                        + [jax.ShapeDtypeStruct((b * s, r), BF16) for r in ranks]
                        + [jax.ShapeDtypeStruct((b, hr, d), F32)]),
        grid=(b, ns),
        in_specs=[row_spec, halo_spec,
                  pl.BlockSpec((1, d), lambda bi, si: (0, 0)),
                  pl.BlockSpec((8, d), lambda bi, si: (0, 0))]
                 + [pl.BlockSpec((None, d, r), functools.partial(lambda bi, si, l: (l, 0, 0), l=w[1]))
                    for w, r in zip(lows, ranks)],
        out_specs=tuple([row_spec] * 3
                        + [pl.BlockSpec((ts, r), lambda bi, si: (bi * ns + si, 0)) for r in ranks]
                        + [pl.BlockSpec((None, hr, d), lambda bi, si: (bi, 0, 0))]),
        compiler_params=_params("arbitrary", "arbitrary"),
        name="rw_mix",
    )(x, halo, g.reshape(1, d), jnp.pad(mix, ((0, 2), (0, 0))), *[w[0] for w in lows])


def _dot_nt(a, b):
    return lax.dot_general(a, b, (((1,), (1,)), ((), ())), preferred_element_type=F32)


def _pair_dots(lhs, rhs, nt=False):
    single = (lambda a, b: _dot_nt(a, b)) if nt else (lambda a, b: jnp.dot(a, b, preferred_element_type=F32))
    n_out = rhs[0].shape[0] if nt else rhs[0].shape[1]
    if lhs[0].shape[1] % LANES or n_out % LANES:
        return [single(a, b) for a, b in zip(lhs, rhs)]
    outs = []
    for g in range(0, len(lhs) - 1, 2):
        zero = jnp.zeros_like(rhs[g])
        both = jnp.concatenate([jnp.concatenate([rhs[g], zero], axis=1),
                                jnp.concatenate([zero, rhs[g + 1]], axis=1)], axis=0)
        o = single(jnp.concatenate([lhs[g], lhs[g + 1]], axis=1), both)
        outs += [o[:, :n_out], o[:, n_out:]]
    if len(lhs) % 2:
        outs.append(single(lhs[-1], rhs[-1]))
    return outs


def _rw_chunk(r, ld, k, v, kk, kka, zs):
    c, width = r.shape
    n = RW_HEAD
    tiles = [slice(g * LANES, (g + 1) * LANES) for g in range(width // LANES)]
    lanes = lambda parts: jnp.concatenate(parts, axis=1) if len(parts) > 1 else parts[0]
    row = lax.broadcasted_iota(jnp.int32, (c, 2 * c), 0)
    col = lax.broadcasted_iota(jnp.int32, (c, 2 * c), 1)
    s_idx = col % c
    first = col < c
    strict = s_idx < row
    incl = s_idx <= row
    head0 = lax.broadcasted_iota(jnp.int32, (c, width), 1) % LANES < n

    tri = (lax.broadcasted_iota(jnp.int32, (c, 3 * c), 1) % c <= lax.broadcasted_iota(jnp.int32, (c, 3 * c), 0))
    hi = ld.astype(BF16)
    mid = (ld - hi.astype(F32)).astype(BF16)
    lo = (ld - hi.astype(F32) - mid.astype(F32)).astype(BF16)
    cum = jnp.dot(tri.astype(BF16), jnp.concatenate([hi, mid, lo], axis=0), preferred_element_type=F32)

    e_cum = jnp.exp(cum)
    e_inv = jnp.exp(-cum)
    at = -kk * jnp.exp(cum - ld)
    rt = r * e_cum
    bt = kka * e_inv
    kt = k * e_inv
    w_last = e_cum[c - 1:c, :]

    zero = jnp.zeros_like(at)
    a0r0 = jnp.concatenate([jnp.where(head0, at, zero), jnp.where(head0, rt, zero)], axis=0).astype(BF16)
    a1 = jnp.where(head0, zero, at).astype(BF16)
    r1 = jnp.where(head0, zero, rt).astype(BF16)
    ar = jnp.concatenate([at, rt], axis=0).astype(BF16)
    bk = jnp.concatenate([bt, kt], axis=0).astype(BF16)
    kb = jnp.concatenate([kt, bt], axis=0).astype(BF16)
    vb = v.astype(BF16)
    vv = jnp.concatenate([vb, vb], axis=0)

    per = lambda a: [a[:, t] for t in tiles]
    g0 = _pair_dots(per(a0r0), per(bk), nt=True)
    g1a = _pair_dots(per(a1), per(kb), nt=True)
    g1r = _pair_dots(per(r1), per(bk), nt=True)
    g0a, g0r = [g[:c] for g in g0], [g[c:] for g in g0]
    l2 = [jnp.where(strict, jnp.where(first, a, b), 0.0) for a, b in zip(g0a, g1a)]
    mak = [jnp.concatenate([jnp.where(strict & ~first, a, 0.0), jnp.where(strict & first, b, 0.0)],
                           axis=0).astype(BF16) for a, b in zip(g0a, g1a)]
    mv = lanes(_pair_dots(mak, per(vv)))
    p = lanes(_pair_dots(per(ar), [z.astype(BF16) for z in zs]))
    u0 = p[:c] + jnp.where(head0, mv[:c], mv[c:])

    sb = SOLVE_BLOCK
    nb = c // sb
    spread = (lax.broadcasted_iota(jnp.int32, (2 * c, LANES), 0) // c
              == lax.broadcasted_iota(jnp.int32, (2 * c, LANES), 1) // n).astype(BF16)
    lane_b = lax.broadcasted_iota(jnp.int32, (sb, 2 * c), 1) % c
    first_b = lax.broadcasted_iota(jnp.int32, (sb, 2 * c), 1) < c
    head0_b = lax.broadcasted_iota(jnp.int32, (sb, width), 1) % LANES < n
    cols = [jnp.concatenate([jnp.where(lane_b == blk * sb + s, l2g[blk * sb:(blk + 1) * sb], 0.0)
                             for blk in range(nb) for s in range(sb - 1)], axis=0).astype(BF16) for l2g in l2]
    lcol = lanes(_pair_dots(cols, [spread] * len(cols)))
    solved = []
    for blk in range(nb):
        ub = u0[blk * sb:(blk + 1) * sb]
        if blk > 0:
            done = jnp.concatenate(solved + [jnp.zeros((c - blk * sb, width), F32)], axis=0).astype(BF16)
            done = jnp.concatenate([done, done], axis=0)
            bands = [l2g[blk * sb:(blk + 1) * sb] for l2g in l2]
            lhs = [jnp.concatenate([jnp.where(first_b, band, 0.0), jnp.where(first_b, 0.0, band)],
                                   axis=0).astype(BF16) for band in bands]
            off = lanes(_pair_dots(lhs, per(done)))
            ub = ub + jnp.where(head0_b, off[:sb], off[sb:])
        for s in range(sb - 1):
            i = blk * (sb - 1) + s
            ub = ub + lcol[i * sb:(i + 1) * sb] * ub[s:s + 1, :]
        solved.append(ub)
    u = jnp.concatenate(solved, axis=0)

    uv = jnp.concatenate([u, v], axis=0).astype(BF16)
    bkw = jnp.concatenate([bt * w_last, kt * w_last], axis=0)
    same_head = (lax.broadcasted_iota(jnp.int32, (LANES, LANES), 0) // n
                 == lax.broadcasted_iota(jnp.int32, (LANES, LANES), 1) // n)
    nmat = [jnp.concatenate([jnp.where(incl, a, 0.0), jnp.where(incl, b, 0.0)], axis=0).astype(BF16)
            for a, b in zip(g0r, g1r)]
    ny = lanes(_pair_dots(nmat, per(uv)))
    upd_z = _pair_dots([bkw[:, t].T.astype(BF16) for t in tiles], per(uv))
    z_new = [z * jnp.broadcast_to(w_last[:, t], (LANES, LANES)).T + jnp.where(same_head, dz, 0.0)
             for t, z, dz in zip(tiles, zs, upd_z)]
    return p[c:] + jnp.where(head0, ny[:c], ny[c:]), z_new


def _seg_sum_wide(p, jj):
    parts = [_seg_sum(p[:, g * LANES:(g + 1) * LANES], jj) for g in range(p.shape[1] // LANES)]
    return jnp.concatenate(parts, axis=1) if len(parts) > 1 else parts[0]


_PV_W0, _PV_A0, _PV_KK, _PV_KA, _PV_RK, _PV_LN_G, _PV_LN_B = range(7)


def _rw_scan_kernel(r_ref, k_ref, v_ref, hw_ref, ha_ref, hg_ref, w2_ref, a2_ref, g2_ref, pv_ref, s0_ref,
                    o_ref, sT_ref, z_ref):
    @pl.when(pl.program_id(2) == 0)
    def _():
        z_ref[...] = s0_ref[...]

    pv = lambda i: pv_ref[i:i + 1, :]
    wide = lambda h_ref, w_ref: jnp.dot(h_ref[...], w_ref[...], preferred_element_type=F32)
    jj = _seg_ones(2 * LANES)
    r = r_ref[...].astype(F32)
    k = k_ref[...].astype(F32)
    v = v_ref[...].astype(F32)
    a = jax.nn.sigmoid(pv(_PV_A0) + wide(ha_ref, a2_ref))
    ld = LOG_DECAY_MIN * jax.nn.sigmoid(pv(_PV_W0) + wide(hw_ref, w2_ref))
    k2 = k * (1.0 + (a - 1.0) * pv(_PV_KA))
    kkr = k * pv(_PV_KK)
    kk = kkr / jnp.maximum(jnp.sqrt(_seg_sum_wide(kkr * kkr, jj)), 1e-12)

    tg = z_ref.shape[0]
    y, zs = _rw_chunk(r, ld, k2, v, kk, kk * a, [z_ref[g] for g in range(tg)])
    for g in range(tg):
        z_ref[g] = zs[g]
    sT_ref[...] = z_ref[...]

    yc = y - _seg_sum_wide(y, jj) * (1.0 / RW_HEAD)
    var = _seg_sum_wide(yc * yc, jj) * (1.0 / RW_HEAD)
    o = yc * lax.rsqrt(var + LNX_EPS) * pv(_PV_LN_G) + pv(_PV_LN_B)
    bonus = _seg_sum_wide(r * k2 * pv(_PV_RK), jj) * v
    o_ref[...] = ((o + bonus) * wide(hg_ref, g2_ref)).astype(o_ref.dtype)


def _rw_scan(r, k, v, narrow, second, pvec, s0, b, s):
    d = r.shape[1]
    ntile = d // LANES
    tg = min(RW_TILES_PER_STEP, ntile)
    tc = min(RW_CHUNK, s)
    nch = s // tc
    seq = pl.BlockSpec((tc, tg * LANES), lambda bi, gi, ci: (bi * nch + ci, gi))
    st = pl.BlockSpec((None, tg, LANES, LANES), lambda bi, gi, ci: (bi, gi, 0, 0))
    ranks = [h.shape[1] for h in narrow]
    return pl.pallas_call(
        _rw_scan_kernel,
        out_shape=(jax.ShapeDtypeStruct((b * s, d), BF16), jax.ShapeDtypeStruct(s0.shape, F32)),
        grid=(b, ntile // tg, nch),
        in_specs=[seq] * 3
                 + [pl.BlockSpec((tc, rk), lambda bi, gi, ci: (bi * nch + ci, 0)) for rk in ranks]
                 + [pl.BlockSpec((None, rk, tg * LANES), functools.partial(lambda bi, gi, ci, l: (l, 0, gi), l=w[1]))
                    for w, rk in zip(second, ranks)]
                 + [pl.BlockSpec((8, tg * LANES), lambda bi, gi, ci: (0, gi)), st],
        out_specs=(seq, st),
        scratch_shapes=[pltpu.VMEM((tg, LANES, LANES), F32)],
        compiler_params=_params("arbitrary", "arbitrary", "arbitrary"),
        name="rw_scan",
    )(r, k, v, *narrow, *[w[0] for w in second], pvec, s0)


def _gelu_tanh(x):
    return 0.5 * x * (1.0 + jnp.tanh(math.sqrt(2.0 / math.pi) * (x + 0.044715 * (x * x * x))))


def _conv_taps(full, cw, cb):
    return cb + full * cw[2:3] + pltpu.roll(full, 1, 0) * cw[1:2] + pltpu.roll(full, 2, 0) * cw[0:1]


def _conv_act_kernel(u_ref, h_ref, cw_ref, cb_ref, o_ref, *, tf):
    f = o_ref.shape[1]
    for j in range(f // tf):
        def conv(cols):
            full = jnp.concatenate([h_ref[:, cols], u_ref[:, cols].astype(F32)], axis=0)
            return _conv_taps(full, cw_ref[:, cols], cb_ref[:, cols])[HALO_ROWS:]
        gate = conv(slice(j * tf, (j + 1) * tf))
        val = conv(slice(f + j * tf, f + (j + 1) * tf))
        o_ref[:, j * tf:(j + 1) * tf] = (_gelu_tanh(gate) * val).astype(o_ref.dtype)


def _conv_act(up, halo, cw, cb, layer, b, s, tf):
    f = up.shape[1] // 2
    return pl.pallas_call(
        functools.partial(_conv_act_kernel, tf=tf),
        out_shape=jax.ShapeDtypeStruct((b * s, f), BF16),
        grid=(b,),
        in_specs=[pl.BlockSpec((s, 2 * f), lambda bi: (bi, 0)),
                  pl.BlockSpec((None, HALO_ROWS, 2 * f), lambda bi: (bi, 0, 0)),
                  pl.BlockSpec((None, 8, 2 * f), lambda bi: (layer, 0, 0)),
                  pl.BlockSpec((None, 1, 2 * f), lambda bi: (layer, 0, 0))],
        out_specs=pl.BlockSpec((s, f), lambda bi: (bi, 0)),
        compiler_params=_params("arbitrary"),
        name="conv_act",
    )(up, halo, cw, cb)


def _ffn_up_kernel(x_ref, wg_ref, wv_ref, cwg_ref, cwv_ref, cbg_ref, cbv_ref, *refs, rows, cast_next):
    if cast_next:
        nu_ref, nd_ref, h_ref, tg_ref, tv_ref, nu_o, nd_o, up_ref = refs
        nu_o[...] = nu_ref[...].astype(nu_o.dtype)
        nd_o[...] = nd_ref[...].astype(nd_o.dtype)
    else:
        h_ref, tg_ref, tv_ref, up_ref = refs
    s = x_ref.shape[0]
    tf = h_ref.shape[1]
    n_pass = s // rows

    def project(k):
        xk = x_ref[k * rows:(k + 1) * rows, :]
        up_ref[k % 2, 0, 8:, :] = jnp.dot(xk, wg_ref[...], preferred_element_type=F32)
        up_ref[k % 2, 1, 8:, :] = jnp.dot(xk, wv_ref[...], preferred_element_type=F32)

    def conv(slot, part, cw_ref, cb_ref):
        buf = up_ref.at[slot, part]
        return (cb_ref[...] + buf[8:8 + rows, :] * cw_ref[2:3, :] + buf[7:7 + rows, :] * cw_ref[1:2, :]
                + buf[6:6 + rows, :] * cw_ref[0:1, :])

    zero = jnp.zeros((8, tf), F32)
    up_ref[0, 0, 0:8, :] = zero
    up_ref[0, 1, 0:8, :] = zero
    project(0)
    for k in range(n_pass):
        slot = k % 2
        if k + 1 < n_pass:
            project(k + 1)
            up_ref[1 - slot, 0, 0:8, :] = up_ref[slot, 0, rows:rows + 8, :]
            up_ref[1 - slot, 1, 0:8, :] = up_ref[slot, 1, rows:rows + 8, :]
        gate = conv(slot, 0, cwg_ref, cbg_ref)
        val = conv(slot, 1, cwv_ref, cbv_ref)
        h_ref[k * rows:(k + 1) * rows, :] = (_gelu_tanh(gate) * val).astype(h_ref.dtype)
    last = (n_pass - 1) % 2
    tg_ref[...] = up_ref[last, 0, rows:rows + 8, :]
    tv_ref[...] = up_ref[last, 1, rows:rows + 8, :]


def _ffn_up(xn, w, cw, cb, layer, b, s, cast_next=None):
    d = xn.shape[1]
    f = w.shape[2] // 2
    tf = _pick(f, (FFN_COLS, LANES))
    nf = f // tf
    rows = min(FFN_ROWS, s)

    def cols(block_rows, off, lead):
        return pl.BlockSpec((None, block_rows, tf), lambda bi, j: (lead, 0, off + j))

    tail = pl.BlockSpec((None, 8, tf), lambda bi, j: (bi, 0, j))
    in_specs = [pl.BlockSpec((s, d), lambda bi, j: (bi, 0)), cols(d, 0, 0), cols(d, nf, 0),
                cols(8, 0, layer), cols(8, nf, layer), cols(1, 0, layer), cols(1, nf, layer)]
    args = [xn, w, w, cw, cw, cb, cb]
    out_shape = [jax.ShapeDtypeStruct((b * s, f), BF16), jax.ShapeDtypeStruct((b, 8, f), F32),
                 jax.ShapeDtypeStruct((b, 8, f), F32)]
    out_specs = [pl.BlockSpec((s, tf), lambda bi, j: (bi, j)), tail, tail]
    if cast_next is not None:
        wu32, wd32, nxt = cast_next
        assert d % b == 0 and (d // b) % HALO_ROWS == 0 and (2 * tf) % LANES == 0
        for stack, blk, imap in ((wu32, (None, d // b, 2 * tf), lambda bi, j: (nxt, bi, j)),
                                 (wd32, (None, tf, d // b), lambda bi, j: (nxt, j, bi))):
            in_specs.append(pl.BlockSpec(blk, imap))
            args.append(stack)
            out_shape.append(jax.ShapeDtypeStruct((1,) + stack.shape[1:], BF16))
            out_specs.append(pl.BlockSpec(blk, functools.partial(lambda bi, j, m: (0,) + m(bi, j)[1:], m=imap)))
    res = pl.pallas_call(
        functools.partial(_ffn_up_kernel, rows=rows, cast_next=cast_next is not None),
        out_shape=tuple(out_shape),
        grid=(b, nf),
        in_specs=in_specs,
        out_specs=tuple(out_specs),
        scratch_shapes=[pltpu.VMEM((2, 2, 8 + rows, tf), F32)],
        compiler_params=_params("arbitrary", "arbitrary"),
        name="ffn_up",
    )(*args)
    return (res[0], jnp.concatenate([res[1], res[2]], axis=2)) + tuple(res[3:])


def _rope_tables(pos):
    half = HEAD_DIM // 2
    inv = ROPE_THETA ** (-jnp.arange(half, dtype=F32) / half)
    ang = pos.astype(F32)[:, None] * inv[None, :]
    cos, sin = jnp.cos(ang), jnp.sin(ang)
    return jnp.concatenate([cos, cos], axis=1), jnp.concatenate([-sin, sin], axis=1)


def _state_to_tiles(wkv):
    b, h, n, _ = wkv.shape
    t = wkv.reshape(b, h // 2, 2, n, n).transpose(0, 1, 2, 4, 3)
    z = t[:, :, :, :, None, :] * jnp.eye(2, dtype=wkv.dtype)[None, None, :, None, :, None]
    return z.reshape(b, h // 2, 2 * n, 2 * n)


def _tiles_to_state(z, h):
    b = z.shape[0]
    n = RW_HEAD
    z = z.reshape(b, h // 2, 2, n, 2, n)
    t = jnp.stack([z[:, :, 0, :, 0, :], z[:, :, 1, :, 1, :]], axis=2)
    return t.transpose(0, 1, 2, 4, 3).reshape(b, h, n, n)


def _attn_layer(xn, b, s, w_qkv, w_o, sinks, pos, keep, cache=None):
    d = xn.shape[1]
    kv_dim = d // N_GROUPS
    cos, sin = _rope_tables(pos)
    cos, sin = jnp.tile(cos, (b, 1)), jnp.tile(sin, (b, 1))
    qkv = _mm(xn, w_qkv, BF16, rope=(cos, sin, d + kv_dim))
    if cache is None:
        o = _attn_prompt(qkv, sinks, b, s, d)
    else:
        o = _attn_sample(qkv, cache[0].reshape(b, -1, kv_dim), cache[1].reshape(b, -1, kv_dim), sinks, b, s, d)
    kv = qkv.reshape(b, s, d + 2 * kv_dim)[:, s - keep:, d:].astype(F32)
    k3 = kv[:, :, :kv_dim].reshape(b, keep, kv_dim // HEAD_DIM, HEAD_DIM)
    v3 = kv[:, :, kv_dim:].reshape(b, keep, kv_dim // HEAD_DIM, HEAD_DIM)
    return _mm(o, w_o, BF16), k3, v3


def _rwkv_layer(x, b, s, g, shift, wkv, p):
    d = x.shape[1]
    heads = d // RW_HEAD
    first = (p["w1"], p["a1"], p["g1"])
    if shift is None:
        mixed = _rw_mix(x, x, g, p["mix"], *first, b, s, halo_is_x=True)
        s0 = jnp.zeros((b, heads // 2, 2 * RW_HEAD, 2 * RW_HEAD), F32)
    else:
        halo = jnp.pad(shift.reshape(b, 1, d), ((0, 0), (7, 0), (0, 0)))
        mixed = _rw_mix(x, halo, g, p["mix"], *first, b, s, halo_is_x=False)
        s0 = _state_to_tiles(wkv.astype(F32))
    xr, xk, xv, hw, ha, hg, tail = mixed
    r = _mm(xr, p["w_r"], BF16)
    k = _mm(xk, p["w_k"], BF16)
    v = _mm(xv, p["w_v"], BF16)
    pvec = jnp.stack([p["w0"], p["a0"], p["k_k"], p["k_a"], p["r_k"].reshape(-1), p["lnx"][0], p["lnx"][1],
                      jnp.zeros_like(p["w0"])])
    o, s_t = _rw_scan(r, k, v, (hw, ha, hg), (p["w2"], p["a2"], p["g2"]), pvec, s0, b, s)
    return _mm(o, p["w_o"], BF16), tail[:, 7:8, :], _tiles_to_state(s_t, heads)


def _ffn_layer(xn, b, s, hist, w_up, w_down, cw, cb, layer, cast_next=None):
    f = w_down.shape[1]
    nxt = ()
    if hist is None:
        h, tail, *nxt = _ffn_up(xn, w_up, cw, cb, layer, b, s, cast_next)
        tail = tail[:, 8 - (CONV_W - 1):, :]
    else:
        up = _mm(xn, (w_up, 0), BF16)
        halo = jnp.pad(hist, ((0, 0), (HALO_ROWS - (CONV_W - 1), 0), (0, 0)))
        h = _conv_act(up, halo, cw, cb, layer, b, s, _pick(f, (FFN_COLS, LANES)))
        tail = up.reshape(b, s, 2 * f)[:, s - (CONV_W - 1):, :].astype(F32)
    return _mm(h, (w_down, 0), BF16), tail, tuple(nxt)


def kernel(x_prompt, x_sample, cache_k, cache_v, state_pool, state_shift, state_wkv, state_conv, norm_g, attn_w_qkv, attn_w_o, attn_sinks, pool_w, pool_scale, rw_mix, rw_w_r, rw_w_k, rw_w_v, rw_w_o, rw_w0, rw_w1, rw_w2, rw_a0, rw_a1, rw_a2, rw_g1, rw_g2, rw_k_k, rw_k_a, rw_r_k, rw_lnx, ffn_w_up, ffn_conv_w, ffn_conv_b, ffn_w_down):
    bp, sp, d = x_prompt.shape
    bs, ss, _ = x_sample.shape
    depth = norm_g.shape[0]
    window = cache_k.shape[2]
    xp = x_prompt.reshape(bp * sp, d)
    xs = x_sample.reshape(bs * ss, d)
    outs = {n: [] for n in ("kp", "vp", "kn", "vn", "poolp", "pools", "shp", "shs", "wkvp", "wkvs", "convp", "convs")}
    xnp = xns = None
    bf = lambda a: a.astype(BF16)
    attn_w_qkv, attn_w_o = bf(attn_w_qkv), bf(attn_w_o)
    w_up, w_down = bf(ffn_w_up[:1]), bf(ffn_w_down[:1])
    rw_w = {n: bf(a) for n, a in dict(w_r=rw_w_r, w_k=rw_w_k, w_v=rw_w_v, w_o=rw_w_o, w1=rw_w1, w2=rw_w2,
                                      a1=rw_a1, a2=rw_a2, g1=rw_g1, g2=rw_g2).items()}
    ffn_cw = jnp.pad(ffn_conv_w, ((0, 0), (0, 8 - CONV_W), (0, 0)))
    ffn_cb = ffn_conv_b[:, None, :]
    for i in range(depth):
        kind, j = i % 3, i // 3
        g = norm_g[i]
        g_after = norm_g[i + 1, 0] if i + 1 < depth else None
        if kind == 0:
            wq, wo = (attn_w_qkv, j), (attn_w_o, j)
            if xnp is None:
                xnp, xns = _norm(xp, g[0]), _norm(xs, g[0])
            mp, kp, vp = _attn_layer(xnp, bp, sp, wq, wo, attn_sinks[j], jnp.arange(sp), window)
            ms, kn, vn = _attn_layer(xns, bs, ss, wq, wo, attn_sinks[j], PAST_LEN + jnp.arange(ss), ss,
                                     cache=(cache_k[j], cache_v[j]))
            outs["kp"].append(kp)
            outs["vp"].append(vp)
            outs["kn"].append(kn)
            outs["vn"].append(vn)
        elif kind == 1:
            wp = pool_w[j].astype(BF16)
            mp, hp = _pool(xp, xp, g[0], wp, pool_scale[j], bp, sp, halo_is_x=True, pos0=0)
            halo = jnp.pad(state_pool[j], ((0, 0), (HALO_ROWS - POOL_HIST, 0), (0, 0)))
            ms, hs = _pool(xs, halo, g[0], wp, pool_scale[j], bs, ss, halo_is_x=False, pos0=PAST_LEN)
            outs["poolp"].append(hp[:, HALO_ROWS - POOL_HIST:])
            outs["pools"].append(hs[:, HALO_ROWS - POOL_HIST:])
        else:
            p = dict(mix=rw_mix[j], w0=rw_w0[j], a0=rw_a0[j], k_k=rw_k_k[j], k_a=rw_k_a[j], r_k=rw_r_k[j],
                     lnx=rw_lnx[j], **{n: (a, j) for n, a in rw_w.items()})
            mp, shp, wkvp = _rwkv_layer(xp, bp, sp, g[0], None, None, p)
            ms, shs, wkvs = _rwkv_layer(xs, bs, ss, g[0], state_shift[j], state_wkv[j], p)
            outs["shp"].append(shp)
            outs["shs"].append(shs)
            outs["wkvp"].append(wkvp)
            outs["wkvs"].append(wkvs)
        xp, xnp = _add_norm(xp, mp, g[1], g[2])
        xs, xns = _add_norm(xs, ms, g[1], g[2])
        cast_next = (ffn_w_up, ffn_w_down, i + 1) if i + 1 < depth else None
        fp, cp, w_next = _ffn_layer(xnp, bp, sp, None, w_up, w_down, ffn_cw, ffn_cb, i, cast_next)
        fs, cs, _ = _ffn_layer(xns, bs, ss, state_conv[i], w_up, w_down, ffn_cw, ffn_cb, i)
        if w_next:
            w_up, w_down = w_next
        outs["convp"].append(cp)
        outs["convs"].append(cs)
        if g_after is not None and (i + 1) % 3 == 0:
            xp, xnp = _add_norm(xp, fp, g[3], g_after)
            xs, xns = _add_norm(xs, fs, g[3], g_after)
        else:
            xp, xnp = _add_norm(xp, fp, g[3]), None
            xs, xns = _add_norm(xs, fs, g[3]), None
    st = lambda n: jnp.stack(outs[n])
    return (xp.reshape(bp, sp, d), xs.reshape(bs, ss, d),
            st("kp"), st("vp"), st("poolp"), st("shp"), st("wkvp"), st("convp"),
            st("kn"), st("vn"), st("pools"), st("shs"), st("wkvs"), st("convs"))
```

```python
import functools
import math

import jax
import jax.numpy as jnp
from jax import lax
from jax.experimental import pallas as pl
from jax.experimental.pallas import tpu as pltpu

BF16 = jnp.bfloat16
F32 = jnp.float32

CHUNK = 64
HEAD_DIM = 128
N_GROUPS = 4
ROPE_THETA = 10000.0
POOL_WINDOWS = (2, 4, 8, 16)
POOL_HIST = 15
RW_HEAD = 64
LNX_EPS = 64e-5
NORM_EPS = 1e-6
PAST_LEN = 2048
CONV_W = 3
LOG_DECAY_MIN = -math.exp(-0.5)
RW_CHUNK = 64
RW_TILES_PER_STEP = 32
SOLVE_BLOCK = 16
ATTN_CHUNKS_PER_STEP = 4
MM_ROWS = 1024
MM_DEEP_K = 8192
MM_ROWS_DEEP_K = 512
FFN_COLS = 256
FFN_ROWS = 512

LANES = 128
VMEM_LIMIT_BYTES = 56 * 1024 * 1024
HALO_ROWS = 16


def _params(*sem):
    return pltpu.CompilerParams(dimension_semantics=sem, vmem_limit_bytes=VMEM_LIMIT_BYTES)


def _pick(n, prefs):
    for p in prefs:
        if n % p == 0:
            return p
    return n


def _rms(xf, g):
    return xf * lax.rsqrt(jnp.mean(xf * xf, axis=-1, keepdims=True) + NORM_EPS) * g


def _seg_ones(rows):
    r = lax.broadcasted_iota(jnp.int32, (rows, LANES), 0)
    c = lax.broadcasted_iota(jnp.int32, (rows, LANES), 1)
    return ((r % LANES) // RW_HEAD == c // RW_HEAD).astype(BF16)


def _seg_sum(p, jj):
    hi = p.astype(BF16)
    lo = (p - hi.astype(F32)).astype(BF16)
    return jnp.dot(jnp.concatenate([hi, lo], axis=1), jj, preferred_element_type=F32)


def _mm_kernel(x_ref, w_ref, *refs, rope_blocks):
    o_ref = refs[-1]
    j = pl.program_id(1)
    acc = jnp.dot(x_ref[...], w_ref[...], preferred_element_type=F32)

    if rope_blocks:
        @pl.when(j < rope_blocks)
        def _():
            cos = refs[0][...]
            sin = refs[1][...]
            for h in range(acc.shape[1] // HEAD_DIM):
                xh = acc[:, h * HEAD_DIM:(h + 1) * HEAD_DIM]
                o_ref[:, h * HEAD_DIM:(h + 1) * HEAD_DIM] = (
                    xh * cos + pltpu.roll(xh, HEAD_DIM // 2, 1) * sin).astype(o_ref.dtype)

        @pl.when(j >= rope_blocks)
        def _():
            o_ref[...] = acc.astype(o_ref.dtype)
    else:
        o_ref[...] = acc.astype(o_ref.dtype)


def _mm(x, w, out_dtype, *, rope=None):
    w, layer = w
    m, k = x.shape
    n = w.shape[2]
    tm = min(MM_ROWS_DEEP_K if k > MM_DEEP_K else MM_ROWS, m)
    tn = _pick(math.gcd(n, rope[2]) if rope else n, (512, 256, 128))
    rope_blocks = 0
    in_specs = [pl.BlockSpec((tm, k), lambda i, j: (i, 0)),
                pl.BlockSpec((None, k, tn), lambda i, j: (layer, 0, j))]
    args = [x, w]
    if rope is not None:
        cos, sin, n_cols = rope
        assert n_cols % tn == 0 and tn % HEAD_DIM == 0
        rope_blocks = n_cols // tn
        in_specs += [pl.BlockSpec((tm, HEAD_DIM), lambda i, j: (i, 0))] * 2
        args += [cos, sin]
    return pl.pallas_call(
        functools.partial(_mm_kernel, rope_blocks=rope_blocks),
        out_shape=jax.ShapeDtypeStruct((m, n), out_dtype),
        grid=(m // tm, n // tn),
        in_specs=in_specs,
        out_specs=pl.BlockSpec((tm, tn), lambda i, j: (i, j)),
        compiler_params=_params("arbitrary", "arbitrary"),
        name="mm",
    )(*args)


def _norm_kernel(x_ref, g_ref, o_ref):
    o_ref[...] = _rms(x_ref[...], g_ref[...]).astype(o_ref.dtype)


def _norm(x, g):
    m, d = x.shape
    tr = min(256, m)
    return pl.pallas_call(
        _norm_kernel,
        out_shape=jax.ShapeDtypeStruct((m, d), BF16),
        grid=(m // tr,),
        in_specs=[pl.BlockSpec((tr, d), lambda i: (i, 0)), pl.BlockSpec((1, d), lambda i: (0, 0))],
        out_specs=pl.BlockSpec((tr, d), lambda i: (i, 0)),
        compiler_params=_params("arbitrary"),
        name="norm",
    )(x, g.reshape(1, d))


def _add_norm_kernel(x_ref, m_ref, g_ref, *refs):
    x = x_ref[...] + _rms(m_ref[...].astype(F32), g_ref[...])
    if len(refs) == 1:
        refs[0][...] = x
    else:
        gn_ref, o_ref, on_ref = refs
        o_ref[...] = x
        on_ref[...] = _rms(x, gn_ref[...]).astype(on_ref.dtype)


def _add_norm(x, mix, g, g_next=None):
    m, d = x.shape
    tr = min(256, m)
    row = pl.BlockSpec((tr, d), lambda i: (i, 0))
    vec = pl.BlockSpec((1, d), lambda i: (0, 0))
    with_next = g_next is not None
    return pl.pallas_call(
        _add_norm_kernel,
        out_shape=((jax.ShapeDtypeStruct((m, d), F32), jax.ShapeDtypeStruct((m, d), BF16)) if with_next
                   else jax.ShapeDtypeStruct((m, d), F32)),
        grid=(m // tr,),
        in_specs=[row, row, vec] + ([vec] if with_next else []),
        out_specs=(row, row) if with_next else row,
        compiler_params=_params("arbitrary"),
        name="add_norm",
    )(x, mix, g.reshape(1, d), *([g_next.reshape(1, d)] if with_next else []))


def _attn_kernel(sink_ref, q_ref, *refs, n_pieces, n_sub, n_kv, band_mask):
    k_refs = refs[:n_pieces]
    v_refs = refs[n_pieces:2 * n_pieces]
    o_ref = refs[2 * n_pieces]
    tq = q_ref.shape[0] // n_sub
    n_band = n_pieces - n_sub + 1
    c = pl.program_id(1)
    scale = HEAD_DIM ** -0.5
    rows = N_GROUPS * tq
    scores = []
    for u in range(n_sub):
        for kv in range(n_kv):
            lo, hi = kv * HEAD_DIM, (kv + 1) * HEAD_DIM
            q4 = jnp.concatenate(
                [q_ref[u * tq:(u + 1) * tq, (kv * N_GROUPS + r) * HEAD_DIM:(kv * N_GROUPS + r + 1) * HEAD_DIM]
                 for r in range(N_GROUPS)], axis=0)
            kb = jnp.concatenate([kr[:, lo:hi].astype(BF16) for kr in k_refs[u:u + n_band]], axis=0)
            scores.append(lax.dot_general(q4, kb, (((1,), (1,)), ((), ())), preferred_element_type=F32))
    s = jnp.concatenate(scores, axis=0)
    if band_mask:
        need = min(n_sub, n_band - 1) * n_kv * rows
        col = lax.broadcasted_iota(jnp.int32, (need, s.shape[1]), 1)
        sub = lax.broadcasted_iota(jnp.int32, (need, s.shape[1]), 0) // (n_kv * rows)
        head = jnp.where(col >= CHUNK * (n_band - 1 - n_sub * c - sub), s[:need], -1e30)
        s = head if need == s.shape[0] else jnp.concatenate([head, s[need:]], axis=0)
    sink = jnp.concatenate([jnp.full((tq, 1), sink_ref[h], F32) for h in range(n_kv * N_GROUPS)] * n_sub, axis=0)
    sink_u = sink * (1.0 / scale)
    mx = jnp.maximum(jnp.max(s, axis=-1, keepdims=True), sink_u)
    p = jnp.exp2((s - mx) * (scale * math.log2(math.e)))
    denom = jnp.sum(p, axis=-1, keepdims=True) + jnp.exp2((sink_u - mx) * (scale * math.log2(math.e)))
    pb = p.astype(BF16)
    inv = 1.0 / denom
    for u in range(n_sub):
        for kv in range(n_kv):
            vb = jnp.concatenate([vr[:, kv * HEAD_DIM:(kv + 1) * HEAD_DIM].astype(BF16)
                                  for vr in v_refs[u:u + n_band]], axis=0)
            blk = slice((u * n_kv + kv) * rows, (u * n_kv + kv + 1) * rows)
            o = jnp.dot(pb[blk], vb, preferred_element_type=F32) * inv[blk]
            for r in range(N_GROUPS):
                h = kv * N_GROUPS + r
                o_ref[u * tq:(u + 1) * tq, h * HEAD_DIM:(h + 1) * HEAD_DIM] = (
                    o[r * tq:(r + 1) * tq].astype(o_ref.dtype))


def _attn_prompt(qkv, sinks, b, s, d):
    kv_dim = d // N_GROUPS
    n_kv = kv_dim // HEAD_DIM
    nc = s // CHUNK
    n_band = 3
    n_sub = ATTN_CHUNKS_PER_STEP if nc % ATTN_CHUNKS_PER_STEP == 0 else 1
    n_pieces = n_band + n_sub - 1
    ns = nc // n_sub
    kcol, vcol = d // kv_dim, d // kv_dim + 1

    def piece(jj, col):
        return pl.BlockSpec((CHUNK, kv_dim),
                            lambda bi, c, sk: (bi * nc + jnp.maximum(n_sub * c - (n_band - 1) + jj, 0), col))

    q_spec = pl.BlockSpec((n_sub * CHUNK, d), lambda bi, c, sk: (bi * ns + c, 0))
    in_specs = [q_spec] + [piece(jj, kcol) for jj in range(n_pieces)] + [piece(jj, vcol) for jj in range(n_pieces)]
    return pl.pallas_call(
        functools.partial(_attn_kernel, n_pieces=n_pieces, n_sub=n_sub, n_kv=n_kv, band_mask=True),
        out_shape=jax.ShapeDtypeStruct((b * s, d), BF16),
        grid_spec=pltpu.PrefetchScalarGridSpec(
            num_scalar_prefetch=1, grid=(b, ns), in_specs=in_specs, out_specs=q_spec),
        compiler_params=_params("arbitrary", "arbitrary"),
        name="attn_prompt",
    )(sinks, *([qkv] * (1 + 2 * n_pieces)))


def _attn_sample(qkv, cache_k, cache_v, sinks, b, t, d):
    kv_dim = d // N_GROUPS
    n_kv = kv_dim // HEAD_DIM
    win = cache_k.shape[1]
    kcol, vcol = d // kv_dim, d // kv_dim + 1
    in_specs = [
        pl.BlockSpec((t, d), lambda bi, c, sk: (bi, 0)),
        pl.BlockSpec((None, win, kv_dim), lambda bi, c, sk: (bi, 0, 0)),
        pl.BlockSpec((t, kv_dim), lambda bi, c, sk: (bi, kcol)),
        pl.BlockSpec((None, win, kv_dim), lambda bi, c, sk: (bi, 0, 0)),
        pl.BlockSpec((t, kv_dim), lambda bi, c, sk: (bi, vcol)),
    ]
    return pl.pallas_call(
        functools.partial(_attn_kernel, n_pieces=2, n_sub=1, n_kv=n_kv, band_mask=False),
        out_shape=jax.ShapeDtypeStruct((b * t, d), BF16),
        grid_spec=pltpu.PrefetchScalarGridSpec(
            num_scalar_prefetch=1, grid=(b, 1), in_specs=in_specs,
            out_specs=pl.BlockSpec((t, d), lambda bi, c, sk: (bi, 0))),
        compiler_params=_params("arbitrary", "arbitrary"),
        name="attn_sample",
    )(sinks, qkv, cache_k, qkv, cache_v, qkv)


def _pool_kernel(x_ref, halo_ref, g_ref, w_ref, sc_ref, o_ref, st_ref, *, halo_is_x, pos0):
    si = pl.program_id(1)
    ts, d = x_ref.shape
    gw = d // len(POOL_WINDOWS)
    g = g_ref[...]
    un = _rms(x_ref[...], g)
    if halo_is_x:
        halo = jnp.where(si > 0, _rms(halo_ref[...], g), 0.0)
    else:
        halo = halo_ref[...]
    full = jnp.concatenate([halo, un], axis=0)
    pos = pos0 + si * ts + lax.broadcasted_iota(jnp.int32, (ts, 1), 0)
    for gi, win in enumerate(POOL_WINDOWS):
        lo, hi = gi * gw, (gi + 1) * gw
        acc = full[:, lo:hi]
        span = 1
        while span < win:
            acc = acc + pltpu.roll(acc, span, 0)
            span *= 2
        cnt = jnp.minimum(win, pos + 1).astype(F32)
        dlt = (acc[HALO_ROWS:] / cnt - un[:, lo:hi]).astype(BF16)
        y = jnp.dot(dlt, w_ref[gi], preferred_element_type=F32)
        o_ref[:, lo:hi] = (y * sc_ref[:, lo:hi]).astype(o_ref.dtype)
    st_ref[...] = full[ts:]


def _pool(x, halo, g, w, scale, b, s, *, halo_is_x, pos0):
    d = x.shape[1]
    ts = min(256, s)
    ns = s // ts
    hb = ts // HALO_ROWS
    if halo_is_x:
        halo_spec = pl.BlockSpec((HALO_ROWS, d), lambda bi, si: (jnp.maximum((bi * ns + si) * hb - 1, 0), 0))
    else:
        halo_spec = pl.BlockSpec((None, HALO_ROWS, d), lambda bi, si: (bi, 0, 0))
    ng, gw = w.shape[0], w.shape[1]
    return pl.pallas_call(
        functools.partial(_pool_kernel, halo_is_x=halo_is_x, pos0=pos0),
        out_shape=(jax.ShapeDtypeStruct((b * s, d), BF16), jax.ShapeDtypeStruct((b, HALO_ROWS, d), F32)),
        grid=(b, ns),
        in_specs=[pl.BlockSpec((ts, d), lambda bi, si: (bi * ns + si, 0)),
                  halo_spec,
                  pl.BlockSpec((1, d), lambda bi, si: (0, 0)),
                  pl.BlockSpec((ng, gw, gw), lambda bi, si: (0, 0, 0)),
                  pl.BlockSpec((1, d), lambda bi, si: (0, 0))],
        out_specs=(pl.BlockSpec((ts, d), lambda bi, si: (bi * ns + si, 0)),
                   pl.BlockSpec((None, HALO_ROWS, d), lambda bi, si: (bi, 0, 0))),
        compiler_params=_params("arbitrary", "arbitrary"),
        name="pool",
    )(x, halo, g.reshape(1, d), w, scale.reshape(1, d))


_MIX_R, _MIX_W, _MIX_K, _MIX_V, _MIX_A, _MIX_G = range(6)


def _rw_mix_kernel(x_ref, halo_ref, g_ref, mix_ref, w1_ref, a1_ref, g1_ref,
                   xr_ref, xk_ref, xv_ref, hw_ref, ha_ref, hg_ref, st_ref, *, halo_is_x):
    si = pl.program_id(1)
    g = g_ref[...]
    un = _rms(x_ref[...], g)
    n_h = halo_ref.shape[0]
    if halo_is_x:
        prev_row = jnp.where(si > 0, _rms(halo_ref[n_h - 1:n_h, :], g), 0.0)
    else:
        prev_row = halo_ref[n_h - 1:n_h, :]
    row = lax.broadcasted_iota(jnp.int32, (un.shape[0], 1), 0)
    prev = jnp.where(row == 0, prev_row, pltpu.roll(un, 1, 0))
    xx = prev - un
    mixed = lambda jm: (un + xx * mix_ref[jm:jm + 1, :]).astype(BF16)
    xr_ref[...] = mixed(_MIX_R)
    xk_ref[...] = mixed(_MIX_K)
    xv_ref[...] = mixed(_MIX_V)
    narrow = lambda jm, w_ref: jnp.dot(mixed(jm), w_ref[...], preferred_element_type=F32)
    hw_ref[...] = jnp.tanh(narrow(_MIX_W, w1_ref)).astype(BF16)
    ha_ref[...] = narrow(_MIX_A, a1_ref).astype(BF16)
    hg_ref[...] = jax.nn.sigmoid(narrow(_MIX_G, g1_ref)).astype(BF16)
    st_ref[...] = un[un.shape[0] - st_ref.shape[0]:]


def _rw_mix(x, halo, g, mix, w1, a1, g1, b, s, *, halo_is_x):
    d = x.shape[1]
    ts = min(256, s)
    ns = s // ts
    hr = 8
    if halo_is_x:
        halo_spec = pl.BlockSpec((hr, d), lambda bi, si: (jnp.maximum((bi * ns + si) * (ts // hr) - 1, 0), 0))
    else:
        halo_spec = pl.BlockSpec((None, hr, d), lambda bi, si: (bi, 0, 0))
    row_spec = pl.BlockSpec((ts, d), lambda bi, si: (bi * ns + si, 0))
    lows = [w1, a1, g1]
    ranks = [w[0].shape[2] for w in lows]
    return pl.pallas_call(
        functools.partial(_rw_mix_kernel, halo_is_x=halo_is_x),
        out_shape=tuple([jax.ShapeDtypeStruct((b * s, d), BF16)] * 3
                        + [jax.ShapeDtypeStruct((b * s, r), BF16) for r in ranks]
                        + [jax.ShapeDtypeStruct((b, hr, d), F32)]),
        grid=(b, ns),
        in_specs=[row_spec, halo_spec,
                  pl.BlockSpec((1, d), lambda bi, si: (0, 0)),
                  pl.BlockSpec((8, d), lambda bi, si: (0, 0))]
                 + [pl.BlockSpec((None, d, r), functools.partial(lambda bi, si, l: (l, 0, 0), l=w[1]))
                    for w, r in zip(lows, ranks)],
        out_specs=tuple([row_spec] * 3
                        + [pl.BlockSpec((ts, r), lambda bi, si: (bi * ns + si, 0)) for r in ranks]
                        + [pl.BlockSpec((None, hr, d), lambda bi, si: (bi, 0, 0))]),
        compiler_params=_params("arbitrary", "arbitrary"),
        name="rw_mix",
    )(x, halo, g.reshape(1, d), jnp.pad(mix, ((0, 2), (0, 0))), *[w[0] for w in lows])


def _dot_nt(a, b):
    return lax.dot_general(a, b, (((1,), (1,)), ((), ())), preferred_element_type=F32)


def _pair_dots(lhs, rhs, nt=False):
    single = (lambda a, b: _dot_nt(a, b)) if nt else (lambda a, b: jnp.dot(a, b, preferred_element_type=F32))
    n_out = rhs[0].shape[0] if nt else rhs[0].shape[1]
    if lhs[0].shape[1] % LANES or n_out % LANES:
        return [single(a, b) for a, b in zip(lhs, rhs)]
    outs = []
    for g in range(0, len(lhs) - 1, 2):
        zero = jnp.zeros_like(rhs[g])
        both = jnp.concatenate([jnp.concatenate([rhs[g], zero], axis=1),
                                jnp.concatenate([zero, rhs[g + 1]], axis=1)], axis=0)
        o = single(jnp.concatenate([lhs[g], lhs[g + 1]], axis=1), both)
        outs += [o[:, :n_out], o[:, n_out:]]
    if len(lhs) % 2:
        outs.append(single(lhs[-1], rhs[-1]))
    return outs


def _rw_chunk(r, ld, k, v, kk, kka, zs):
    c, width = r.shape
    n = RW_HEAD
    tiles = [slice(g * LANES, (g + 1) * LANES) for g in range(width // LANES)]
    lanes = lambda parts: jnp.concatenate(parts, axis=1) if len(parts) > 1 else parts[0]
    row = lax.broadcasted_iota(jnp.int32, (c, 2 * c), 0)
    col = lax.broadcasted_iota(jnp.int32, (c, 2 * c), 1)
    s_idx = col % c
    first = col < c
    strict = s_idx < row
    incl = s_idx <= row
    head0 = lax.broadcasted_iota(jnp.int32, (c, width), 1) % LANES < n

    tri = (lax.broadcasted_iota(jnp.int32, (c, 3 * c), 1) % c <= lax.broadcasted_iota(jnp.int32, (c, 3 * c), 0))
    hi = ld.astype(BF16)
    mid = (ld - hi.astype(F32)).astype(BF16)
    lo = (ld - hi.astype(F32) - mid.astype(F32)).astype(BF16)
    cum = jnp.dot(tri.astype(BF16), jnp.concatenate([hi, mid, lo], axis=0), preferred_element_type=F32)

    e_cum = jnp.exp(cum)
    e_inv = jnp.exp(-cum)
    at = -kk * jnp.exp(cum - ld)
    rt = r * e_cum
    bt = kka * e_inv
    kt = k * e_inv
    w_last = e_cum[c - 1:c, :]

    zero = jnp.zeros_like(at)
    a0r0 = jnp.concatenate([jnp.where(head0, at, zero), jnp.where(head0, rt, zero)], axis=0).astype(BF16)
    a1 = jnp.where(head0, zero, at).astype(BF16)
    r1 = jnp.where(head0, zero, rt).astype(BF16)
    ar = jnp.concatenate([at, rt], axis=0).astype(BF16)
    bk = jnp.concatenate([bt, kt], axis=0).astype(BF16)
    kb = jnp.concatenate([kt, bt], axis=0).astype(BF16)
    vb = v.astype(BF16)
    vv = jnp.concatenate([vb, vb], axis=0)

    per = lambda a: [a[:, t] for t in tiles]
    g0 = _pair_dots(per(a0r0), per(bk), nt=True)
    g1a = _pair_dots(per(a1), per(kb), nt=True)
    g1r = _pair_dots(per(r1), per(bk), nt=True)
    g0a, g0r = [g[:c] for g in g0], [g[c:] for g in g0]
    l2 = [jnp.where(strict, jnp.where(first, a, b), 0.0) for a, b in zip(g0a, g1a)]
    mak = [jnp.concatenate([jnp.where(strict & ~first, a, 0.0), jnp.where(strict & first, b, 0.0)],
                           axis=0).astype(BF16) for a, b in zip(g0a, g1a)]
    mv = lanes(_pair_dots(mak, per(vv)))
    p = lanes(_pair_dots(per(ar), [z.astype(BF16) for z in zs]))
    u0 = p[:c] + jnp.where(head0, mv[:c], mv[c:])

    sb = SOLVE_BLOCK
    nb = c // sb
    spread = (lax.broadcasted_iota(jnp.int32, (2 * c, LANES), 0) // c
              == lax.broadcasted_iota(jnp.int32, (2 * c, LANES), 1) // n).astype(BF16)
    lane_b = lax.broadcasted_iota(jnp.int32, (sb, 2 * c), 1) % c
    first_b = lax.broadcasted_iota(jnp.int32, (sb, 2 * c), 1) < c
    head0_b = lax.broadcasted_iota(jnp.int32, (sb, width), 1) % LANES < n
    cols = [jnp.concatenate([jnp.where(lane_b == blk * sb + s, l2g[blk * sb:(blk + 1) * sb], 0.0)
                             for blk in range(nb) for s in range(sb - 1)], axis=0).astype(BF16) for l2g in l2]
    lcol = lanes(_pair_dots(cols, [spread] * len(cols)))
    solved = []
    for blk in range(nb):
        ub = u0[blk * sb:(blk + 1) * sb]
        if blk > 0:
            done = jnp.concatenate(solved + [jnp.zeros((c - blk * sb, width), F32)], axis=0).astype(BF16)
            done = jnp.concatenate([done, done], axis=0)
            bands = [l2g[blk * sb:(blk + 1) * sb] for l2g in l2]
            lhs = [jnp.concatenate([jnp.where(first_b, band, 0.0), jnp.where(first_b, 0.0, band)],
                                   axis=0).astype(BF16) for band in bands]
            off = lanes(_pair_dots(lhs, per(done)))
            ub = ub + jnp.where(head0_b, off[:sb], off[sb:])
        for s in range(sb - 1):
            i = blk * (sb - 1) + s
            ub = ub + lcol[i * sb:(i + 1) * sb] * ub[s:s + 1, :]
        solved.append(ub)
    u = jnp.concatenate(solved, axis=0)

    uv = jnp.concatenate([u, v], axis=0).astype(BF16)
    bkw = jnp.concatenate([bt * w_last, kt * w_last], axis=0)
    same_head = (lax.broadcasted_iota(jnp.int32, (LANES, LANES), 0) // n
                 == lax.broadcasted_iota(jnp.int32, (LANES, LANES), 1) // n)
    nmat = [jnp.concatenate([jnp.where(incl, a, 0.0), jnp.where(incl, b, 0.0)], axis=0).astype(BF16)
            for a, b in zip(g0r, g1r)]
    ny = lanes(_pair_dots(nmat, per(uv)))
    upd_z = _pair_dots([bkw[:, t].T.astype(BF16) for t in tiles], per(uv))
    z_new = [z * jnp.broadcast_to(w_last[:, t], (LANES, LANES)).T + jnp.where(same_head, dz, 0.0)
             for t, z, dz in zip(tiles, zs, upd_z)]
    return p[c:] + jnp.where(head0, ny[:c], ny[c:]), z_new


def _seg_sum_wide(p, jj):
    parts = [_seg_sum(p[:, g * LANES:(g + 1) * LANES], jj) for g in range(p.shape[1] // LANES)]
    return jnp.concatenate(parts, axis=1) if len(parts) > 1 else parts[0]


_PV_W0, _PV_A0, _PV_KK, _PV_KA, _PV_RK, _PV_LN_G, _PV_LN_B = range(7)


def _rw_scan_kernel(r_ref, k_ref, v_ref, hw_ref, ha_ref, hg_ref, w2_ref, a2_ref, g2_ref, pv_ref, s0_ref,
                    o_ref, sT_ref, z_ref):
    @pl.when(pl.program_id(2) == 0)
    def _():
        z_ref[...] = s0_ref[...]

    pv = lambda i: pv_ref[i:i + 1, :]
    wide = lambda h_ref, w_ref: jnp.dot(h_ref[...], w_ref[...], preferred_element_type=F32)
    jj = _seg_ones(2 * LANES)
    r = r_ref[...].astype(F32)
    k = k_ref[...].astype(F32)
    v = v_ref[...].astype(F32)
    a = jax.nn.sigmoid(pv(_PV_A0) + wide(ha_ref, a2_ref))
    ld = LOG_DECAY_MIN * jax.nn.sigmoid(pv(_PV_W0) + wide(hw_ref, w2_ref))
    k2 = k * (1.0 + (a - 1.0) * pv(_PV_KA))
    kkr = k * pv(_PV_KK)
    kk = kkr / jnp.maximum(jnp.sqrt(_seg_sum_wide(kkr * kkr, jj)), 1e-12)

    tg = z_ref.shape[0]
    y, zs = _rw_chunk(r, ld, k2, v, kk, kk * a, [z_ref[g] for g in range(tg)])
    for g in range(tg):
        z_ref[g] = zs[g]
    sT_ref[...] = z_ref[...]

    yc = y - _seg_sum_wide(y, jj) * (1.0 / RW_HEAD)
    var = _seg_sum_wide(yc * yc, jj) * (1.0 / RW_HEAD)
    o = yc * lax.rsqrt(var + LNX_EPS) * pv(_PV_LN_G) + pv(_PV_LN_B)
    bonus = _seg_sum_wide(r * k2 * pv(_PV_RK), jj) * v
    o_ref[...] = ((o + bonus) * wide(hg_ref, g2_ref)).astype(o_ref.dtype)


def _rw_scan(r, k, v, narrow, second, pvec, s0, b, s):
    d = r.shape[1]
    ntile = d // LANES
    tg = min(RW_TILES_PER_STEP, ntile)
    tc = min(RW_CHUNK, s)
    nch = s // tc
    seq = pl.BlockSpec((tc, tg * LANES), lambda bi, gi, ci: (bi * nch + ci, gi))
    st = pl.BlockSpec((None, tg, LANES, LANES), lambda bi, gi, ci: (bi, gi, 0, 0))
    ranks = [h.shape[1] for h in narrow]
    return pl.pallas_call(
        _rw_scan_kernel,
        out_shape=(jax.ShapeDtypeStruct((b * s, d), BF16), jax.ShapeDtypeStruct(s0.shape, F32)),
        grid=(b, ntile // tg, nch),
        in_specs=[seq] * 3
                 + [pl.BlockSpec((tc, rk), lambda bi, gi, ci: (bi * nch + ci, 0)) for rk in ranks]
                 + [pl.BlockSpec((None, rk, tg * LANES), functools.partial(lambda bi, gi, ci, l: (l, 0, gi), l=w[1]))
                    for w, rk in zip(second, ranks)]
                 + [pl.BlockSpec((8, tg * LANES), lambda bi, gi, ci: (0, gi)), st],
        out_specs=(seq, st),
        scratch_shapes=[pltpu.VMEM((tg, LANES, LANES), F32)],
        compiler_params=_params("arbitrary", "arbitrary", "arbitrary"),
        name="rw_scan",
    )(r, k, v, *narrow, *[w[0] for w in second], pvec, s0)


def _gelu_tanh(x):
    return 0.5 * x * (1.0 + jnp.tanh(math.sqrt(2.0 / math.pi) * (x + 0.044715 * (x * x * x))))


def _conv_taps(full, cw, cb):
    return cb + full * cw[2:3] + pltpu.roll(full, 1, 0) * cw[1:2] + pltpu.roll(full, 2, 0) * cw[0:1]


def _conv_act_kernel(u_ref, h_ref, cw_ref, cb_ref, o_ref, *, tf):
    f = o_ref.shape[1]
    for j in range(f // tf):
        def conv(cols):
            full = jnp.concatenate([h_ref[:, cols], u_ref[:, cols].astype(F32)], axis=0)
            return _conv_taps(full, cw_ref[:, cols], cb_ref[:, cols])[HALO_ROWS:]
        gate = conv(slice(j * tf, (j + 1) * tf))
        val = conv(slice(f + j * tf, f + (j + 1) * tf))
        o_ref[:, j * tf:(j + 1) * tf] = (_gelu_tanh(gate) * val).astype(o_ref.dtype)


def _conv_act(up, halo, cw, cb, layer, b, s, tf):
    f = up.shape[1] // 2
    return pl.pallas_call(
        functools.partial(_conv_act_kernel, tf=tf),
        out_shape=jax.ShapeDtypeStruct((b * s, f), BF16),
        grid=(b,),
        in_specs=[pl.BlockSpec((s, 2 * f), lambda bi: (bi, 0)),
                  pl.BlockSpec((None, HALO_ROWS, 2 * f), lambda bi: (bi, 0, 0)),
                  pl.BlockSpec((None, 8, 2 * f), lambda bi: (layer, 0, 0)),
                  pl.BlockSpec((None, 1, 2 * f), lambda bi: (layer, 0, 0))],
        out_specs=pl.BlockSpec((s, f), lambda bi: (bi, 0)),
        compiler_params=_params("arbitrary"),
        name="conv_act",
    )(up, halo, cw, cb)


def _ffn_up_kernel(x_ref, wg_ref, wv_ref, cwg_ref, cwv_ref, cbg_ref, cbv_ref, *refs, rows, cast_next):
    if cast_next:
        nu_ref, nd_ref, h_ref, tg_ref, tv_ref, nu_o, nd_o, up_ref = refs
        nu_o[...] = nu_ref[...].astype(nu_o.dtype)
        nd_o[...] = nd_ref[...].astype(nd_o.dtype)
    else:
        h_ref, tg_ref, tv_ref, up_ref = refs
    s = x_ref.shape[0]
    tf = h_ref.shape[1]
    n_pass = s // rows

    def project(k):
        xk = x_ref[k * rows:(k + 1) * rows, :]
        up_ref[k % 2, 0, 8:, :] = jnp.dot(xk, wg_ref[...], preferred_element_type=F32)
        up_ref[k % 2, 1, 8:, :] = jnp.dot(xk, wv_ref[...], preferred_element_type=F32)

    def conv(slot, part, cw_ref, cb_ref):
        buf = up_ref.at[slot, part]
        return (cb_ref[...] + buf[8:8 + rows, :] * cw_ref[2:3, :] + buf[7:7 + rows, :] * cw_ref[1:2, :]
                + buf[6:6 + rows, :] * cw_ref[0:1, :])

    zero = jnp.zeros((8, tf), F32)
    up_ref[0, 0, 0:8, :] = zero
    up_ref[0, 1, 0:8, :] = zero
    project(0)
    for k in range(n_pass):
        slot = k % 2
        if k + 1 < n_pass:
            project(k + 1)
            up_ref[1 - slot, 0, 0:8, :] = up_ref[slot, 0, rows:rows + 8, :]
            up_ref[1 - slot, 1, 0:8, :] = up_ref[slot, 1, rows:rows + 8, :]
        gate = conv(slot, 0, cwg_ref, cbg_ref)
        val = conv(slot, 1, cwv_ref, cbv_ref)
        h_ref[k * rows:(k + 1) * rows, :] = (_gelu_tanh(gate) * val).astype(h_ref.dtype)
    last = (n_pass - 1) % 2
    tg_ref[...] = up_ref[last, 0, rows:rows + 8, :]
    tv_ref[...] = up_ref[last, 1, rows:rows + 8, :]


def _ffn_up(xn, w, cw, cb, layer, b, s, cast_next=None):
    d = xn.shape[1]
    f = w.shape[2] // 2
    tf = _pick(f, (FFN_COLS, LANES))
    nf = f // tf
    rows = min(FFN_ROWS, s)

    def cols(block_rows, off, lead):
        return pl.BlockSpec((None, block_rows, tf), lambda bi, j: (lead, 0, off + j))

    tail = pl.BlockSpec((None, 8, tf), lambda bi, j: (bi, 0, j))
    in_specs = [pl.BlockSpec((s, d), lambda bi, j: (bi, 0)), cols(d, 0, 0), cols(d, nf, 0),
                cols(8, 0, layer), cols(8, nf, layer), cols(1, 0, layer), cols(1, nf, layer)]
    args = [xn, w, w, cw, cw, cb, cb]
    out_shape = [jax.ShapeDtypeStruct((b * s, f), BF16), jax.ShapeDtypeStruct((b, 8, f), F32),
                 jax.ShapeDtypeStruct((b, 8, f), F32)]
    out_specs = [pl.BlockSpec((s, tf), lambda bi, j: (bi, j)), tail, tail]
    if cast_next is not None:
        wu32, wd32, nxt = cast_next
        assert d % b == 0 and (d // b) % HALO_ROWS == 0 and (2 * tf) % LANES == 0
        for stack, blk, imap in ((wu32, (None, d // b, 2 * tf), lambda bi, j: (nxt, bi, j)),
                                 (wd32, (None, tf, d // b), lambda bi, j: (nxt, j, bi))):
            in_specs.append(pl.BlockSpec(blk, imap))
            args.append(stack)
            out_shape.append(jax.ShapeDtypeStruct((1,) + stack.shape[1:], BF16))
            out_specs.append(pl.BlockSpec(blk, functools.partial(lambda bi, j, m: (0,) + m(bi, j)[1:], m=imap)))
    res = pl.pallas_call(
        functools.partial(_ffn_up_kernel, rows=rows, cast_next=cast_next is not None),
        out_shape=tuple(out_shape),
        grid=(b, nf),
        in_specs=in_specs,
        out_specs=tuple(out_specs),
        scratch_shapes=[pltpu.VMEM((2, 2, 8 + rows, tf), F32)],
        compiler_params=_params("arbitrary", "arbitrary"),
        name="ffn_up",
    )(*args)
    return (res[0], jnp.concatenate([res[1], res[2]], axis=2)) + tuple(res[3:])


def _rope_tables(pos):
    half = HEAD_DIM // 2
    inv = ROPE_THETA ** (-jnp.arange(half, dtype=F32) / half)
    ang = pos.astype(F32)[:, None] * inv[None, :]
    cos, sin = jnp.cos(ang), jnp.sin(ang)
    return jnp.concatenate([cos, cos], axis=1), jnp.concatenate([-sin, sin], axis=1)


def _state_to_tiles(wkv):
    b, h, n, _ = wkv.shape
    t = wkv.reshape(b, h // 2, 2, n, n).transpose(0, 1, 2, 4, 3)
    z = t[:, :, :, :, None, :] * jnp.eye(2, dtype=wkv.dtype)[None, None, :, None, :, None]
    return z.reshape(b, h // 2, 2 * n, 2 * n)


def _tiles_to_state(z, h):
    b = z.shape[0]
    n = RW_HEAD
    z = z.reshape(b, h // 2, 2, n, 2, n)
    t = jnp.stack([z[:, :, 0, :, 0, :], z[:, :, 1, :, 1, :]], axis=2)
    return t.transpose(0, 1, 2, 4, 3).reshape(b, h, n, n)


def _attn_layer(xn, b, s, w_qkv, w_o, sinks, pos, keep, cache=None):
    d = xn.shape[1]
    kv_dim = d // N_GROUPS
    cos, sin = _rope_tables(pos)
    cos, sin = jnp.tile(cos, (b, 1)), jnp.tile(sin, (b, 1))
    qkv = _mm(xn, w_qkv, BF16, rope=(cos, sin, d + kv_dim))
    if cache is None:
        o = _attn_prompt(qkv, sinks, b, s, d)
    else:
        o = _attn_sample(qkv, cache[0].reshape(b, -1, kv_dim), cache[1].reshape(b, -1, kv_dim), sinks, b, s, d)
    kv = qkv.reshape(b, s, d + 2 * kv_dim)[:, s - keep:, d:].astype(F32)
    k3 = kv[:, :, :kv_dim].reshape(b, keep, kv_dim // HEAD_DIM, HEAD_DIM)
    v3 = kv[:, :, kv_dim:].reshape(b, keep, kv_dim // HEAD_DIM, HEAD_DIM)
    return _mm(o, w_o, BF16), k3, v3


def _rwkv_layer(x, b, s, g, shift, wkv, p):
    d = x.shape[1]
    heads = d // RW_HEAD
    first = (p["w1"], p["a1"], p["g1"])
    if shift is None:
        mixed = _rw_mix(x, x, g, p["mix"], *first, b, s, halo_is_x=True)
        s0 = jnp.zeros((b, heads // 2, 2 * RW_HEAD, 2 * RW_HEAD), F32)
    else:
        halo = jnp.pad(shift.reshape(b, 1, d), ((0, 0), (7, 0), (0, 0)))
        mixed = _rw_mix(x, halo, g, p["mix"], *first, b, s, halo_is_x=False)
        s0 = _state_to_tiles(wkv.astype(F32))
    xr, xk, xv, hw, ha, hg, tail = mixed
    r = _mm(xr, p["w_r"], BF16)
    k = _mm(xk, p["w_k"], BF16)
    v = _mm(xv, p["w_v"], BF16)
    pvec = jnp.stack([p["w0"], p["a0"], p["k_k"], p["k_a"], p["r_k"].reshape(-1), p["lnx"][0], p["lnx"][1],
                      jnp.zeros_like(p["w0"])])
    o, s_t = _rw_scan(r, k, v, (hw, ha, hg), (p["w2"], p["a2"], p["g2"]), pvec, s0, b, s)
    return _mm(o, p["w_o"], BF16), tail[:, 7:8, :], _tiles_to_state(s_t, heads)


def _ffn_layer(xn, b, s, hist, w_up, w_down, cw, cb, layer, cast_next=None):
    f = w_down.shape[1]
    nxt = ()
    if hist is None:
        h, tail, *nxt = _ffn_up(xn, w_up, cw, cb, layer, b, s, cast_next)
        tail = tail[:, 8 - (CONV_W - 1):, :]
    else:
        up = _mm(xn, (w_up, 0), BF16)
        halo = jnp.pad(hist, ((0, 0), (HALO_ROWS - (CONV_W - 1), 0), (0, 0)))
        h = _conv_act(up, halo, cw, cb, layer, b, s, _pick(f, (FFN_COLS, LANES)))
        tail = up.reshape(b, s, 2 * f)[:, s - (CONV_W - 1):, :].astype(F32)
    return _mm(h, (w_down, 0), BF16), tail, tuple(nxt)


def kernel(x_prompt, x_sample, cache_k, cache_v, state_pool, state_shift, state_wkv, state_conv, norm_g, attn_w_qkv, attn_w_o, attn_sinks, pool_w, pool_scale, rw_mix, rw_w_r, rw_w_k, rw_w_v, rw_w_o, rw_w0, rw_w1, rw_w2, rw_a0, rw_a1, rw_a2, rw_g1, rw_g2, rw_k_k, rw_k_a, rw_r_k, rw_lnx, ffn_w_up, ffn_conv_w, ffn_conv_b, ffn_w_down):
    bp, sp, d = x_prompt.shape
    bs, ss, _ = x_sample.shape
    depth = norm_g.shape[0]
    window = cache_k.shape[2]
    xp = x_prompt.reshape(bp * sp, d)
    xs = x_sample.reshape(bs * ss, d)
    outs = {n: [] for n in ("kp", "vp", "kn", "vn", "poolp", "pools", "shp", "shs", "wkvp", "wkvs", "convp", "convs")}
    xnp = xns = None
    bf = lambda a: a.astype(BF16)
    attn_w_qkv, attn_w_o = bf(attn_w_qkv), bf(attn_w_o)
    w_up, w_down = bf(ffn_w_up[:1]), bf(ffn_w_down[:1])
    rw_w = {n: bf(a) for n, a in dict(w_r=rw_w_r, w_k=rw_w_k, w_v=rw_w_v, w_o=rw_w_o, w1=rw_w1, w2=rw_w2,
                                      a1=rw_a1, a2=rw_a2, g1=rw_g1, g2=rw_g2).items()}
    ffn_cw = jnp.pad(ffn_conv_w, ((0, 0), (0, 8 - CONV_W), (0, 0)))
    ffn_cb = ffn_conv_b[:, None, :]
    for i in range(depth):
        kind, j = i % 3, i // 3
        g = norm_g[i]
        g_after = norm_g[i + 1, 0] if i + 1 < depth else None
        if kind == 0:
            wq, wo = (attn_w_qkv, j), (attn_w_o, j)
            if xnp is None:
                xnp, xns = _norm(xp, g[0]), _norm(xs, g[0])
            mp, kp, vp = _attn_layer(xnp, bp, sp, wq, wo, attn_sinks[j], jnp.arange(sp), window)
            ms, kn, vn = _attn_layer(xns, bs, ss, wq, wo, attn_sinks[j], PAST_LEN + jnp.arange(ss), ss,
                                     cache=(cache_k[j], cache_v[j]))
            outs["kp"].append(kp)
            outs["vp"].append(vp)
            outs["kn"].append(kn)
            outs["vn"].append(vn)
        elif kind == 1:
            wp = pool_w[j].astype(BF16)
            mp, hp = _pool(xp, xp, g[0], wp, pool_scale[j], bp, sp, halo_is_x=True, pos0=0)
            halo = jnp.pad(state_pool[j], ((0, 0), (HALO_ROWS - POOL_HIST, 0), (0, 0)))
            ms, hs = _pool(xs, halo, g[0], wp, pool_scale[j], bs, ss, halo_is_x=False, pos0=PAST_LEN)
            outs["poolp"].append(hp[:, HALO_ROWS - POOL_HIST:])
            outs["pools"].append(hs[:, HALO_ROWS - POOL_HIST:])
        else:
            p = dict(mix=rw_mix[j], w0=rw_w0[j], a0=rw_a0[j], k_k=rw_k_k[j], k_a=rw_k_a[j], r_k=rw_r_k[j],
                     lnx=rw_lnx[j], **{n: (a, j) for n, a in rw_w.items()})
            mp, shp, wkvp = _rwkv_layer(xp, bp, sp, g[0], None, None, p)
            ms, shs, wkvs = _rwkv_layer(xs, bs, ss, g[0], state_shift[j], state_wkv[j], p)
            outs["shp"].append(shp)
            outs["shs"].append(shs)
            outs["wkvp"].append(wkvp)
            outs["wkvs"].append(wkvs)
        xp, xnp = _add_norm(xp, mp, g[1], g[2])
        xs, xns = _add_norm(xs, ms, g[1], g[2])
        cast_next = (ffn_w_up, ffn_w_down, i + 1) if i + 1 < depth else None
        fp, cp, w_next = _ffn_layer(xnp, bp, sp, None, w_up, w_down, ffn_cw, ffn_cb, i, cast_next)
        fs, cs, _ = _ffn_layer(xns, bs, ss, state_conv[i], w_up, w_down, ffn_cw, ffn_cb, i)
        if w_next:
            w_up, w_down = w_next
        outs["convp"].append(cp)
        outs["convs"].append(cs)
        if g_after is not None and (i + 1) % 3 == 0:
            xp, xnp = _add_norm(xp, fp, g[3], g_after)
            xs, xns = _add_norm(xs, fs, g[3], g_after)
        else:
            xp, xnp = _add_norm(xp, fp, g[3]), None
            xs, xns = _add_norm(xs, fs, g[3]), None
    st = lambda n: jnp.stack(outs[n])
    return (xp.reshape(bp, sp, d), xs.reshape(bs, ss, d),
            st("kp"), st("vp"), st("poolp"), st("shp"), st("wkvp"), st("convp"),
            st("kn"), st("vn"), st("pools"), st("shs"), st("wkvs"), st("convs"))
```

```python
import functools
import math

import jax
import jax.numpy as jnp
from jax import lax
from jax.experimental import pallas as pl
from jax.experimental.pallas import tpu as pltpu

BF16 = jnp.bfloat16
F32 = jnp.float32

CHUNK = 64
HEAD_DIM = 128
N_GROUPS = 4
ROPE_THETA = 10000.0
POOL_WINDOWS = (2, 4, 8, 16)
POOL_HIST = 15
RW_HEAD = 64
LNX_EPS = 64e-5
NORM_EPS = 1e-6
PAST_LEN = 2048
CONV_W = 3
LOG_DECAY_MIN = -math.exp(-0.5)
RW_CHUNK = 64
RW_TILES_PER_STEP = 32
SOLVE_BLOCK = 16
ATTN_CHUNKS_PER_STEP = 4
MM_ROWS = 1024
MM_DEEP_K = 8192
MM_ROWS_DEEP_K = 512
FFN_COLS = 256
FFN_ROWS = 512

LANES = 128
VMEM_LIMIT_BYTES = 56 * 1024 * 1024
HALO_ROWS = 16


def _params(*sem):
    return pltpu.CompilerParams(dimension_semantics=sem, vmem_limit_bytes=VMEM_LIMIT_BYTES)


def _pick(n, prefs):
    for p in prefs:
        if n % p == 0:
            return p
    return n


def _rms(xf, g):
    return xf * lax.rsqrt(jnp.mean(xf * xf, axis=-1, keepdims=True) + NORM_EPS) * g


def _seg_ones(rows):
    r = lax.broadcasted_iota(jnp.int32, (rows, LANES), 0)
    c = lax.broadcasted_iota(jnp.int32, (rows, LANES), 1)
    return ((r % LANES) // RW_HEAD == c // RW_HEAD).astype(BF16)


def _seg_sum(p, jj):
    hi = p.astype(BF16)
    lo = (p - hi.astype(F32)).astype(BF16)
    return jnp.dot(jnp.concatenate([hi, lo], axis=1), jj, preferred_element_type=F32)


def _mm_kernel(x_ref, w_ref, *refs, rope_blocks):
    o_ref = refs[-1]
    j = pl.program_id(1)
    acc = jnp.dot(x_ref[...], w_ref[...], preferred_element_type=F32)

    if rope_blocks:
        @pl.when(j < rope_blocks)
        def _():
            cos = refs[0][...]
            sin = refs[1][...]
            for h in range(acc.shape[1] // HEAD_DIM):
                xh = acc[:, h * HEAD_DIM:(h + 1) * HEAD_DIM]
                o_ref[:, h * HEAD_DIM:(h + 1) * HEAD_DIM] = (
                    xh * cos + pltpu.roll(xh, HEAD_DIM // 2, 1) * sin).astype(o_ref.dtype)

        @pl.when(j >= rope_blocks)
        def _():
            o_ref[...] = acc.astype(o_ref.dtype)
    else:
        o_ref[...] = acc.astype(o_ref.dtype)


def _mm(x, w, out_dtype, *, rope=None):
    w, layer = w
    m, k = x.shape
    n = w.shape[2]
    deep = k > MM_DEEP_K
    tm = min(MM_ROWS_DEEP_K if deep else MM_ROWS, m)
    tn = _pick(math.gcd(n, rope[2]) if rope else n, (512, 256, 128) if deep else (1024, 512, 256, 128))
    rope_blocks = 0
    in_specs = [pl.BlockSpec((tm, k), lambda i, j: (i, 0)),
                pl.BlockSpec((None, k, tn), lambda i, j: (layer, 0, j))]
    args = [x, w]
    if rope is not None:
        cos, sin, n_cols = rope
        assert n_cols % tn == 0 and tn % HEAD_DIM == 0
        rope_blocks = n_cols // tn
        in_specs += [pl.BlockSpec((tm, HEAD_DIM), lambda i, j: (i, 0))] * 2
        args += [cos, sin]
    return pl.pallas_call(
        functools.partial(_mm_kernel, rope_blocks=rope_blocks),
        out_shape=jax.ShapeDtypeStruct((m, n), out_dtype),
        grid=(m // tm, n // tn),
        in_specs=in_specs,
        out_specs=pl.BlockSpec((tm, tn), lambda i, j: (i, j)),
        compiler_params=_params("arbitrary", "arbitrary"),
        name="mm",
    )(*args)


def _norm_kernel(x_ref, g_ref, o_ref):
    o_ref[...] = _rms(x_ref[...], g_ref[...]).astype(o_ref.dtype)


def _norm(x, g):
    m, d = x.shape
    tr = min(256, m)
    return pl.pallas_call(
        _norm_kernel,
        out_shape=jax.ShapeDtypeStruct((m, d), BF16),
        grid=(m // tr,),
        in_specs=[pl.BlockSpec((tr, d), lambda i: (i, 0)), pl.BlockSpec((1, d), lambda i: (0, 0))],
        out_specs=pl.BlockSpec((tr, d), lambda i: (i, 0)),
        compiler_params=_params("arbitrary"),
        name="norm",
    )(x, g.reshape(1, d))


def _add_norm_kernel(x_ref, m_ref, g_ref, *refs):
    x = x_ref[...] + _rms(m_ref[...].astype(F32), g_ref[...])
    if len(refs) == 1:
        refs[0][...] = x
    else:
        gn_ref, o_ref, on_ref = refs
        o_ref[...] = x
        on_ref[...] = _rms(x, gn_ref[...]).astype(on_ref.dtype)


def _add_norm(x, mix, g, g_next=None):
    m, d = x.shape
    tr = min(256, m)
    row = pl.BlockSpec((tr, d), lambda i: (i, 0))
    vec = pl.BlockSpec((1, d), lambda i: (0, 0))
    with_next = g_next is not None
    return pl.pallas_call(
        _add_norm_kernel,
        out_shape=((jax.ShapeDtypeStruct((m, d), F32), jax.ShapeDtypeStruct((m, d), BF16)) if with_next
                   else jax.ShapeDtypeStruct((m, d), F32)),
        grid=(m // tr,),
        in_specs=[row, row, vec] + ([vec] if with_next else []),
        out_specs=(row, row) if with_next else row,
        compiler_params=_params("arbitrary"),
        name="add_norm",
    )(x, mix, g.reshape(1, d), *([g_next.reshape(1, d)] if with_next else []))


def _attn_kernel(sink_ref, q_ref, *refs, n_pieces, n_sub, n_kv, band_mask):
    k_refs = refs[:n_pieces]
    v_refs = refs[n_pieces:2 * n_pieces]
    o_ref = refs[2 * n_pieces]
    tq = q_ref.shape[0] // n_sub
    n_band = n_pieces - n_sub + 1
    c = pl.program_id(1)
    scale = HEAD_DIM ** -0.5
    rows = N_GROUPS * tq
    scores = []
    for u in range(n_sub):
        for kv in range(n_kv):
            lo, hi = kv * HEAD_DIM, (kv + 1) * HEAD_DIM
            q4 = jnp.concatenate(
                [q_ref[u * tq:(u + 1) * tq, (kv * N_GROUPS + r) * HEAD_DIM:(kv * N_GROUPS + r + 1) * HEAD_DIM]
                 for r in range(N_GROUPS)], axis=0)
            kb = jnp.concatenate([kr[:, lo:hi].astype(BF16) for kr in k_refs[u:u + n_band]], axis=0)
            scores.append(lax.dot_general(q4, kb, (((1,), (1,)), ((), ())), preferred_element_type=F32))
    s = jnp.concatenate(scores, axis=0)
    if band_mask:
        need = min(n_sub, n_band - 1) * n_kv * rows
        col = lax.broadcasted_iota(jnp.int32, (need, s.shape[1]), 1)
        sub = lax.broadcasted_iota(jnp.int32, (need, s.shape[1]), 0) // (n_kv * rows)
        head = jnp.where(col >= CHUNK * (n_band - 1 - n_sub * c - sub), s[:need], -1e30)
        s = head if need == s.shape[0] else jnp.concatenate([head, s[need:]], axis=0)
    sink = jnp.concatenate([jnp.full((tq, 1), sink_ref[h], F32) for h in range(n_kv * N_GROUPS)] * n_sub, axis=0)
    sink_u = sink * (1.0 / scale)
    mx = jnp.maximum(jnp.max(s, axis=-1, keepdims=True), sink_u)
    p = jnp.exp2((s - mx) * (scale * math.log2(math.e)))
    denom = jnp.sum(p, axis=-1, keepdims=True) + jnp.exp2((sink_u - mx) * (scale * math.log2(math.e)))
    pb = p.astype(BF16)
    inv = 1.0 / denom
    for u in range(n_sub):
        for kv in range(n_kv):
            vb = jnp.concatenate([vr[:, kv * HEAD_DIM:(kv + 1) * HEAD_DIM].astype(BF16)
                                  for vr in v_refs[u:u + n_band]], axis=0)
            blk = slice((u * n_kv + kv) * rows, (u * n_kv + kv + 1) * rows)
            o = jnp.dot(pb[blk], vb, preferred_element_type=F32) * inv[blk]
            for r in range(N_GROUPS):
                h = kv * N_GROUPS + r
                o_ref[u * tq:(u + 1) * tq, h * HEAD_DIM:(h + 1) * HEAD_DIM] = (
                    o[r * tq:(r + 1) * tq].astype(o_ref.dtype))


def _attn_prompt(qkv, sinks, b, s, d):
    kv_dim = d // N_GROUPS
    n_kv = kv_dim // HEAD_DIM
    nc = s // CHUNK
    n_band = 3
    n_sub = ATTN_CHUNKS_PER_STEP if nc % ATTN_CHUNKS_PER_STEP == 0 else 1
    n_pieces = n_band + n_sub - 1
    ns = nc // n_sub
    kcol, vcol = d // kv_dim, d // kv_dim + 1

    def piece(jj, col):
        return pl.BlockSpec((CHUNK, kv_dim),
                            lambda bi, c, sk: (bi * nc + jnp.maximum(n_sub * c - (n_band - 1) + jj, 0), col))

    q_spec = pl.BlockSpec((n_sub * CHUNK, d), lambda bi, c, sk: (bi * ns + c, 0))
    in_specs = [q_spec] + [piece(jj, kcol) for jj in range(n_pieces)] + [piece(jj, vcol) for jj in range(n_pieces)]
    return pl.pallas_call(
        functools.partial(_attn_kernel, n_pieces=n_pieces, n_sub=n_sub, n_kv=n_kv, band_mask=True),
        out_shape=jax.ShapeDtypeStruct((b * s, d), BF16),
        grid_spec=pltpu.PrefetchScalarGridSpec(
            num_scalar_prefetch=1, grid=(b, ns), in_specs=in_specs, out_specs=q_spec),
        compiler_params=_params("arbitrary", "arbitrary"),
        name="attn_prompt",
    )(sinks, *([qkv] * (1 + 2 * n_pieces)))


def _attn_sample(qkv, cache_k, cache_v, sinks, b, t, d):
    kv_dim = d // N_GROUPS
    n_kv = kv_dim // HEAD_DIM
    win = cache_k.shape[1]
    kcol, vcol = d // kv_dim, d // kv_dim + 1
    in_specs = [
        pl.BlockSpec((t, d), lambda bi, c, sk: (bi, 0)),
        pl.BlockSpec((None, win, kv_dim), lambda bi, c, sk: (bi, 0, 0)),
        pl.BlockSpec((t, kv_dim), lambda bi, c, sk: (bi, kcol)),
        pl.BlockSpec((None, win, kv_dim), lambda bi, c, sk: (bi, 0, 0)),
        pl.BlockSpec((t, kv_dim), lambda bi, c, sk: (bi, vcol)),
    ]
    return pl.pallas_call(
        functools.partial(_attn_kernel, n_pieces=2, n_sub=1, n_kv=n_kv, band_mask=False),
        out_shape=jax.ShapeDtypeStruct((b * t, d), BF16),
        grid_spec=pltpu.PrefetchScalarGridSpec(
            num_scalar_prefetch=1, grid=(b, 1), in_specs=in_specs,
            out_specs=pl.BlockSpec((t, d), lambda bi, c, sk: (bi, 0))),
        compiler_params=_params("arbitrary", "arbitrary"),
        name="attn_sample",
    )(sinks, qkv, cache_k, qkv, cache_v, qkv)


def _pool_kernel(x_ref, halo_ref, g_ref, w_ref, sc_ref, o_ref, st_ref, *, halo_is_x, pos0):
    si = pl.program_id(1)
    ts, d = x_ref.shape
    gw = d // len(POOL_WINDOWS)
    g = g_ref[...]
    un = _rms(x_ref[...], g)
    if halo_is_x:
        halo = jnp.where(si > 0, _rms(halo_ref[...], g), 0.0)
    else:
        halo = halo_ref[...]
    full = jnp.concatenate([halo, un], axis=0)
    pos = pos0 + si * ts + lax.broadcasted_iota(jnp.int32, (ts, 1), 0)
    for gi, win in enumerate(POOL_WINDOWS):
        lo, hi = gi * gw, (gi + 1) * gw
        acc = full[:, lo:hi]
        span = 1
        while span < win:
            acc = acc + pltpu.roll(acc, span, 0)
            span *= 2
        cnt = jnp.minimum(win, pos + 1).astype(F32)
        dlt = (acc[HALO_ROWS:] / cnt - un[:, lo:hi]).astype(BF16)
        y = jnp.dot(dlt, w_ref[gi], preferred_element_type=F32)
        o_ref[:, lo:hi] = (y * sc_ref[:, lo:hi]).astype(o_ref.dtype)
    st_ref[...] = full[ts:]


def _pool(x, halo, g, w, scale, b, s, *, halo_is_x, pos0):
    d = x.shape[1]
    ts = min(256, s)
    ns = s // ts
    hb = ts // HALO_ROWS
    if halo_is_x:
        halo_spec = pl.BlockSpec((HALO_ROWS, d), lambda bi, si: (jnp.maximum((bi * ns + si) * hb - 1, 0), 0))
    else:
        halo_spec = pl.BlockSpec((None, HALO_ROWS, d), lambda bi, si: (bi, 0, 0))
    ng, gw = w.shape[0], w.shape[1]
    return pl.pallas_call(
        functools.partial(_pool_kernel, halo_is_x=halo_is_x, pos0=pos0),
        out_shape=(jax.ShapeDtypeStruct((b * s, d), BF16), jax.ShapeDtypeStruct((b, HALO_ROWS, d), F32)),
        grid=(b, ns),
        in_specs=[pl.BlockSpec((ts, d), lambda bi, si: (bi * ns + si, 0)),
                  halo_spec,
                  pl.BlockSpec((1, d), lambda bi, si: (0, 0)),
                  pl.BlockSpec((ng, gw, gw), lambda bi, si: (0, 0, 0)),
                  pl.BlockSpec((1, d), lambda bi, si: (0, 0))],
        out_specs=(pl.BlockSpec((ts, d), lambda bi, si: (bi * ns + si, 0)),
                   pl.BlockSpec((None, HALO_ROWS, d), lambda bi, si: (bi, 0, 0))),
        compiler_params=_params("arbitrary", "arbitrary"),
        name="pool",
    )(x, halo, g.reshape(1, d), w, scale.reshape(1, d))


_MIX_R, _MIX_W, _MIX_K, _MIX_V, _MIX_A, _MIX_G = range(6)


def _rw_mix_kernel(x_ref, halo_ref, g_ref, mix_ref, w1_ref, a1_ref, g1_ref,
                   xr_ref, xk_ref, xv_ref, hw_ref, ha_ref, hg_ref, st_ref, *, halo_is_x):
    si = pl.program_id(1)
    g = g_ref[...]
    un = _rms(x_ref[...], g)
    n_h = halo_ref.shape[0]
    if halo_is_x:
        prev_row = jnp.where(si > 0, _rms(halo_ref[n_h - 1:n_h, :], g), 0.0)
    else:
        prev_row = halo_ref[n_h - 1:n_h, :]
    row = lax.broadcasted_iota(jnp.int32, (un.shape[0], 1), 0)
    prev = jnp.where(row == 0, prev_row, pltpu.roll(un, 1, 0))
    xx = prev - un
    mixed = lambda jm: (un + xx * mix_ref[jm:jm + 1, :]).astype(BF16)
    xr_ref[...] = mixed(_MIX_R)
    xk_ref[...] = mixed(_MIX_K)
    xv_ref[...] = mixed(_MIX_V)
    narrow = lambda jm, w_ref: jnp.dot(mixed(jm), w_ref[...], preferred_element_type=F32)
    hw_ref[...] = jnp.tanh(narrow(_MIX_W, w1_ref)).astype(BF16)
    ha_ref[...] = narrow(_MIX_A, a1_ref).astype(BF16)
    hg_ref[...] = jax.nn.sigmoid(narrow(_MIX_G, g1_ref)).astype(BF16)
    st_ref[...] = un[un.shape[0] - st_ref.shape[0]:]


def _rw_mix(x, halo, g, mix, w1, a1, g1, b, s, *, halo_is_x):
    d = x.shape[1]
    ts = min(256, s)
    ns = s // ts
    hr = 8
    if halo_is_x:
        halo_spec = pl.BlockSpec((hr, d), lambda bi, si: (jnp.maximum((bi * ns + si) * (ts // hr) - 1, 0), 0))
    else:
        halo_spec = pl.BlockSpec((None, hr, d), lambda bi, si: (bi, 0, 0))
    row_spec = pl.BlockSpec((ts, d), lambda bi, si: (bi * ns + si, 0))
    lows = [w1, a1, g1]
    ranks = [w[0].shape[2] for w in lows]
    return pl.pallas_call(
        functools.partial(_rw_mix_kernel, halo_is_x=halo_is_x),
        out_shape=tuple([jax.ShapeDtypeStruct((b * s, d), BF16)] * 3
                        + [jax.ShapeDtypeStruct((b * s, r), BF16) for r in ranks]
                        + [jax.ShapeDtypeStruct((b, hr, d), F32)]),
        grid=(b, ns),
        in_specs=[row_spec, halo_spec,
                  pl.BlockSpec((1, d), lambda bi, si: (0, 0)),
                  pl.BlockSpec((8, d), lambda bi, si: (0, 0))]
                 + [pl.BlockSpec((None, d, r), functools.partial(lambda bi, si, l: (l, 0, 0), l=w[1]))
                    for w, r in zip(lows, ranks)],
        out_specs=tuple([row_spec] * 3
                        + [pl.BlockSpec((ts, r), lambda bi, si: (bi * ns + si, 0)) for r in ranks]
                        + [pl.BlockSpec((None, hr, d), lambda bi, si: (bi, 0, 0))]),
        compiler_params=_params("arbitrary", "arbitrary"),
        name="rw_mix",
    )(x, halo, g.reshape(1, d), jnp.pad(mix, ((0, 2), (0, 0))), *[w[0] for w in lows])


def _dot_nt(a, b):
    return lax.dot_general(a, b, (((1,), (1,)), ((), ())), preferred_element_type=F32)


def _pair_dots(lhs, rhs, nt=False):
    single = (lambda a, b: _dot_nt(a, b)) if nt else (lambda a, b: jnp.dot(a, b, preferred_element_type=F32))
    n_out = rhs[0].shape[0] if nt else rhs[0].shape[1]
    if lhs[0].shape[1] % LANES or n_out % LANES:
        return [single(a, b) for a, b in zip(lhs, rhs)]
    outs = []
    for g in range(0, len(lhs) - 1, 2):
        zero = jnp.zeros_like(rhs[g])
        both = jnp.concatenate([jnp.concatenate([rhs[g], zero], axis=1),
                                jnp.concatenate([zero, rhs[g + 1]], axis=1)], axis=0)
        o = single(jnp.concatenate([lhs[g], lhs[g + 1]], axis=1), both)
        outs += [o[:, :n_out], o[:, n_out:]]
    if len(lhs) % 2:
        outs.append(single(lhs[-1], rhs[-1]))
    return outs


def _rw_chunk(r, ld, k, v, kk, kka, zs):
    c, width = r.shape
    n = RW_HEAD
    tiles = [slice(g * LANES, (g + 1) * LANES) for g in range(width // LANES)]
    lanes = lambda parts: jnp.concatenate(parts, axis=1) if len(parts) > 1 else parts[0]
    row = lax.broadcasted_iota(jnp.int32, (c, 2 * c), 0)
    col = lax.broadcasted_iota(jnp.int32, (c, 2 * c), 1)
    s_idx = col % c
    first = col < c
    strict = s_idx < row
    incl = s_idx <= row
    head0 = lax.broadcasted_iota(jnp.int32, (c, width), 1) % LANES < n

    tri = (lax.broadcasted_iota(jnp.int32, (c, 3 * c), 1) % c <= lax.broadcasted_iota(jnp.int32, (c, 3 * c), 0))
    hi = ld.astype(BF16)
    mid = (ld - hi.astype(F32)).astype(BF16)
    lo = (ld - hi.astype(F32) - mid.astype(F32)).astype(BF16)
    cum = jnp.dot(tri.astype(BF16), jnp.concatenate([hi, mid, lo], axis=0), preferred_element_type=F32)

    e_cum = jnp.exp(cum)
    e_inv = jnp.exp(-cum)
    at = -kk * jnp.exp(cum - ld)
    rt = r * e_cum
    bt = kka * e_inv
    kt = k * e_inv
    w_last = e_cum[c - 1:c, :]

    zero = jnp.zeros_like(at)
    a0r0r1 = jnp.concatenate([jnp.where(head0, at, zero), jnp.where(head0, rt, zero),
                              jnp.where(head0, zero, rt)], axis=0).astype(BF16)
    a1 = jnp.where(head0, zero, at).astype(BF16)
    ar = jnp.concatenate([at, rt], axis=0).astype(BF16)
    bk = jnp.concatenate([bt, kt], axis=0).astype(BF16)
    kb = jnp.concatenate([kt, bt], axis=0).astype(BF16)
    vb = v.astype(BF16)
    vv = jnp.concatenate([vb, vb], axis=0)

    per = lambda a: [a[:, t] for t in tiles]
    g0 = _pair_dots(per(a0r0r1), per(bk), nt=True)
    g1a = _pair_dots(per(a1), per(kb), nt=True)
    g0a, g0r, g1r = [g[:c] for g in g0], [g[c:2 * c] for g in g0], [g[2 * c:] for g in g0]
    l2 = [jnp.where(strict, jnp.where(first, a, b), 0.0) for a, b in zip(g0a, g1a)]
    mak = [jnp.concatenate([jnp.where(strict & ~first, a, 0.0), jnp.where(strict & first, b, 0.0)],
                           axis=0).astype(BF16) for a, b in zip(g0a, g1a)]
    mv = lanes(_pair_dots(mak, per(vv)))
    p = lanes(_pair_dots(per(ar), [z.astype(BF16) for z in zs]))
    u0 = p[:c] + jnp.where(head0, mv[:c], mv[c:])

    sb = SOLVE_BLOCK
    nb = c // sb
    spread = (lax.broadcasted_iota(jnp.int32, (2 * c, LANES), 0) // c
              == lax.broadcasted_iota(jnp.int32, (2 * c, LANES), 1) // n).astype(BF16)
    lane_b = lax.broadcasted_iota(jnp.int32, (sb, 2 * c), 1) % c
    first_b = lax.broadcasted_iota(jnp.int32, (sb, 2 * c), 1) < c
    head0_b = lax.broadcasted_iota(jnp.int32, (sb, width), 1) % LANES < n
    cols = [jnp.concatenate([jnp.where(lane_b == blk * sb + s, l2g[blk * sb:(blk + 1) * sb], 0.0)
                             for blk in range(nb) for s in range(sb - 1)], axis=0).astype(BF16) for l2g in l2]
    lcol = lanes(_pair_dots(cols, [spread] * len(cols)))
    solved = []
    for blk in range(nb):
        ub = u0[blk * sb:(blk + 1) * sb]
        if blk > 0:
            done = jnp.concatenate(solved + [jnp.zeros((c - blk * sb, width), F32)], axis=0).astype(BF16)
            done = jnp.concatenate([done, done], axis=0)
            bands = [l2g[blk * sb:(blk + 1) * sb] for l2g in l2]
            lhs = [jnp.concatenate([jnp.where(first_b, band, 0.0), jnp.where(first_b, 0.0, band)],
                                   axis=0).astype(BF16) for band in bands]
            off = lanes(_pair_dots(lhs, per(done)))
            ub = ub + jnp.where(head0_b, off[:sb], off[sb:])
        for s in range(sb - 1):
            i = blk * (sb - 1) + s
            ub = ub + lcol[i * sb:(i + 1) * sb] * ub[s:s + 1, :]
        solved.append(ub)
    u = jnp.concatenate(solved, axis=0)

    uv = jnp.concatenate([u, v], axis=0).astype(BF16)
    bkw = jnp.concatenate([bt * w_last, kt * w_last], axis=0)
    same_head = (lax.broadcasted_iota(jnp.int32, (LANES, LANES), 0) // n
                 == lax.broadcasted_iota(jnp.int32, (LANES, LANES), 1) // n)
    nmat = [jnp.concatenate([jnp.where(incl, a, 0.0), jnp.where(incl, b, 0.0), bkw[:, t].T], axis=0).astype(BF16)
            for a, b, t in zip(g0r, g1r, tiles)]
    both = _pair_dots(nmat, per(uv))
    ny = lanes([o[:2 * c] for o in both])
    z_new = [z * jnp.broadcast_to(w_last[:, t], (LANES, LANES)).T + jnp.where(same_head, o[2 * c:], 0.0)
             for t, z, o in zip(tiles, zs, both)]
    return p[c:] + jnp.where(head0, ny[:c], ny[c:]), z_new


def _seg_sum_wide(p, jj):
    parts = [_seg_sum(p[:, g * LANES:(g + 1) * LANES], jj) for g in range(p.shape[1] // LANES)]
    return jnp.concatenate(parts, axis=1) if len(parts) > 1 else parts[0]


_PV_W0, _PV_A0, _PV_KK, _PV_KA, _PV_RK, _PV_LN_G, _PV_LN_B = range(7)


def _rw_scan_kernel(r_ref, k_ref, v_ref, hw_ref, ha_ref, hg_ref, w2_ref, a2_ref, g2_ref, pv_ref, s0_ref,
                    o_ref, sT_ref, z_ref):
    @pl.when(pl.program_id(2) == 0)
    def _():
        z_ref[...] = s0_ref[...]

    pv = lambda i: pv_ref[i:i + 1, :]
    wide = lambda h_ref, w_ref: jnp.dot(h_ref[...], w_ref[...], preferred_element_type=F32)
    jj = _seg_ones(2 * LANES)
    r = r_ref[...].astype(F32)
    k = k_ref[...].astype(F32)
    v = v_ref[...].astype(F32)
    a = jax.nn.sigmoid(pv(_PV_A0) + wide(ha_ref, a2_ref))
    ld = LOG_DECAY_MIN * jax.nn.sigmoid(pv(_PV_W0) + wide(hw_ref, w2_ref))
    k2 = k * (1.0 + (a - 1.0) * pv(_PV_KA))
    kkr = k * pv(_PV_KK)
    kk = kkr / jnp.maximum(jnp.sqrt(_seg_sum_wide(kkr * kkr, jj)), 1e-12)

    tg = z_ref.shape[0]
    y, zs = _rw_chunk(r, ld, k2, v, kk, kk * a, [z_ref[g] for g in range(tg)])
    for g in range(tg):
        z_ref[g] = zs[g]
    sT_ref[...] = z_ref[...]

    yc = y - _seg_sum_wide(y, jj) * (1.0 / RW_HEAD)
    var = _seg_sum_wide(yc * yc, jj) * (1.0 / RW_HEAD)
    o = yc * lax.rsqrt(var + LNX_EPS) * pv(_PV_LN_G) + pv(_PV_LN_B)
    bonus = _seg_sum_wide(r * k2 * pv(_PV_RK), jj) * v
    o_ref[...] = ((o + bonus) * wide(hg_ref, g2_ref)).astype(o_ref.dtype)


def _rw_scan(r, k, v, narrow, second, pvec, s0, b, s):
    d = r.shape[1]
    ntile = d // LANES
    tg = min(RW_TILES_PER_STEP, ntile)
    tc = min(RW_CHUNK, s)
    nch = s // tc
    seq = pl.BlockSpec((tc, tg * LANES), lambda bi, gi, ci: (bi * nch + ci, gi))
    st = pl.BlockSpec((None, tg, LANES, LANES), lambda bi, gi, ci: (bi, gi, 0, 0))
    ranks = [h.shape[1] for h in narrow]
    return pl.pallas_call(
        _rw_scan_kernel,
        out_shape=(jax.ShapeDtypeStruct((b * s, d), BF16), jax.ShapeDtypeStruct(s0.shape, F32)),
        grid=(b, ntile // tg, nch),
        in_specs=[seq] * 3
                 + [pl.BlockSpec((tc, rk), lambda bi, gi, ci: (bi * nch + ci, 0)) for rk in ranks]
                 + [pl.BlockSpec((None, rk, tg * LANES), functools.partial(lambda bi, gi, ci, l: (l, 0, gi), l=w[1]))
                    for w, rk in zip(second, ranks)]
                 + [pl.BlockSpec((8, tg * LANES), lambda bi, gi, ci: (0, gi)), st],
        out_specs=(seq, st),
        scratch_shapes=[pltpu.VMEM((tg, LANES, LANES), F32)],
        compiler_params=_params("arbitrary", "arbitrary", "arbitrary"),
        name="rw_scan",
    )(r, k, v, *narrow, *[w[0] for w in second], pvec, s0)


def _gelu_tanh(x):
    return 0.5 * x * (1.0 + jnp.tanh(math.sqrt(2.0 / math.pi) * (x + 0.044715 * (x * x * x))))


def _conv_taps(full, cw, cb):
    return cb + full * cw[2:3] + pltpu.roll(full, 1, 0) * cw[1:2] + pltpu.roll(full, 2, 0) * cw[0:1]


def _conv_act_kernel(u_ref, h_ref, cw_ref, cb_ref, o_ref, *, tf):
    f = o_ref.shape[1]
    for j in range(f // tf):
        def conv(cols):
            full = jnp.concatenate([h_ref[:, cols], u_ref[:, cols].astype(F32)], axis=0)
            return _conv_taps(full, cw_ref[:, cols], cb_ref[:, cols])[HALO_ROWS:]
        gate = conv(slice(j * tf, (j + 1) * tf))
        val = conv(slice(f + j * tf, f + (j + 1) * tf))
        o_ref[:, j * tf:(j + 1) * tf] = (_gelu_tanh(gate) * val).astype(o_ref.dtype)


def _conv_act(up, halo, cw, cb, layer, b, s, tf):
    f = up.shape[1] // 2
    return pl.pallas_call(
        functools.partial(_conv_act_kernel, tf=tf),
        out_shape=jax.ShapeDtypeStruct((b * s, f), BF16),
        grid=(b,),
        in_specs=[pl.BlockSpec((s, 2 * f), lambda bi: (bi, 0)),
                  pl.BlockSpec((None, HALO_ROWS, 2 * f), lambda bi: (bi, 0, 0)),
                  pl.BlockSpec((None, 8, 2 * f), lambda bi: (layer, 0, 0)),
                  pl.BlockSpec((None, 1, 2 * f), lambda bi: (layer, 0, 0))],
        out_specs=pl.BlockSpec((s, f), lambda bi: (bi, 0)),
        compiler_params=_params("arbitrary"),
        name="conv_act",
    )(up, halo, cw, cb)


def _ffn_up_kernel(x_ref, wg_ref, wv_ref, cwg_ref, cwv_ref, cbg_ref, cbv_ref, *refs, rows, cast_next):
    if cast_next:
        nu_ref, nd_ref, h_ref, tg_ref, tv_ref, nu_o, nd_o, up_ref = refs
        nu_o[...] = nu_ref[...].astype(nu_o.dtype)
        nd_o[...] = nd_ref[...].astype(nd_o.dtype)
    else:
        h_ref, tg_ref, tv_ref, up_ref = refs
    s = x_ref.shape[0]
    tf = h_ref.shape[1]
    n_pass = s // rows

    def project(k):
        xk = x_ref[k * rows:(k + 1) * rows, :]
        up_ref[k % 2, 0, 8:, :] = jnp.dot(xk, wg_ref[...], preferred_element_type=F32)
        up_ref[k % 2, 1, 8:, :] = jnp.dot(xk, wv_ref[...], preferred_element_type=F32)

    def conv(slot, part, cw_ref, cb_ref):
        buf = up_ref.at[slot, part]
        return (cb_ref[...] + buf[8:8 + rows, :] * cw_ref[2:3, :] + buf[7:7 + rows, :] * cw_ref[1:2, :]
                + buf[6:6 + rows, :] * cw_ref[0:1, :])

    zero = jnp.zeros((8, tf), F32)
    up_ref[0, 0, 0:8, :] = zero
    up_ref[0, 1, 0:8, :] = zero
    project(0)
    for k in range(n_pass):
        slot = k % 2
        if k + 1 < n_pass:
            project(k + 1)
            up_ref[1 - slot, 0, 0:8, :] = up_ref[slot, 0, rows:rows + 8, :]
            up_ref[1 - slot, 1, 0:8, :] = up_ref[slot, 1, rows:rows + 8, :]
        gate = conv(slot, 0, cwg_ref, cbg_ref)
        val = conv(slot, 1, cwv_ref, cbv_ref)
        h_ref[k * rows:(k + 1) * rows, :] = (_gelu_tanh(gate) * val).astype(h_ref.dtype)
    last = (n_pass - 1) % 2
    tg_ref[...] = up_ref[last, 0, rows:rows + 8, :]
    tv_ref[...] = up_ref[last, 1, rows:rows + 8, :]


def _ffn_up(xn, w, cw, cb, layer, b, s, cast_next=None):
    d = xn.shape[1]
    f = w.shape[2] // 2
    tf = _pick(f, (FFN_COLS, LANES))
    nf = f // tf
    rows = min(FFN_ROWS, s)

    def cols(block_rows, off, lead):
        return pl.BlockSpec((None, block_rows, tf), lambda bi, j: (lead, 0, off + j))

    tail = pl.BlockSpec((None, 8, tf), lambda bi, j: (bi, 0, j))
    in_specs = [pl.BlockSpec((s, d), lambda bi, j: (bi, 0)), cols(d, 0, 0), cols(d, nf, 0),
                cols(8, 0, layer), cols(8, nf, layer), cols(1, 0, layer), cols(1, nf, layer)]
    args = [xn, w, w, cw, cw, cb, cb]
    out_shape = [jax.ShapeDtypeStruct((b * s, f), BF16), jax.ShapeDtypeStruct((b, 8, f), F32),
                 jax.ShapeDtypeStruct((b, 8, f), F32)]
    out_specs = [pl.BlockSpec((s, tf), lambda bi, j: (bi, j)), tail, tail]
    if cast_next is not None:
        wu32, wd32, nxt = cast_next
        assert d % b == 0 and (d // b) % HALO_ROWS == 0 and (2 * tf) % LANES == 0
        for stack, blk, imap in ((wu32, (None, d // b, 2 * tf), lambda bi, j: (nxt, bi, j)),
                                 (wd32, (None, tf, d // b), lambda bi, j: (nxt, j, bi))):
            in_specs.append(pl.BlockSpec(blk, imap))
            args.append(stack)
            out_shape.append(jax.ShapeDtypeStruct((1,) + stack.shape[1:], BF16))
            out_specs.append(pl.BlockSpec(blk, functools.partial(lambda bi, j, m: (0,) + m(bi, j)[1:], m=imap)))
    res = pl.pallas_call(
        functools.partial(_ffn_up_kernel, rows=rows, cast_next=cast_next is not None),
        out_shape=tuple(out_shape),
        grid=(b, nf),
        in_specs=in_specs,
        out_specs=tuple(out_specs),
        scratch_shapes=[pltpu.VMEM((2, 2, 8 + rows, tf), F32)],
        compiler_params=_params("arbitrary", "arbitrary"),
        name="ffn_up",
    )(*args)
    return (res[0], jnp.concatenate([res[1], res[2]], axis=2)) + tuple(res[3:])


def _rope_tables(pos):
    half = HEAD_DIM // 2
    inv = ROPE_THETA ** (-jnp.arange(half, dtype=F32) / half)
    ang = pos.astype(F32)[:, None] * inv[None, :]
    cos, sin = jnp.cos(ang), jnp.sin(ang)
    return jnp.concatenate([cos, cos], axis=1), jnp.concatenate([-sin, sin], axis=1)


def _state_to_tiles(wkv):
    b, h, n, _ = wkv.shape
    t = wkv.reshape(b, h // 2, 2, n, n).transpose(0, 1, 2, 4, 3)
    z = t[:, :, :, :, None, :] * jnp.eye(2, dtype=wkv.dtype)[None, None, :, None, :, None]
    return z.reshape(b, h // 2, 2 * n, 2 * n)


def _tiles_to_state(z, h):
    b = z.shape[0]
    n = RW_HEAD
    z = z.reshape(b, h // 2, 2, n, 2, n)
    t = jnp.stack([z[:, :, 0, :, 0, :], z[:, :, 1, :, 1, :]], axis=2)
    return t.transpose(0, 1, 2, 4, 3).reshape(b, h, n, n)


def _attn_layer(xn, b, s, w_qkv, w_o, sinks, pos, keep, cache=None):
    d = xn.shape[1]
    kv_dim = d // N_GROUPS
    cos, sin = _rope_tables(pos)
    cos, sin = jnp.tile(cos, (b, 1)), jnp.tile(sin, (b, 1))
    qkv = _mm(xn, w_qkv, BF16, rope=(cos, sin, d + kv_dim))
    if cache is None:
        o = _attn_prompt(qkv, sinks, b, s, d)
    else:
        o = _attn_sample(qkv, cache[0].reshape(b, -1, kv_dim), cache[1].reshape(b, -1, kv_dim), sinks, b, s, d)
    kv = qkv.reshape(b, s, d + 2 * kv_dim)[:, s - keep:, d:].astype(F32)
    k3 = kv[:, :, :kv_dim].reshape(b, keep, kv_dim // HEAD_DIM, HEAD_DIM)
    v3 = kv[:, :, kv_dim:].reshape(b, keep, kv_dim // HEAD_DIM, HEAD_DIM)
    return _mm(o, w_o, BF16), k3, v3


def _rwkv_layer(x, b, s, g, shift, wkv, p):
    d = x.shape[1]
    heads = d // RW_HEAD
    first = (p["w1"], p["a1"], p["g1"])
    if shift is None:
        mixed = _rw_mix(x, x, g, p["mix"], *first, b, s, halo_is_x=True)
        s0 = jnp.zeros((b, heads // 2, 2 * RW_HEAD, 2 * RW_HEAD), F32)
    else:
        halo = jnp.pad(shift.reshape(b, 1, d), ((0, 0), (7, 0), (0, 0)))
        mixed = _rw_mix(x, halo, g, p["mix"], *first, b, s, halo_is_x=False)
        s0 = _state_to_tiles(wkv.astype(F32))
    xr, xk, xv, hw, ha, hg, tail = mixed
    r = _mm(xr, p["w_r"], BF16)
    k = _mm(xk, p["w_k"], BF16)
    v = _mm(xv, p["w_v"], BF16)
    pvec = jnp.stack([p["w0"], p["a0"], p["k_k"], p["k_a"], p["r_k"].reshape(-1), p["lnx"][0], p["lnx"][1],
                      jnp.zeros_like(p["w0"])])
    o, s_t = _rw_scan(r, k, v, (hw, ha, hg), (p["w2"], p["a2"], p["g2"]), pvec, s0, b, s)
    return _mm(o, p["w_o"], BF16), tail[:, 7:8, :], _tiles_to_state(s_t, heads)


def _ffn_layer(xn, b, s, hist, w_up, w_down, cw, cb, layer, cast_next=None):
    f = w_down.shape[1]
    nxt = ()
    if hist is None:
        h, tail, *nxt = _ffn_up(xn, w_up, cw, cb, layer, b, s, cast_next)
        tail = tail[:, 8 - (CONV_W - 1):, :]
    else:
        up = _mm(xn, (w_up, 0), BF16)
        halo = jnp.pad(hist, ((0, 0), (HALO_ROWS - (CONV_W - 1), 0), (0, 0)))
        h = _conv_act(up, halo, cw, cb, layer, b, s, _pick(f, (FFN_COLS, LANES)))
        tail = up.reshape(b, s, 2 * f)[:, s - (CONV_W - 1):, :].astype(F32)
    return _mm(h, (w_down, 0), BF16), tail, tuple(nxt)


def kernel(x_prompt, x_sample, cache_k, cache_v, state_pool, state_shift, state_wkv, state_conv, norm_g, attn_w_qkv, attn_w_o, attn_sinks, pool_w, pool_scale, rw_mix, rw_w_r, rw_w_k, rw_w_v, rw_w_o, rw_w0, rw_w1, rw_w2, rw_a0, rw_a1, rw_a2, rw_g1, rw_g2, rw_k_k, rw_k_a, rw_r_k, rw_lnx, ffn_w_up, ffn_conv_w, ffn_conv_b, ffn_w_down):
    bp, sp, d = x_prompt.shape
    bs, ss, _ = x_sample.shape
    depth = norm_g.shape[0]
    window = cache_k.shape[2]
    xp = x_prompt.reshape(bp * sp, d)
    xs = x_sample.reshape(bs * ss, d)
    outs = {n: [] for n in ("kp", "vp", "kn", "vn", "poolp", "pools", "shp", "shs", "wkvp", "wkvs", "convp", "convs")}
    xnp = xns = None
    bf = lambda a: a.astype(BF16)
    attn_w_qkv, attn_w_o = bf(attn_w_qkv), bf(attn_w_o)
    w_up, w_down = bf(ffn_w_up[:1]), bf(ffn_w_down[:1])
    rw_w = {n: bf(a) for n, a in dict(w_r=rw_w_r, w_k=rw_w_k, w_v=rw_w_v, w_o=rw_w_o, w1=rw_w1, w2=rw_w2,
                                      a1=rw_a1, a2=rw_a2, g1=rw_g1, g2=rw_g2).items()}
    ffn_cw = jnp.pad(ffn_conv_w, ((0, 0), (0, 8 - CONV_W), (0, 0)))
    ffn_cb = ffn_conv_b[:, None, :]
    for i in range(depth):
        kind, j = i % 3, i // 3
        g = norm_g[i]
        g_after = norm_g[i + 1, 0] if i + 1 < depth else None
        if kind == 0:
            wq, wo = (attn_w_qkv, j), (attn_w_o, j)
            if xnp is None:
                xnp, xns = _norm(xp, g[0]), _norm(xs, g[0])
            mp, kp, vp = _attn_layer(xnp, bp, sp, wq, wo, attn_sinks[j], jnp.arange(sp), window)
            ms, kn, vn = _attn_layer(xns, bs, ss, wq, wo, attn_sinks[j], PAST_LEN + jnp.arange(ss), ss,
                                     cache=(cache_k[j], cache_v[j]))
            outs["kp"].append(kp)
            outs["vp"].append(vp)
            outs["kn"].append(kn)
            outs["vn"].append(vn)
        elif kind == 1:
            wp = pool_w[j].astype(BF16)
            mp, hp = _pool(xp, xp, g[0], wp, pool_scale[j], bp, sp, halo_is_x=True, pos0=0)
            halo = jnp.pad(state_pool[j], ((0, 0), (HALO_ROWS - POOL_HIST, 0), (0, 0)))
            ms, hs = _pool(xs, halo, g[0], wp, pool_scale[j], bs, ss, halo_is_x=False, pos0=PAST_LEN)
            outs["poolp"].append(hp[:, HALO_ROWS - POOL_HIST:])
            outs["pools"].append(hs[:, HALO_ROWS - POOL_HIST:])
        else:
            p = dict(mix=rw_mix[j], w0=rw_w0[j], a0=rw_a0[j], k_k=rw_k_k[j], k_a=rw_k_a[j], r_k=rw_r_k[j],
                     lnx=rw_lnx[j], **{n: (a, j) for n, a in rw_w.items()})
            mp, shp, wkvp = _rwkv_layer(xp, bp, sp, g[0], None, None, p)
            ms, shs, wkvs = _rwkv_layer(xs, bs, ss, g[0], state_shift[j], state_wkv[j], p)
            outs["shp"].append(shp)
            outs["shs"].append(shs)
            outs["wkvp"].append(wkvp)
            outs["wkvs"].append(wkvs)
        xp, xnp = _add_norm(xp, mp, g[1], g[2])
        xs, xns = _add_norm(xs, ms, g[1], g[2])
        cast_next = (ffn_w_up, ffn_w_down, i + 1) if i + 1 < depth else None
        fp, cp, w_next = _ffn_layer(xnp, bp, sp, None, w_up, w_down, ffn_cw, ffn_cb, i, cast_next)
        fs, cs, _ = _ffn_layer(xns, bs, ss, state_conv[i], w_up, w_down, ffn_cw, ffn_cb, i)
        if w_next:
            w_up, w_down = w_next
        outs["convp"].append(cp)
        outs["convs"].append(cs)
        if g_after is not None and (i + 1) % 3 == 0:
            xp, xnp = _add_norm(xp, fp, g[3], g_after)
            xs, xns = _add_norm(xs, fs, g[3], g_after)
        else:
            xp, xnp = _add_norm(xp, fp, g[3]), None
            xs, xns = _add_norm(xs, fs, g[3]), None
    st = lambda n: jnp.stack(outs[n])
    return (xp.reshape(bp, sp, d), xs.reshape(bs, ss, d),
            st("kp"), st("vp"), st("poolp"), st("shp"), st("wkvp"), st("convp"),
            st("kn"), st("vn"), st("pools"), st("shs"), st("wkvs"), st("convs"))
```
